```python
import jax, jax.numpy as jnp
from jax import lax
import numpy as np

D_MODEL = 2048
BATCH = 2
SEQ = 16384
DEPTH = 1
DEC_BATCH = 8
DEC_SEQ = 32
PAST_LEN = 2048

CHUNK = 64
M_HEADS = 4
M_DQK = 256
M_DV = 256
M_WIDTH = M_HEADS * M_DV
F_HEADS = 8
F_HEAD_DIM = 128
F_WIDTH = F_HEADS * F_HEAD_DIM
Q_BLOCK = 128
D_FF = 4 * D_MODEL
EPS = 1e-6
IN_SPLITS = (M_HEADS * M_DQK, M_HEADS * M_DQK, M_WIDTH, M_WIDTH, M_HEADS, M_HEADS,
             F_WIDTH, F_WIDTH, F_WIDTH, F_HEADS, D_MODEL, D_MODEL)
D_IN = sum(IN_SPLITS)

kernel_name = 'hybrid_mlstm_fox_streaming_step'


def rmsnorm(x, g):
    x32 = x.astype(jnp.float32)
    y = x32 * lax.rsqrt(jnp.mean(x32 * x32, axis=-1, keepdims=True) + EPS)
    return (y * g.astype(jnp.float32)).astype(x.dtype)


def split_in(z):
    offs = np.cumsum(IN_SPLITS)[:-1].tolist()
    return jnp.split(z, offs, axis=-1)


def mlstm_chunk(carry, q, k, v, ig, lf):
    c0, n0, m0 = carry
    L = q.shape[1]
    b = jnp.cumsum(lf, axis=1).transpose(0, 2, 1)
    ih = ig.transpose(0, 2, 1)
    causal = jnp.tril(jnp.ones((L, L), dtype=bool))
    log_d = jnp.where(causal, b[..., :, None] - b[..., None, :] + ih[..., None, :], -jnp.inf)
    log_inter = b + m0[..., None]
    m = jnp.maximum(log_inter, jnp.max(log_d, axis=-1))
    w_intra = jnp.exp(log_d - m[..., None])
    w_inter = jnp.exp(log_inter - m)
    s = jnp.einsum('bthd,bshd->bhts', q, k) * w_intra
    num = (jnp.einsum('bhts,bshe->bthe', s, v)
           + jnp.einsum('bthd,bhde->bthe', q, c0) * w_inter.transpose(0, 2, 1)[..., None])
    den = jnp.sum(s, axis=-1) + w_inter * jnp.einsum('bthd,bhd->bht', q, n0)
    den = jnp.maximum(jnp.abs(den), jnp.exp(-m))
    h = num / den.transpose(0, 2, 1)[..., None]
    m_end = m[..., -1]
    w_state = jnp.exp(b[..., -1] + m0 - m_end)
    w_tok = jnp.exp(b[..., -1:] - b + ih - m_end[..., None])
    c = w_state[..., None, None] * c0 + jnp.einsum('bhs,bshd,bshe->bhde', w_tok, k, v)
    n = w_state[..., None] * n0 + jnp.einsum('bhs,bshd->bhd', w_tok, k)
    return (c, n, m_end), h


def mlstm_run(state, q, k, v, ig, lf):
    B, L = q.shape[:2]
    if L <= CHUNK:
        return mlstm_chunk(state, q, k, v, ig, lf)
    nc = L // CHUNK

    def to_blocks(a):
        return jnp.moveaxis(a.reshape((B, nc, CHUNK) + a.shape[2:]), 1, 0)

    def step(carry, xs):
        return mlstm_chunk(carry, *xs)

    state, h = lax.scan(step, state, (to_blocks(q), to_blocks(k), to_blocks(v), to_blocks(ig), to_blocks(lf)))
    h = jnp.moveaxis(h, 0, 1).reshape((B, L) + h.shape[3:])
    return state, h


def mlstm_mixer(mq, mk, mv, mo, mi, mf, b_i, b_f, norm_h, state):
    B, L = mq.shape[:2]
    f32 = jnp.float32
    q = mq.reshape(B, L, M_HEADS, M_DQK).astype(f32)
    k = mk.reshape(B, L, M_HEADS, M_DQK).astype(f32) * (M_DQK ** -0.5)
    v = mv.reshape(B, L, M_HEADS, M_DV).astype(f32)
    ig = mi.astype(f32) + b_i.astype(f32)
    lf = jax.nn.log_sigmoid(mf.astype(f32) + b_f.astype(f32))
    c0, n0, m0 = state
    new_state, h = mlstm_run((c0.astype(f32), n0.astype(f32), m0.astype(f32)), q, k, v, ig, lf)
    h = h * lax.rsqrt(jnp.mean(h * h, axis=-1, keepdims=True) + EPS)
    h = h.reshape(B, L, M_WIDTH) * norm_h.astype(f32) * jax.nn.sigmoid(mo.astype(f32))
    return h.astype(mq.dtype), new_state


def fox_attend(q, k, v, f_q, f_k, q_pos, k_pos):
    s = jnp.einsum('bqhd,bkhd->bhqk', q, k) * (F_HEAD_DIM ** -0.5)
    s = s + f_q.transpose(0, 2, 1)[..., :, None] - f_k.transpose(0, 2, 1)[..., None, :]
    s = jnp.where(k_pos[None, :] <= q_pos[:, None], s, -jnp.inf)
    p = jax.nn.softmax(s, axis=-1)
    return jnp.einsum('bhqk,bkhd->bqhd', p, v)


def fox_mixer(fq, fk, fv, ff, b_f, past):
    B, L = fq.shape[:2]
    f32 = jnp.float32
    q = fq.reshape(B, L, F_HEADS, F_HEAD_DIM)
    k = fk.reshape(B, L, F_HEADS, F_HEAD_DIM)
    v = fv.reshape(B, L, F_HEADS, F_HEAD_DIM)
    logf = jax.nn.log_sigmoid(ff.astype(f32) + b_f.astype(f32))
    if past is None:
        cum = jnp.cumsum(logf, axis=1)
        nb = L // Q_BLOCK
        k32, v32 = k.astype(f32), v.astype(f32)
        q_blocks = jnp.moveaxis(q.astype(f32).reshape(B, nb, Q_BLOCK, F_HEADS, F_HEAD_DIM), 1, 0)
        f_blocks = jnp.moveaxis(cum.reshape(B, nb, Q_BLOCK, F_HEADS), 1, 0)
        pos_blocks = jnp.arange(L).reshape(nb, Q_BLOCK)
        key_pos = jnp.arange(L)

        def block(args):
            qi, fi, pi = args
            return fox_attend(qi, k32, v32, fi, cum, pi, key_pos)

        out = lax.map(block, (q_blocks, f_blocks, pos_blocks))
        out = jnp.moveaxis(out, 0, 1).reshape(B, L, F_WIDTH)
    else:
        ck, cv, clf = past
        P = ck.shape[1]
        k_all = jnp.concatenate([ck.astype(f32), k.astype(f32)], axis=1)
        v_all = jnp.concatenate([cv.astype(f32), v.astype(f32)], axis=1)
        cum = jnp.cumsum(jnp.concatenate([clf.astype(f32), logf], axis=1), axis=1)
        out = fox_attend(q.astype(f32), k_all, v_all, cum[:, P:], cum,
                         P + jnp.arange(L), jnp.arange(P + L)).reshape(B, L, F_WIDTH)
    return out.astype(fq.dtype), k, v, logf


def trunk_layer(x, mstate, fox_past, norm_mix, w_in, b_mlstm_i, b_mlstm_f, b_fox_f, norm_mlstm_h,
                w_branch_a, w_branch_b, w_out, norm_ffn, w_up, w_down):
    xn = rmsnorm(x, norm_mix)
    z = jnp.einsum('bld,de->ble', xn, w_in)
    mq, mk, mv, mo, mi, mf, fq, fk, fv, ff, ga, gb = split_in(z)
    h_a, (c_new, n_new, m_new) = mlstm_mixer(mq, mk, mv, mo, mi, mf, b_mlstm_i, b_mlstm_f, norm_mlstm_h, mstate)
    h_b, k_rows, v_rows, logf_rows = fox_mixer(fq, fk, fv, ff, b_fox_f, fox_past)
    merged = (jax.nn.sigmoid(ga) * jnp.einsum('blc,cd->bld', h_a, w_branch_a)
              + jax.nn.sigmoid(gb) * jnp.einsum('blc,cd->bld', h_b, w_branch_b))
    x = x + jnp.einsum('bld,de->ble', merged, w_out)
    hn = rmsnorm(x, norm_ffn)
    u = jax.nn.relu(jnp.einsum('bld,df->blf', hn, w_up))
    x = x + jnp.einsum('blf,fd->bld', u * u, w_down)
    return x, (k_rows, v_rows, logf_rows, c_new, n_new, m_new)


def setup_inputs(seed: int = 0) -> dict:
    key = jax.random.key(seed)
    ks = jax.random.split(key, 24)
    f32 = jnp.float32

    def nrm(k, shape, scale):
        return jax.random.normal(k, shape, f32) * scale

    return {
        'x_prompt': nrm(ks[0], (BATCH, SEQ, D_MODEL), 1.0),
        'x_sample': nrm(ks[1], (DEC_BATCH, DEC_SEQ, D_MODEL), 1.0),
        'cache_fox_k': nrm(ks[2], (DEPTH, DEC_BATCH, PAST_LEN, F_HEADS, F_HEAD_DIM), 1.0),
        'cache_fox_v': nrm(ks[3], (DEPTH, DEC_BATCH, PAST_LEN, F_HEADS, F_HEAD_DIM), 1.0),
        'cache_fox_logf': jax.nn.log_sigmoid(3.0 + nrm(ks[4], (DEPTH, DEC_BATCH, PAST_LEN, F_HEADS), 1.0)),
        'state_mlstm_c': nrm(ks[5], (DEPTH, DEC_BATCH, M_HEADS, M_DQK, M_DV), 0.05),
        'state_mlstm_n': nrm(ks[6], (DEPTH, DEC_BATCH, M_HEADS, M_DQK), 0.05),
        'state_mlstm_m': nrm(ks[7], (DEPTH, DEC_BATCH, M_HEADS), 1.0),
        'norm_mix': 1.0 + nrm(ks[8], (DEPTH, D_MODEL), 0.02),
        'w_in': nrm(ks[9], (DEPTH, D_MODEL, D_IN), D_MODEL ** -0.5),
        'b_mlstm_i': nrm(ks[10], (DEPTH, M_HEADS), 0.1),
        'b_mlstm_f': 3.0 + nrm(ks[11], (DEPTH, M_HEADS), 0.1),
        'b_fox_f': 3.0 + nrm(ks[12], (DEPTH, F_HEADS), 0.1),
        'norm_mlstm_h': 1.0 + nrm(ks[13], (DEPTH, M_WIDTH), 0.02),
        'w_branch_a': nrm(ks[14], (DEPTH, M_WIDTH, D_MODEL), M_WIDTH ** -0.5),
        'w_branch_b': nrm(ks[15], (DEPTH, F_WIDTH, D_MODEL), F_WIDTH ** -0.5),
        'w_out': nrm(ks[16], (DEPTH, D_MODEL, D_MODEL), D_MODEL ** -0.5),
        'norm_ffn': 1.0 + nrm(ks[17], (DEPTH, D_MODEL), 0.02),
        'w_up': nrm(ks[18], (DEPTH, D_MODEL, D_FF), D_MODEL ** -0.5),
        'w_down': nrm(ks[19], (DEPTH, D_FF, D_MODEL), D_FF ** -0.5),
        'norm_final': 1.0 + nrm(ks[20], (D_MODEL,), 0.02),
    }


def reference(x_prompt, x_sample, cache_fox_k, cache_fox_v, cache_fox_logf, state_mlstm_c, state_mlstm_n,
              state_mlstm_m, norm_mix, w_in, b_mlstm_i, b_mlstm_f, b_fox_f, norm_mlstm_h, w_branch_a,
              w_branch_b, w_out, norm_ffn, w_up, w_down, norm_final):
    f32 = jnp.float32
    hp, hs = x_prompt, x_sample
    pk, pv, plf, pc, pn, pm = [], [], [], [], [], []
    sk, sv, slf, sc, sn, sm = [], [], [], [], [], []
    for l in range(DEPTH):
        w = (norm_mix[l], w_in[l], b_mlstm_i[l], b_mlstm_f[l], b_fox_f[l], norm_mlstm_h[l],
             w_branch_a[l], w_branch_b[l], w_out[l], norm_ffn[l], w_up[l], w_down[l])
        bp = hp.shape[0]
        fresh = (jnp.zeros((bp, M_HEADS, M_DQK, M_DV), f32), jnp.zeros((bp, M_HEADS, M_DQK), f32),
                 jnp.zeros((bp, M_HEADS), f32))
        hp, st_p = trunk_layer(hp, fresh, None, *w)
        hs, st_s = trunk_layer(hs, (state_mlstm_c[l], state_mlstm_n[l], state_mlstm_m[l]),
                               (cache_fox_k[l], cache_fox_v[l], cache_fox_logf[l]), *w)
        for lst, a in zip((pk, pv, plf, pc, pn, pm), st_p):
            lst.append(a)
        for lst, a in zip((sk, sv, slf, sc, sn, sm), st_s):
            lst.append(a)
    y_prompt = rmsnorm(hp, norm_final)
    y_sample = rmsnorm(hs, norm_final)
    return (y_prompt, y_sample,
            jnp.stack(pk), jnp.stack(pv), jnp.stack(plf), jnp.stack(pc), jnp.stack(pn), jnp.stack(pm),
            jnp.stack(sk), jnp.stack(sv), jnp.stack(slf), jnp.stack(sc), jnp.stack(sn), jnp.stack(sm))
```

```python
import functools

import jax
import jax.numpy as jnp
from jax import lax
from jax.experimental import pallas as pl
from jax.experimental.pallas import tpu as pltpu

M_HEADS = 4
M_DQK = 256
M_DV = 256
M_WIDTH = M_HEADS * M_DV
F_HEADS = 8
F_HEAD_DIM = 128
F_WIDTH = F_HEADS * F_HEAD_DIM
EPS = 1e-6
N_GATE_ROWS = 2 * M_HEADS + F_HEADS
LANES = 128
MLSTM_BLOCK = 256
ATTN_BLOCK = 512
VMEM_LIMIT_BYTES = 56 * 1024 * 1024

f32 = jnp.float32
bf16 = jnp.bfloat16

_NT = (((1,), (1,)), ((), ()))
_TN = (((0,), (0,)), ((), ()))


def _tile(n, pref):
    t = min(n, pref)
    while n % t:
        t //= 2
    return t


def _params(*sem):
    return pltpu.CompilerParams(dimension_semantics=sem, vmem_limit_bytes=VMEM_LIMIT_BYTES)


def _log_sigmoid(z):
    return jnp.minimum(z, 0.0) - jnp.log1p(jnp.exp(-jnp.abs(z)))


def _cumsum_lanes(x):
    rows, n = x.shape
    pad = -rows % 16
    if rows == 1:
        x16 = jnp.broadcast_to(x, (16, n))
    elif pad == 0:
        x16 = x
    else:
        x16 = jnp.concatenate([x, jnp.zeros((pad, n), f32)], axis=0)
    r = lax.broadcasted_iota(jnp.int32, (n, n), 0)
    c = lax.broadcasted_iota(jnp.int32, (n, n), 1)
    u = jnp.where(r <= c, 1.0, 0.0).astype(bf16)
    hi = x16.astype(bf16)
    rem = x16 - hi.astype(f32)
    mid = rem.astype(bf16)
    lo = (rem - mid.astype(f32)).astype(bf16)
    out = (jnp.dot(hi, u, preferred_element_type=f32) + jnp.dot(mid, u, preferred_element_type=f32)
           + jnp.dot(lo, u, preferred_element_type=f32))
    return out[:rows]


def _norm_kernel(x_ref, g_ref, wg_ref, xn_ref, gt_ref):
    x = x_ref[...]
    y = x * lax.rsqrt(jnp.mean(x * x, axis=-1, keepdims=True) + EPS) * g_ref[...]
    xn = y.astype(bf16)
    xn_ref[...] = xn
    gt_ref[...] = lax.dot_general(wg_ref[...], xn, _NT, preferred_element_type=f32)


def _norm_gates(x, g, wg_t):
    t, d = x.shape
    tm = _tile(t, 512)
    return pl.pallas_call(
        _norm_kernel,
        grid=(t // tm,),
        in_specs=[pl.BlockSpec((tm, d), lambda i: (i, 0)),
                  pl.BlockSpec((1, d), lambda i: (0, 0)),
                  pl.BlockSpec((N_GATE_ROWS, d), lambda i: (0, 0))],
        out_specs=[pl.BlockSpec((tm, d), lambda i: (i, 0)),
                   pl.BlockSpec((N_GATE_ROWS, tm), lambda i: (0, i))],
        out_shape=[jax.ShapeDtypeStruct((t, d), bf16), jax.ShapeDtypeStruct((N_GATE_ROWS, t), f32)],
        compiler_params=_params("parallel"),
        name="norm_gates",
    )(x, g, wg_t)


def _mm_kernel(a_ref, w_ref, *o_refs):
    r = jnp.dot(a_ref[...], w_ref[...], preferred_element_type=f32)
    for o in o_refs:
        o[...] = r.astype(o.dtype)


def _matmul(a, w, out_dtypes, name):
    t, k = a.shape
    n = w.shape[1]
    tm = _tile(t, 1024)
    tn = _tile(n, 512)
    return pl.pallas_call(
        _mm_kernel,
        grid=(t // tm, n // tn),
        in_specs=[pl.BlockSpec((tm, k), lambda i, j: (i, 0)),
                  pl.BlockSpec((k, tn), lambda i, j: (0, j))],
        out_specs=[pl.BlockSpec((tm, tn), lambda i, j: (i, j)) for _ in out_dtypes],
        out_shape=[jax.ShapeDtypeStruct((t, n), dt) for dt in out_dtypes],
        compiler_params=_params("parallel", "arbitrary"),
        name=name,
    )(a, w)


def _scan_kernel(x_ref, bias_ref, logf_ref, cum_ref, carry_ref, *, apply_log_sigmoid):
    @pl.when(pl.program_id(1) == 0)
    def _():
        carry_ref[...] = jnp.zeros_like(carry_ref)

    x = x_ref[...]
    if apply_log_sigmoid:
        x = _log_sigmoid(x + bias_ref[...])
    logf_ref[...] = x
    cum = _cumsum_lanes(x) + carry_ref[:, :1]
    cum_ref[...] = cum
    carry_ref[...] = jnp.broadcast_to(cum[:, -1:], carry_ref.shape)


def _scan_rows(x, bias, *, row_block, rows, n_streams, apply_log_sigmoid):
    total = x.shape[1]
    s = total // n_streams
    tb = _tile(s, 512)
    nb = s // tb
    spec = pl.BlockSpec((rows, tb), lambda b, j: (0, b * nb + j))
    return pl.pallas_call(
        functools.partial(_scan_kernel, apply_log_sigmoid=apply_log_sigmoid),
        grid=(n_streams, nb),
        in_specs=[pl.BlockSpec((rows, tb), lambda b, j: (row_block, b * nb + j)),
                  pl.BlockSpec((rows, 1), lambda b, j: (0, 0))],
        out_specs=[spec, spec],
        out_shape=[jax.ShapeDtypeStruct((rows, total), f32)] * 2,
        scratch_shapes=[pltpu.VMEM((rows, LANES), f32)],
        compiler_params=_params("arbitrary", "arbitrary"),
        name="logf_scan",
    )(x, bias)


def _mlstm_kernel(bias_ref, q_ref, k_ref, v_ref, o_ref, ig_ref, fg_ref, nh_ref, c0_ref, n0_ref, m0_ref,
                  h_ref, c_out_ref, n_out_ref, m_out_ref, caug_ref, m_ref):
    head = pl.program_id(1)
    blk = pl.program_id(2)
    n_blk = pl.num_programs(2)
    L = q_ref.shape[0]

    @pl.when(blk == 0)
    def _():
        lane = lax.broadcasted_iota(jnp.int32, (M_DQK, LANES), 1)
        caug_ref[:, :M_DV] = c0_ref[...]
        caug_ref[:, M_DV:] = jnp.where(lane == 0, n0_ref[...], 0.0)
        m_ref[...] = m0_ref[...]

    m0 = m_ref[:, :1]
    ig = ig_ref[...] + bias_ref[head]
    lf = _log_sigmoid(fg_ref[...] + bias_ref[M_HEADS + head])
    b_row = _cumsum_lanes(lf)
    a_row = ig - b_row

    t_idx = lax.broadcasted_iota(jnp.int32, (L, L), 0)
    s_idx = lax.broadcasted_iota(jnp.int32, (L, L), 1)
    causal = s_idx <= t_idx
    a_mat = jnp.where(causal, a_row, -jnp.inf)
    g_col = jnp.maximum(m0, jnp.max(a_mat, axis=1, keepdims=True))
    b_col = jnp.sum(jnp.where(causal, lf, 0.0), axis=1, keepdims=True)
    w_intra = jnp.exp(a_mat - g_col)
    w_inter = jnp.exp(m0 - g_col)

    q = q_ref[...]
    k = k_ref[...]
    ones_col = jnp.where(lax.broadcasted_iota(jnp.int32, (L, LANES), 1) == 0, 1.0, 0.0).astype(bf16)
    v_aug = jnp.concatenate([v_ref[...], ones_col], axis=1)

    s = lax.dot_general(q, k, _NT, preferred_element_type=f32)
    sw = (s * w_intra).astype(bf16)
    num_aug = (jnp.dot(sw, v_aug, preferred_element_type=f32)
               + w_inter * jnp.dot(q, caug_ref[...].astype(bf16), preferred_element_type=f32))
    num = num_aug[:, :M_DV]
    den = num_aug[:, M_DV:M_DV + 1]
    den = jnp.maximum(jnp.abs(den), jnp.exp(-(b_col + g_col)))
    h = num / den
    h = h * lax.rsqrt(jnp.mean(h * h, axis=-1, keepdims=True) + EPS)
    h = h * nh_ref[...] * jax.nn.sigmoid(o_ref[...].astype(f32))
    h_ref[...] = h.astype(h_ref.dtype)

    g_end = jnp.maximum(m0, jnp.max(a_row, axis=1, keepdims=True))
    b_end = jnp.sum(lf, axis=1, keepdims=True)
    diag = s_idx == t_idx
    ig_col = jnp.sum(jnp.where(diag, ig, 0.0), axis=1, keepdims=True)
    w_tok = jnp.exp(ig_col - b_col - g_end)
    w_state = jnp.exp(m0 - g_end)
    kw = (k.astype(f32) * w_tok).astype(bf16)
    caug_ref[...] = w_state * caug_ref[...] + lax.dot_general(kw, v_aug, _TN, preferred_element_type=f32)
    m_ref[...] = jnp.broadcast_to(b_end + g_end, m_ref.shape)

    @pl.when(blk == n_blk - 1)
    def _():
        c_out_ref[...] = caug_ref[:, :M_DV]
        n_out_ref[...] = caug_ref[:, M_DV:]
        m_out_ref[...] = m_ref[...]


def _mlstm(z, gates_t, gate_bias, norm_h, c0, n0, m0, *, n_streams):
    t = z.shape[0]
    s = t // n_streams
    L = _tile(s, MLSTM_BLOCK)
    nb = s // L
    gates4 = gates_t.reshape(N_GATE_ROWS, n_streams * nb, 1, L)
    n0c = n0.reshape(n_streams, M_HEADS, M_DQK, 1)
    m0b = jnp.broadcast_to(m0.reshape(n_streams, M_HEADS, 1, 1), (n_streams, M_HEADS, 1, LANES))

    def zcol(group):
        return pl.BlockSpec((L, M_DV), lambda b, h, c: (b * nb + c, group * M_HEADS + h))

    def gate(group):
        return pl.BlockSpec((None, None, 1, L), lambda b, h, c: (group * M_HEADS + h, b * nb + c, 0, 0))

    state_c = pl.BlockSpec((None, None, M_DQK, M_DV), lambda b, h, c: (b, h, 0, 0))
    state_n1 = pl.BlockSpec((None, None, M_DQK, 1), lambda b, h, c: (b, h, 0, 0))
    state_n = pl.BlockSpec((None, None, M_DQK, LANES), lambda b, h, c: (b, h, 0, 0))
    state_m = pl.BlockSpec((None, None, 1, LANES), lambda b, h, c: (b, h, 0, 0))
    return pl.pallas_call(
        _mlstm_kernel,
        grid=(n_streams, M_HEADS, nb),
        in_specs=[pl.BlockSpec(memory_space=pltpu.SMEM),
                  zcol(0), zcol(1), zcol(2), zcol(3), gate(0), gate(1),
                  pl.BlockSpec((1, M_DV), lambda b, h, c: (0, h)),
                  state_c, state_n1, state_m],
        out_specs=[pl.BlockSpec((L, M_DV), lambda b, h, c: (b * nb + c, h)), state_c, state_n, state_m],
        out_shape=[jax.ShapeDtypeStruct((t, M_WIDTH), bf16),
                   jax.ShapeDtypeStruct((n_streams, M_HEADS, M_DQK, M_DV), f32),
                   jax.ShapeDtypeStruct((n_streams, M_HEADS, M_DQK, LANES), f32),
                   jax.ShapeDtypeStruct((n_streams, M_HEADS, 1, LANES), f32)],
        scratch_shapes=[pltpu.VMEM((M_DQK, M_DV + LANES), f32), pltpu.VMEM((1, LANES), f32)],
        compiler_params=_params("parallel", "parallel", "arbitrary"),
        name="mlstm",
    )(gate_bias, z, z, z, z, gates4, gates4, norm_h, c0, n0c, m0b)


def _fox_prompt_kernel(q_ref, k_ref, v_ref, cum_ref, o_ref):
    i = pl.program_id(2)
    tq = q_ref.shape[0]
    q = q_ref[...]

    def scores(j):
        start = pl.multiple_of(j * tq, tq)
        k = k_ref[pl.ds(start, tq), :]
        return lax.dot_general(q, k, _NT, preferred_element_type=f32) - cum_ref[j], start

    def update(carry, s, start):
        m, l, acc = carry
        m_new = jnp.maximum(m, jnp.max(s, axis=1, keepdims=True))
        alpha = jnp.exp(m - m_new)
        p = jnp.exp(s - m_new)
        l = alpha * l + jnp.sum(p, axis=1, keepdims=True)
        acc = alpha * acc + jnp.dot(p.astype(bf16), v_ref[pl.ds(start, tq), :], preferred_element_type=f32)
        return m_new, l, acc

    def body(j, carry):
        s, start = scores(j)
        return update(carry, s, start)

    init = (jnp.full((tq, 1), -jnp.inf, f32), jnp.zeros((tq, 1), f32), jnp.zeros((tq, F_HEAD_DIM), f32))
    carry = lax.fori_loop(0, i, body, init)
    s, start = scores(i)
    row = lax.broadcasted_iota(jnp.int32, (tq, tq), 0)
    col = lax.broadcasted_iota(jnp.int32, (tq, tq), 1)
    s = jnp.where(col <= row, s, -jnp.inf)
    _, l, acc = update(carry, s, start)
    o_ref[...] = (acc / l).astype(o_ref.dtype)


def _fox_prompt(z, kv, cum_t, *, q_col0, n_streams):
    t = z.shape[0]
    s = t // n_streams
    tq = _tile(s, ATTN_BLOCK)
    nq = s // tq
    cum4 = cum_t.reshape(F_HEADS, n_streams * nq, 1, tq)
    q0 = q_col0 // F_HEAD_DIM
    return pl.pallas_call(
        _fox_prompt_kernel,
        grid=(n_streams, F_HEADS, nq),
        in_specs=[pl.BlockSpec((tq, F_HEAD_DIM), lambda b, h, i: (b * nq + i, q0 + h)),
                  pl.BlockSpec((s, F_HEAD_DIM), lambda b, h, i: (b, h)),
                  pl.BlockSpec((s, F_HEAD_DIM), lambda b, h, i: (b, F_HEADS + h)),
                  pl.BlockSpec((None, nq, 1, tq), lambda b, h, i: (h, b, 0, 0))],
        out_specs=pl.BlockSpec((tq, F_HEAD_DIM), lambda b, h, i: (b * nq + i, h)),
        out_shape=jax.ShapeDtypeStruct((t, F_WIDTH), bf16),
        compiler_params=_params("parallel", "parallel", "arbitrary"),
        name="fox_prompt",
    )(z, kv, kv, cum4)


def _fox_decode_kernel(bias_ref, q_ref, kn_ref, vn_ref, kc_ref, vc_ref, cumc_ref, fg_ref, o_ref, logf_ref):
    head = pl.program_id(1)
    L = q_ref.shape[0]
    q = q_ref[...]
    logf = _log_sigmoid(fg_ref[...] + bias_ref[head])
    logf_ref[...] = logf
    cum_new = _cumsum_lanes(logf)
    cum_c = cumc_ref[...]
    cum_c = cum_c - cum_c[:, -1:]

    s_c = lax.dot_general(q, kc_ref[...].astype(bf16), _NT, preferred_element_type=f32) - cum_c
    s_n = lax.dot_general(q, kn_ref[...], _NT, preferred_element_type=f32) - cum_new
    row = lax.broadcasted_iota(jnp.int32, (L, L), 0)
    col = lax.broadcasted_iota(jnp.int32, (L, L), 1)
    s_n = jnp.where(col <= row, s_n, -jnp.inf)
    m = jnp.maximum(jnp.max(s_c, axis=1, keepdims=True), jnp.max(s_n, axis=1, keepdims=True))
    p_c = jnp.exp(s_c - m)
    p_n = jnp.exp(s_n - m)
    l = jnp.sum(p_c, axis=1, keepdims=True) + jnp.sum(p_n, axis=1, keepdims=True)
    acc = (jnp.dot(p_c.astype(bf16), vc_ref[...].astype(bf16), preferred_element_type=f32)
           + jnp.dot(p_n.astype(bf16), vn_ref[...], preferred_element_type=f32))
    o_ref[...] = (acc / l).astype(o_ref.dtype)


def _fox_decode(z, kv, gates_t, bias, cache_k, cache_v, cum_cache, *, q_col0, n_streams):
    t = z.shape[0]
    L = t // n_streams
    p = cache_k.shape[1]
    gates4 = gates_t.reshape(N_GATE_ROWS, n_streams, 1, L)
    q0 = q_col0 // F_HEAD_DIM
    return pl.pallas_call(
        _fox_decode_kernel,
        grid=(n_streams, F_HEADS),
        in_specs=[pl.BlockSpec(memory_space=pltpu.SMEM),
                  pl.BlockSpec((L, F_HEAD_DIM), lambda b, h: (b, q0 + h)),
                  pl.BlockSpec((L, F_HEAD_DIM), lambda b, h: (b, h)),
                  pl.BlockSpec((L, F_HEAD_DIM), lambda b, h: (b, F_HEADS + h)),
                  pl.BlockSpec((None, p, F_HEAD_DIM), lambda b, h: (b, 0, h)),
                  pl.BlockSpec((None, p, F_HEAD_DIM), lambda b, h: (b, 0, h)),
                  pl.BlockSpec((None, 1, p), lambda b, h: (b * F_HEADS + h, 0, 0)),
                  pl.BlockSpec((None, None, 1, L), lambda b, h: (2 * M_HEADS + h, b, 0, 0))],
        out_specs=[pl.BlockSpec((L, F_HEAD_DIM), lambda b, h: (b, h)),
                   pl.BlockSpec((None, None, 1, L), lambda b, h: (b, h, 0, 0))],
        out_shape=[jax.ShapeDtypeStruct((t, F_WIDTH), bf16),
                   jax.ShapeDtypeStruct((n_streams, F_HEADS, 1, L), f32)],
        compiler_params=_params("parallel", "parallel"),
        name="fox_decode",
    )(bias, z, kv, kv, cache_k, cache_v, cum_cache, gates4)


def _merge_kernel(x_ref, ha_ref, hb_ref, ga_ref, gb_ref, wa_ref, wb_ref, wo_ref, g_ref, x1_ref, hn_ref):
    pa = jnp.dot(ha_ref[...], wa_ref[...], preferred_element_type=f32)
    pb = jnp.dot(hb_ref[...], wb_ref[...], preferred_element_type=f32)
    merged = (jax.nn.sigmoid(ga_ref[...].astype(f32)) * pa + jax.nn.sigmoid(gb_ref[...].astype(f32)) * pb)
    x1 = x_ref[...] + jnp.dot(merged.astype(bf16), wo_ref[...], preferred_element_type=f32)
    x1_ref[...] = x1
    hn = x1 * lax.rsqrt(jnp.mean(x1 * x1, axis=-1, keepdims=True) + EPS) * g_ref[...]
    hn_ref[...] = hn.astype(hn_ref.dtype)


def _merge(x, h_a, h_b, z, w_a, w_b, w_o, g, *, gate_col0):
    t, d = x.shape
    tm = _tile(t, 256)
    ga_blk = gate_col0 // d
    row = lambda i: (i, 0)
    fixed = lambda i: (0, 0)
    resident = functools.partial(pl.BlockSpec, index_map=fixed, pipeline_mode=pl.Buffered(1))
    return pl.pallas_call(
        _merge_kernel,
        grid=(t // tm,),
        in_specs=[pl.BlockSpec((tm, d), row),
                  pl.BlockSpec((tm, M_WIDTH), row),
                  pl.BlockSpec((tm, F_WIDTH), row),
                  pl.BlockSpec((tm, d), lambda i: (i, ga_blk)),
                  pl.BlockSpec((tm, d), lambda i: (i, ga_blk + 1)),
                  resident(w_a.shape), resident(w_b.shape), resident(w_o.shape),
                  pl.BlockSpec((1, d), fixed)],
        out_specs=[pl.BlockSpec((tm, d), row), pl.BlockSpec((tm, d), row)],
        out_shape=[jax.ShapeDtypeStruct((t, d), f32), jax.ShapeDtypeStruct((t, d), bf16)],
        compiler_params=_params("parallel"),
        name="merge",
    )(x, h_a, h_b, z, z, w_a, w_b, w_o, g)


def _ffn_kernel(hn_ref, x1_ref, wu_ref, wd_ref, g_ref, y_ref):
    f = pl.program_id(1)
    u = jnp.maximum(jnp.dot(hn_ref[...], wu_ref[...], preferred_element_type=f32), 0.0)
    part = jnp.dot((u * u).astype(bf16), wd_ref[...], preferred_element_type=f32)

    @pl.when(f == 0)
    def _():
        y_ref[...] = x1_ref[...] + part

    @pl.when(f > 0)
    def _():
        y_ref[...] += part

    @pl.when(f == pl.num_programs(1) - 1)
    def _():
        x2 = y_ref[...]
        y_ref[...] = x2 * lax.rsqrt(jnp.mean(x2 * x2, axis=-1, keepdims=True) + EPS) * g_ref[...]


def _ffn(hn, x1, w_up, w_down, g):
    t, d = x1.shape
    dff = w_up.shape[1]
    tm = _tile(t, 512)
    tf = _tile(dff, 512)
    return pl.pallas_call(
        _ffn_kernel,
        grid=(t // tm, dff // tf),
        in_specs=[pl.BlockSpec((tm, d), lambda i, f: (i, 0)),
                  pl.BlockSpec((tm, d), lambda i, f: (i, 0)),
                  pl.BlockSpec((d, tf), lambda i, f: (0, f)),
                  pl.BlockSpec((tf, d), lambda i, f: (f, 0)),
                  pl.BlockSpec((1, d), lambda i, f: (0, 0))],
        out_specs=pl.BlockSpec((tm, d), lambda i, f: (i, 0)),
        out_shape=jax.ShapeDtypeStruct((t, d), f32),
        compiler_params=_params("parallel", "arbitrary"),
        name="ffn",
    )(hn, x1, w_up, w_down, g)


def _layer(x3, w, mstate, fox_past):
    nstr, frames, d = x3.shape
    t = nstr * frames
    x = x3.reshape(t, d)
    xn, gates_t = _norm_gates(x, w["norm_mix"], w["w_gates_t"])
    kv32, kv = _matmul(xn, w["w_kv"], (f32, bf16), "proj_kv")
    (z,) = _matmul(xn, w["w_rest"], (bf16,), "proj_rest")

    c0, n0, m0 = mstate
    h_a, c_new, n_slab, m_slab = _mlstm(z, gates_t, w["mlstm_bias"], w["norm_mlstm_h"], c0, n0, m0,
                                        n_streams=nstr)
    n_new = n_slab[..., 0]
    m_new = m_slab[:, :, 0, 0]

    if fox_past is None:
        logf_t, cum_t = _scan_rows(gates_t, w["b_fox_f_col"], row_block=1, rows=F_HEADS, n_streams=nstr,
                                   apply_log_sigmoid=True)
        h_b = _fox_prompt(z, kv, cum_t, q_col0=w["q_col0"], n_streams=nstr)
        logf = logf_t.T.reshape(nstr, frames, F_HEADS)
    else:
        ck, cv, clf = fox_past
        p = ck.shape[1]
        clf_t = jnp.transpose(clf, (0, 2, 1)).reshape(nstr * F_HEADS, p)
        _, cum_c = _scan_rows(clf_t, jnp.zeros((nstr * F_HEADS, 1), f32), row_block=0, rows=nstr * F_HEADS,
                              n_streams=1, apply_log_sigmoid=False)
        h_b, logf4 = _fox_decode(z, kv, gates_t, w["b_fox_f"], ck.reshape(nstr, p, F_WIDTH),
                                 cv.reshape(nstr, p, F_WIDTH), cum_c.reshape(nstr * F_HEADS, 1, p),
                                 q_col0=w["q_col0"], n_streams=nstr)
        logf = jnp.transpose(logf4[:, :, 0, :], (0, 2, 1))

    x1, hn = _merge(x, h_a, h_b, z, w["w_branch_a"], w["w_branch_b"], w["w_out"], w["norm_ffn"],
                    gate_col0=w["gate_col0"])
    y = _ffn(hn, x1, w["w_up"], w["w_down"], w["norm_final"])

    k_rows = kv32[:, :F_WIDTH].reshape(nstr, frames, F_HEADS, F_HEAD_DIM)
    v_rows = kv32[:, F_WIDTH:].reshape(nstr, frames, F_HEADS, F_HEAD_DIM)
    return y.reshape(nstr, frames, d), (k_rows, v_rows, logf, c_new, n_new, m_new)


def _prepare_weights(norm_mix, w_in, b_mlstm_i, b_mlstm_f, b_fox_f, norm_mlstm_h, w_branch_a, w_branch_b,
                     w_out, norm_ffn, w_up, w_down, norm_final):
    d = w_in.shape[0]
    sizes = (M_HEADS * M_DQK, M_HEADS * M_DQK, M_WIDTH, M_WIDTH, M_HEADS, M_HEADS,
             F_WIDTH, F_WIDTH, F_WIDTH, F_HEADS, d, d)
    offs = [0]
    for s in sizes:
        offs.append(offs[-1] + s)
    mq, mk, mv, mo, mi, mf, fq, fk, fv, ff, ga, gb = (w_in[:, offs[i]:offs[i + 1]] for i in range(12))
    w_rest = jnp.concatenate([mq, mk * (M_DQK ** -0.5), mv, mo, ga, gb, fq * (F_HEAD_DIM ** -0.5)], axis=1)
    return {
        "norm_mix": norm_mix.reshape(1, d),
        "w_gates_t": jnp.concatenate([mi, mf, ff], axis=1).T.astype(bf16),
        "w_kv": jnp.concatenate([fk, fv], axis=1).astype(bf16),
        "w_rest": w_rest.astype(bf16),
        "gate_col0": 4 * M_WIDTH,
        "q_col0": 4 * M_WIDTH + 2 * d,
        "mlstm_bias": jnp.concatenate([b_mlstm_i, b_mlstm_f]).astype(f32),
        "b_fox_f": b_fox_f.astype(f32),
        "b_fox_f_col": b_fox_f.astype(f32).reshape(F_HEADS, 1),
        "norm_mlstm_h": norm_mlstm_h.reshape(1, M_WIDTH),
        "w_branch_a": w_branch_a.astype(bf16),
        "w_branch_b": w_branch_b.astype(bf16),
        "w_out": w_out.astype(bf16),
        "norm_ffn": norm_ffn.reshape(1, d),
        "w_up": w_up.astype(bf16),
        "w_down": w_down.astype(bf16),
        "norm_final": norm_final.reshape(1, d),
    }


def kernel(x_prompt, x_sample, cache_fox_k, cache_fox_v, cache_fox_logf, state_mlstm_c, state_mlstm_n, state_mlstm_m, norm_mix, w_in, b_mlstm_i, b_mlstm_f, b_fox_f, norm_mlstm_h, w_branch_a, w_branch_b, w_out, norm_ffn, w_up, w_down, norm_final):
    depth = w_in.shape[0]
    assert depth == 1, "the final norm is fused into the layer's FFN kernel"
    w = _prepare_weights(norm_mix[0], w_in[0], b_mlstm_i[0], b_mlstm_f[0], b_fox_f[0], norm_mlstm_h[0],
                         w_branch_a[0], w_branch_b[0], w_out[0], norm_ffn[0], w_up[0], w_down[0], norm_final)
    bp = x_prompt.shape[0]
    fresh = (jnp.zeros((bp, M_HEADS, M_DQK, M_DV), f32), jnp.zeros((bp, M_HEADS, M_DQK), f32),
             jnp.zeros((bp, M_HEADS), f32))
    y_p, st_p = _layer(x_prompt, w, fresh, None)
    y_s, st_s = _layer(x_sample, w, (state_mlstm_c[0], state_mlstm_n[0], state_mlstm_m[0]),
                       (cache_fox_k[0], cache_fox_v[0], cache_fox_logf[0]))
    return (y_p, y_s) + tuple(a[None] for a in st_p) + tuple(a[None] for a in st_s)
```

```python
import functools

import jax
import jax.numpy as jnp
from jax import lax
from jax.experimental import pallas as pl
from jax.experimental.pallas import tpu as pltpu

M_HEADS = 4
M_DQK = 256
M_DV = 256
M_WIDTH = M_HEADS * M_DV
F_HEADS = 8
F_HEAD_DIM = 128
F_WIDTH = F_HEADS * F_HEAD_DIM
EPS = 1e-6
N_GATE_ROWS = 2 * M_HEADS + F_HEADS
LANES = 128
assert F_HEAD_DIM == LANES
MLSTM_BLOCK = 256
ATTN_BLOCK = 512
LOG2E = 1.4426950408889634
VMEM_LIMIT_BYTES = 56 * 1024 * 1024

f32 = jnp.float32
bf16 = jnp.bfloat16

_NT = (((1,), (1,)), ((), ()))
_TN = (((0,), (0,)), ((), ()))


def _tile(n, pref):
    t = min(n, pref)
    while n % t:
        t //= 2
    return t


def _params(*sem):
    return pltpu.CompilerParams(dimension_semantics=sem, vmem_limit_bytes=VMEM_LIMIT_BYTES)


def _log_sigmoid(z):
    return jnp.minimum(z, 0.0) - jnp.log1p(jnp.exp(-jnp.abs(z)))


def _cumsum_lanes(x):
    rows, n = x.shape
    pad = -rows % 16
    if rows == 1:
        x16 = jnp.broadcast_to(x, (16, n))
    elif pad == 0:
        x16 = x
    else:
        x16 = jnp.concatenate([x, jnp.zeros((pad, n), f32)], axis=0)
    r = lax.broadcasted_iota(jnp.int32, (n, n), 0)
    c = lax.broadcasted_iota(jnp.int32, (n, n), 1)
    u = jnp.where(r <= c, 1.0, 0.0).astype(bf16)
    hi = x16.astype(bf16)
    rem = x16 - hi.astype(f32)
    mid = rem.astype(bf16)
    lo = (rem - mid.astype(f32)).astype(bf16)
    out = (jnp.dot(hi, u, preferred_element_type=f32) + jnp.dot(mid, u, preferred_element_type=f32)
           + jnp.dot(lo, u, preferred_element_type=f32))
    return out[:rows]


def _norm_kernel(x_ref, g_ref, wg_ref, xn_ref, gt_ref):
    x = x_ref[...]
    y = x * lax.rsqrt(jnp.mean(x * x, axis=-1, keepdims=True) + EPS) * g_ref[...]
    xn = y.astype(bf16)
    xn_ref[...] = xn
    gt_ref[...] = lax.dot_general(wg_ref[...], xn, _NT, preferred_element_type=f32)


def _norm_gates(x, g, wg_t):
    t, d = x.shape
    tm = _tile(t, 512)
    return pl.pallas_call(
        _norm_kernel,
        grid=(t // tm,),
        in_specs=[pl.BlockSpec((tm, d), lambda i: (i, 0)),
                  pl.BlockSpec((1, d), lambda i: (0, 0)),
                  pl.BlockSpec((N_GATE_ROWS, d), lambda i: (0, 0))],
        out_specs=[pl.BlockSpec((tm, d), lambda i: (i, 0)),
                   pl.BlockSpec((N_GATE_ROWS, tm), lambda i: (0, i))],
        out_shape=[jax.ShapeDtypeStruct((t, d), bf16), jax.ShapeDtypeStruct((N_GATE_ROWS, t), f32)],
        compiler_params=_params("parallel"),
        name="norm_gates",
    )(x, g, wg_t)


def _mm_kernel(a_ref, w_ref, *o_refs):
    r = jnp.dot(a_ref[...], w_ref[...], preferred_element_type=f32)
    for o in o_refs:
        o[...] = r.astype(o.dtype)


def _matmul(a, w, out_dtypes, name):
    t, k = a.shape
    n = w.shape[1]
    tm = _tile(t, 1024)
    tn = _tile(n, 1024)
    return pl.pallas_call(
        _mm_kernel,
        grid=(t // tm, n // tn),
        in_specs=[pl.BlockSpec((tm, k), lambda i, j: (i, 0)),
                  pl.BlockSpec((k, tn), lambda i, j: (0, j))],
        out_specs=[pl.BlockSpec((tm, tn), lambda i, j: (i, j)) for _ in out_dtypes],
        out_shape=[jax.ShapeDtypeStruct((t, n), dt) for dt in out_dtypes],
        compiler_params=_params("parallel", "arbitrary"),
        name=name,
    )(a, w)


def _scan_kernel(x_ref, bias_ref, logf_ref, cum_ref, carry_ref, *, apply_log_sigmoid):
    @pl.when(pl.program_id(1) == 0)
    def _():
        carry_ref[...] = jnp.zeros_like(carry_ref)

    x = x_ref[...]
    if apply_log_sigmoid:
        x = _log_sigmoid(x + bias_ref[...])
    logf_ref[...] = x
    cum = _cumsum_lanes(x) + carry_ref[:, :1]
    cum_ref[...] = cum * LOG2E
    carry_ref[...] = jnp.broadcast_to(cum[:, -1:], carry_ref.shape)


def _scan_rows(x, bias, *, row_block, rows, n_streams, apply_log_sigmoid):
    total = x.shape[1]
    s = total // n_streams
    tb = _tile(s, 512)
    nb = s // tb
    spec = pl.BlockSpec((rows, tb), lambda b, j: (0, b * nb + j))
    return pl.pallas_call(
        functools.partial(_scan_kernel, apply_log_sigmoid=apply_log_sigmoid),
        grid=(n_streams, nb),
        in_specs=[pl.BlockSpec((rows, tb), lambda b, j: (row_block, b * nb + j)),
                  pl.BlockSpec((rows, 1), lambda b, j: (0, 0))],
        out_specs=[spec, spec],
        out_shape=[jax.ShapeDtypeStruct((rows, total), f32)] * 2,
        scratch_shapes=[pltpu.VMEM((rows, LANES), f32)],
        compiler_params=_params("arbitrary", "arbitrary"),
        name="logf_scan",
    )(x, bias)


def _mlstm_kernel(bias_ref, q_ref, k_ref, v_ref, o_ref, ig_ref, fg_ref, nh_ref, c0_ref, n0_ref, m0_ref,
                  h_ref, c_out_ref, n_out_ref, m_out_ref, caug_ref, m_ref):
    head = pl.program_id(1)
    blk = pl.program_id(2)
    n_blk = pl.num_programs(2)
    L = q_ref.shape[0]

    @pl.when(blk == 0)
    def _():
        lane = lax.broadcasted_iota(jnp.int32, (M_DQK, LANES), 1)
        caug_ref[:, :M_DV] = c0_ref[...]
        caug_ref[:, M_DV:] = jnp.where(lane == 0, n0_ref[...], 0.0)
        m_ref[...] = m0_ref[...]

    m0 = m_ref[:, :1]
    ig = ig_ref[...] + bias_ref[head]
    lf = _log_sigmoid(fg_ref[...] + bias_ref[M_HEADS + head])
    b_row = _cumsum_lanes(lf)
    a_row = ig - b_row

    t_idx = lax.broadcasted_iota(jnp.int32, (L, L), 0)
    s_idx = lax.broadcasted_iota(jnp.int32, (L, L), 1)
    causal = s_idx <= t_idx
    a_mat = jnp.where(causal, a_row, -jnp.inf)
    g_col = jnp.maximum(m0, jnp.max(a_mat, axis=1, keepdims=True))
    b_col = jnp.sum(jnp.where(causal, lf, 0.0), axis=1, keepdims=True)
    w_intra = jnp.exp(a_mat - g_col)
    w_inter = jnp.exp(m0 - g_col)

    q = q_ref[...]
    k = k_ref[...]
    ones_col = jnp.where(lax.broadcasted_iota(jnp.int32, (L, LANES), 1) == 0, 1.0, 0.0).astype(bf16)
    v_aug = jnp.concatenate([v_ref[...], ones_col], axis=1)

    s = lax.dot_general(q, k, _NT, preferred_element_type=f32)
    sw = (s * w_intra).astype(bf16)
    num_aug = (jnp.dot(sw, v_aug, preferred_element_type=f32)
               + w_inter * jnp.dot(q, caug_ref[...].astype(bf16), preferred_element_type=f32))
    num = num_aug[:, :M_DV]
    den = num_aug[:, M_DV:M_DV + 1]
    den = jnp.maximum(jnp.abs(den), jnp.exp(-(b_col + g_col)))
    h = num / den
    h = h * lax.rsqrt(jnp.mean(h * h, axis=-1, keepdims=True) + EPS)
    h = h * nh_ref[...] * jax.nn.sigmoid(o_ref[...].astype(f32))
    h_ref[...] = h.astype(h_ref.dtype)

    g_end = jnp.maximum(m0, jnp.max(a_row, axis=1, keepdims=True))
    b_end = jnp.sum(lf, axis=1, keepdims=True)
    diag = s_idx == t_idx
    ig_col = jnp.sum(jnp.where(diag, ig, 0.0), axis=1, keepdims=True)
    w_tok = jnp.exp(ig_col - b_col - g_end)
    w_state = jnp.exp(m0 - g_end)
    kw = (k.astype(f32) * w_tok).astype(bf16)
    caug_ref[...] = w_state * caug_ref[...] + lax.dot_general(kw, v_aug, _TN, preferred_element_type=f32)
    m_ref[...] = jnp.broadcast_to(b_end + g_end, m_ref.shape)

    @pl.when(blk == n_blk - 1)
    def _():
        c_out_ref[...] = caug_ref[:, :M_DV]
        n_out_ref[...] = caug_ref[:, M_DV:]
        m_out_ref[...] = m_ref[...]


def _mlstm(z, gates_t, gate_bias, norm_h, c0, n0, m0, *, n_streams):
    t = z.shape[0]
    s = t // n_streams
    L = _tile(s, MLSTM_BLOCK)
    nb = s // L
    gates4 = gates_t.reshape(N_GATE_ROWS, n_streams * nb, 1, L)
    n0c = n0.reshape(n_streams, M_HEADS, M_DQK, 1)
    m0b = jnp.broadcast_to(m0.reshape(n_streams, M_HEADS, 1, 1), (n_streams, M_HEADS, 1, LANES))

    def zcol(group):
        return pl.BlockSpec((L, M_DV), lambda b, h, c: (b * nb + c, group * M_HEADS + h))

    def gate(group):
        return pl.BlockSpec((None, None, 1, L), lambda b, h, c: (group * M_HEADS + h, b * nb + c, 0, 0))

    state_c = pl.BlockSpec((None, None, M_DQK, M_DV), lambda b, h, c: (b, h, 0, 0))
    state_n1 = pl.BlockSpec((None, None, M_DQK, 1), lambda b, h, c: (b, h, 0, 0))
    state_n = pl.BlockSpec((None, None, M_DQK, LANES), lambda b, h, c: (b, h, 0, 0))
    state_m = pl.BlockSpec((None, None, 1, LANES), lambda b, h, c: (b, h, 0, 0))
    return pl.pallas_call(
        _mlstm_kernel,
        grid=(n_streams, M_HEADS, nb),
        in_specs=[pl.BlockSpec(memory_space=pltpu.SMEM),
                  zcol(0), zcol(1), zcol(2), zcol(3), gate(0), gate(1),
                  pl.BlockSpec((1, M_DV), lambda b, h, c: (0, h)),
                  state_c, state_n1, state_m],
        out_specs=[pl.BlockSpec((L, M_DV), lambda b, h, c: (b * nb + c, h)), state_c, state_n, state_m],
        out_shape=[jax.ShapeDtypeStruct((t, M_WIDTH), bf16),
                   jax.ShapeDtypeStruct((n_streams, M_HEADS, M_DQK, M_DV), f32),
                   jax.ShapeDtypeStruct((n_streams, M_HEADS, M_DQK, LANES), f32),
                   jax.ShapeDtypeStruct((n_streams, M_HEADS, 1, LANES), f32)],
        scratch_shapes=[pltpu.VMEM((M_DQK, M_DV + LANES), f32), pltpu.VMEM((1, LANES), f32)],
        compiler_params=_params("parallel", "parallel", "arbitrary"),
        name="mlstm",
    )(gate_bias, z, z, z, z, gates4, gates4, norm_h, c0, n0c, m0b)


def _fox_prompt_kernel(q_ref, k_ref, v_ref, cum_ref, o_ref, sa_ref, sb_ref, xa_ref, xb_ref, m_ref, l_ref, acc_ref):
    i = pl.program_id(2)
    tq = q_ref.shape[0]

    def causal(s):
        row = lax.broadcasted_iota(jnp.int32, (tq, tq), 0)
        col = lax.broadcasted_iota(jnp.int32, (tq, tq), 1)
        return jnp.where(col <= row, s, -jnp.inf)

    def scores(j, s_ref, x_ref, masked):
        k = k_ref[pl.ds(pl.multiple_of(j * tq, tq), tq), :]
        s = lax.dot_general(q_ref[...], k, _NT, preferred_element_type=f32) - cum_ref[j]
        if masked:
            s = causal(s)
        s_ref[...] = s
        x_ref[...] = jnp.broadcast_to(jnp.max(s, axis=1, keepdims=True), x_ref.shape)

    def absorb(j, s_ref, x_ref, mask_now=False):
        s = s_ref[...]
        if mask_now:
            s = causal(s)
            x = jnp.max(s, axis=1, keepdims=True)
        else:
            x = x_ref[...]
        m_old = m_ref[...]
        m_new = jnp.maximum(m_old, x)
        alpha = jnp.exp2(m_old - m_new)
        p = jnp.exp2(s - jnp.concatenate([m_new] * (tq // LANES), axis=1))
        v = v_ref[pl.ds(pl.multiple_of(j * tq, tq), tq), :]
        l_ref[...] = alpha * l_ref[...] + jnp.sum(p, axis=1, keepdims=True)
        acc_ref[...] = alpha * acc_ref[...] + jnp.dot(p.astype(bf16), v, preferred_element_type=f32)
        m_ref[...] = m_new

    m_ref[...] = jnp.full(m_ref.shape, -jnp.inf, f32)
    l_ref[...] = jnp.zeros(l_ref.shape, f32)
    acc_ref[...] = jnp.zeros(acc_ref.shape, f32)
    scores(0, sa_ref, xa_ref, False)

    def pair(jj, carry):
        j0 = 2 * jj
        scores(j0 + 1, sb_ref, xb_ref, False)
        absorb(j0, sa_ref, xa_ref)
        scores(j0 + 2, sa_ref, xa_ref, False)
        absorb(j0 + 1, sb_ref, xb_ref)
        return carry

    lax.fori_loop(0, i // 2, pair, 0)

    @pl.when(i % 2 == 0)
    def _():
        absorb(i, sa_ref, xa_ref, mask_now=True)

    @pl.when(i % 2 == 1)
    def _():
        scores(i, sb_ref, xb_ref, True)
        absorb(i - 1, sa_ref, xa_ref)
        absorb(i, sb_ref, xb_ref)

    o_ref[...] = (acc_ref[...] / l_ref[...]).astype(o_ref.dtype)


def _fox_prompt(z, k, v, cum_t, *, q_col0, n_streams):
    t = z.shape[0]
    s = t // n_streams
    tq = _tile(s, ATTN_BLOCK)
    nq = s // tq
    cum4 = cum_t.reshape(F_HEADS, n_streams * nq, 1, tq)
    q0 = q_col0 // F_HEAD_DIM
    return pl.pallas_call(
        _fox_prompt_kernel,
        grid=(n_streams, F_HEADS, nq),
        in_specs=[pl.BlockSpec((tq, F_HEAD_DIM), lambda b, h, i: (b * nq + i, q0 + h)),
                  pl.BlockSpec((s, F_HEAD_DIM), lambda b, h, i: (b, h)),
                  pl.BlockSpec((s, F_HEAD_DIM), lambda b, h, i: (b, h)),
                  pl.BlockSpec((None, nq, 1, tq), lambda b, h, i: (h, b, 0, 0))],
        out_specs=pl.BlockSpec((tq, F_HEAD_DIM), lambda b, h, i: (b * nq + i, h)),
        out_shape=jax.ShapeDtypeStruct((t, F_WIDTH), bf16),
        scratch_shapes=[pltpu.VMEM((tq, tq), f32), pltpu.VMEM((tq, tq), f32)]
        + [pltpu.VMEM((tq, LANES), f32)] * 5,
        compiler_params=_params("parallel", "parallel", "arbitrary"),
        name="fox_prompt",
    )(z, k, v, cum4)


def _fox_decode_kernel(bias_ref, q_ref, kn_ref, vn_ref, kc_ref, vc_ref, cumc_ref, fg_ref, o_ref, logf_ref):
    head = pl.program_id(1)
    L = q_ref.shape[0]
    q = q_ref[...]
    logf = _log_sigmoid(fg_ref[...] + bias_ref[head])
    logf_ref[...] = logf
    cum_new = _cumsum_lanes(logf) * LOG2E
    cum_c = cumc_ref[...]
    cum_c = cum_c - cum_c[:, -1:]

    s_c = lax.dot_general(q, kc_ref[...].astype(bf16), _NT, preferred_element_type=f32) - cum_c
    s_n = lax.dot_general(q, kn_ref[...], _NT, preferred_element_type=f32) - cum_new
    row = lax.broadcasted_iota(jnp.int32, (L, L), 0)
    col = lax.broadcasted_iota(jnp.int32, (L, L), 1)
    s_n = jnp.where(col <= row, s_n, -jnp.inf)
    m = jnp.maximum(jnp.max(s_c, axis=1, keepdims=True), jnp.max(s_n, axis=1, keepdims=True))
    p_c = jnp.exp2(s_c - m)
    p_n = jnp.exp2(s_n - m)
    l = jnp.sum(p_c, axis=1, keepdims=True) + jnp.sum(p_n, axis=1, keepdims=True)
    acc = (jnp.dot(p_c.astype(bf16), vc_ref[...].astype(bf16), preferred_element_type=f32)
           + jnp.dot(p_n.astype(bf16), vn_ref[...], preferred_element_type=f32))
    o_ref[...] = (acc / l).astype(o_ref.dtype)


def _fox_decode(z, k, v, gates_t, bias, cache_k, cache_v, cum_cache, *, q_col0, n_streams):
    t = z.shape[0]
    L = t // n_streams
    p = cache_k.shape[1]
    gates4 = gates_t.reshape(N_GATE_ROWS, n_streams, 1, L)
    q0 = q_col0 // F_HEAD_DIM
    return pl.pallas_call(
        _fox_decode_kernel,
        grid=(n_streams, F_HEADS),
        in_specs=[pl.BlockSpec(memory_space=pltpu.SMEM),
                  pl.BlockSpec((L, F_HEAD_DIM), lambda b, h: (b, q0 + h)),
                  pl.BlockSpec((L, F_HEAD_DIM), lambda b, h: (b, h)),
                  pl.BlockSpec((L, F_HEAD_DIM), lambda b, h: (b, h)),
                  pl.BlockSpec((None, p, F_HEAD_DIM), lambda b, h: (b, 0, h)),
                  pl.BlockSpec((None, p, F_HEAD_DIM), lambda b, h: (b, 0, h)),
                  pl.BlockSpec((None, 1, p), lambda b, h: (b * F_HEADS + h, 0, 0)),
                  pl.BlockSpec((None, None, 1, L), lambda b, h: (2 * M_HEADS + h, b, 0, 0))],
        out_specs=[pl.BlockSpec((L, F_HEAD_DIM), lambda b, h: (b, h)),
                   pl.BlockSpec((None, None, 1, L), lambda b, h: (b, h, 0, 0))],
        out_shape=[jax.ShapeDtypeStruct((t, F_WIDTH), bf16),
                   jax.ShapeDtypeStruct((n_streams, F_HEADS, 1, L), f32)],
        compiler_params=_params("parallel", "parallel"),
        name="fox_decode",
    )(bias, z, k, v, cache_k, cache_v, cum_cache, gates4)


def _merge_kernel(x_ref, ha_ref, hb_ref, ga_ref, gb_ref, wa_ref, wb_ref, wo_ref, g_ref, x1_ref, hn_ref):
    pa = jnp.dot(ha_ref[...], wa_ref[...], preferred_element_type=f32)
    pb = jnp.dot(hb_ref[...], wb_ref[...], preferred_element_type=f32)
    merged = (jax.nn.sigmoid(ga_ref[...].astype(f32)) * pa + jax.nn.sigmoid(gb_ref[...].astype(f32)) * pb)
    x1 = x_ref[...] + jnp.dot(merged.astype(bf16), wo_ref[...], preferred_element_type=f32)
    x1_ref[...] = x1
    hn = x1 * lax.rsqrt(jnp.mean(x1 * x1, axis=-1, keepdims=True) + EPS) * g_ref[...]
    hn_ref[...] = hn.astype(hn_ref.dtype)


def _merge(x, h_a, h_b, z, w_a, w_b, w_o, g, *, gate_col0):
    t, d = x.shape
    tm = _tile(t, 256)
    ga_blk = gate_col0 // d
    row = lambda i: (i, 0)
    fixed = lambda i: (0, 0)
    resident = functools.partial(pl.BlockSpec, index_map=fixed, pipeline_mode=pl.Buffered(1))
    return pl.pallas_call(
        _merge_kernel,
        grid=(t // tm,),
        in_specs=[pl.BlockSpec((tm, d), row),
                  pl.BlockSpec((tm, M_WIDTH), row),
                  pl.BlockSpec((tm, F_WIDTH), row),
                  pl.BlockSpec((tm, d), lambda i: (i, ga_blk)),
                  pl.BlockSpec((tm, d), lambda i: (i, ga_blk + 1)),
                  resident(w_a.shape), resident(w_b.shape), resident(w_o.shape),
                  pl.BlockSpec((1, d), fixed)],
        out_specs=[pl.BlockSpec((tm, d), row), pl.BlockSpec((tm, d), row)],
        out_shape=[jax.ShapeDtypeStruct((t, d), f32), jax.ShapeDtypeStruct((t, d), bf16)],
        compiler_params=_params("parallel"),
        name="merge",
    )(x, h_a, h_b, z, z, w_a, w_b, w_o, g)


def _ffn_kernel(hn_ref, x1_ref, wu_ref, wd_ref, g_ref, y_ref):
    f = pl.program_id(1)
    u = jnp.maximum(jnp.dot(hn_ref[...], wu_ref[...], preferred_element_type=f32), 0.0)
    part = jnp.dot((u * u).astype(bf16), wd_ref[...], preferred_element_type=f32)

    @pl.when(f == 0)
    def _():
        y_ref[...] = x1_ref[...] + part

    @pl.when(f > 0)
    def _():
        y_ref[...] += part

    @pl.when(f == pl.num_programs(1) - 1)
    def _():
        x2 = y_ref[...]
        y_ref[...] = x2 * lax.rsqrt(jnp.mean(x2 * x2, axis=-1, keepdims=True) + EPS) * g_ref[...]


def _ffn(hn, x1, w_up, w_down, g):
    t, d = x1.shape
    dff = w_up.shape[1]
    tm = _tile(t, 512)
    tf = _tile(dff, 1024)
    return pl.pallas_call(
        _ffn_kernel,
        grid=(t // tm, dff // tf),
        in_specs=[pl.BlockSpec((tm, d), lambda i, f: (i, 0)),
                  pl.BlockSpec((tm, d), lambda i, f: (i, 0)),
                  pl.BlockSpec((d, tf), lambda i, f: (0, f)),
                  pl.BlockSpec((tf, d), lambda i, f: (f, 0)),
                  pl.BlockSpec((1, d), lambda i, f: (0, 0))],
        out_specs=pl.BlockSpec((tm, d), lambda i, f: (i, 0)),
        out_shape=jax.ShapeDtypeStruct((t, d), f32),
        compiler_params=_params("parallel", "arbitrary"),
        name="ffn",
    )(hn, x1, w_up, w_down, g)


def _layer(x3, w, mstate, fox_past):
    nstr, frames, d = x3.shape
    t = nstr * frames
    x = x3.reshape(t, d)
    xn, gates_t = _norm_gates(x, w["norm_mix"], w["w_gates_t"])
    k32, k16 = _matmul(xn, w["w_fk"], (f32, bf16), "proj_k")
    v32, v16 = _matmul(xn, w["w_fv"], (f32, bf16), "proj_v")
    (z,) = _matmul(xn, w["w_rest"], (bf16,), "proj_rest")

    c0, n0, m0 = mstate
    h_a, c_new, n_slab, m_slab = _mlstm(z, gates_t, w["mlstm_bias"], w["norm_mlstm_h"], c0, n0, m0,
                                        n_streams=nstr)
    n_new = n_slab[..., 0]
    m_new = m_slab[:, :, 0, 0]

    if fox_past is None:
        logf_t, cum_t = _scan_rows(gates_t, w["b_fox_f_col"], row_block=1, rows=F_HEADS, n_streams=nstr,
                                   apply_log_sigmoid=True)
        h_b = _fox_prompt(z, k16, v16, cum_t, q_col0=w["q_col0"], n_streams=nstr)
        logf = logf_t.T.reshape(nstr, frames, F_HEADS)
    else:
        ck, cv, clf = fox_past
        p = ck.shape[1]
        clf_t = jnp.transpose(clf, (0, 2, 1)).reshape(nstr * F_HEADS, p)
        _, cum_c = _scan_rows(clf_t, jnp.zeros((nstr * F_HEADS, 1), f32), row_block=0, rows=nstr * F_HEADS,
                              n_streams=1, apply_log_sigmoid=False)
        h_b, logf4 = _fox_decode(z, k16, v16, gates_t, w["b_fox_f"], ck.reshape(nstr, p, F_WIDTH),
                                 cv.reshape(nstr, p, F_WIDTH), cum_c.reshape(nstr * F_HEADS, 1, p),
                                 q_col0=w["q_col0"], n_streams=nstr)
        logf = jnp.transpose(logf4[:, :, 0, :], (0, 2, 1))

    x1, hn = _merge(x, h_a, h_b, z, w["w_branch_a"], w["w_branch_b"], w["w_out"], w["norm_ffn"],
                    gate_col0=w["gate_col0"])
    y = _ffn(hn, x1, w["w_up"], w["w_down"], w["norm_final"])

    k_rows = k32.reshape(nstr, frames, F_HEADS, F_HEAD_DIM)
    v_rows = v32.reshape(nstr, frames, F_HEADS, F_HEAD_DIM)
    return y.reshape(nstr, frames, d), (k_rows, v_rows, logf, c_new, n_new, m_new)


def _prepare_weights(norm_mix, w_in, b_mlstm_i, b_mlstm_f, b_fox_f, norm_mlstm_h, w_branch_a, w_branch_b,
                     w_out, norm_ffn, w_up, w_down, norm_final):
    d = w_in.shape[0]
    sizes = (M_HEADS * M_DQK, M_HEADS * M_DQK, M_WIDTH, M_WIDTH, M_HEADS, M_HEADS,
             F_WIDTH, F_WIDTH, F_WIDTH, F_HEADS, d, d)
    offs = [0]
    for s in sizes:
        offs.append(offs[-1] + s)
    mq, mk, mv, mo, mi, mf, fq, fk, fv, ff, ga, gb = (w_in[:, offs[i]:offs[i + 1]] for i in range(12))
    w_rest = jnp.concatenate([mq, mk * (M_DQK ** -0.5), mv, mo, ga, gb, fq * (F_HEAD_DIM ** -0.5 * LOG2E)], axis=1)
    return {
        "norm_mix": norm_mix.reshape(1, d),
        "w_gates_t": jnp.concatenate([mi, mf, ff], axis=1).T.astype(bf16),
        "w_fk": fk.astype(bf16),
        "w_fv": fv.astype(bf16),
        "w_rest": w_rest.astype(bf16),
        "gate_col0": 4 * M_WIDTH,
        "q_col0": 4 * M_WIDTH + 2 * d,
        "mlstm_bias": jnp.concatenate([b_mlstm_i, b_mlstm_f]).astype(f32),
        "b_fox_f": b_fox_f.astype(f32),
        "b_fox_f_col": b_fox_f.astype(f32).reshape(F_HEADS, 1),
        "norm_mlstm_h": norm_mlstm_h.reshape(1, M_WIDTH),
        "w_branch_a": w_branch_a.astype(bf16),
        "w_branch_b": w_branch_b.astype(bf16),
        "w_out": w_out.astype(bf16),
        "norm_ffn": norm_ffn.reshape(1, d),
        "w_up": w_up.astype(bf16),
        "w_down": w_down.astype(bf16),
        "norm_final": norm_final.reshape(1, d),
    }


def kernel(x_prompt, x_sample, cache_fox_k, cache_fox_v, cache_fox_logf, state_mlstm_c, state_mlstm_n, state_mlstm_m, norm_mix, w_in, b_mlstm_i, b_mlstm_f, b_fox_f, norm_mlstm_h, w_branch_a, w_branch_b, w_out, norm_ffn, w_up, w_down, norm_final):
    depth = w_in.shape[0]
    assert depth == 1, "the final norm is fused into the layer's FFN kernel"
    w = _prepare_weights(norm_mix[0], w_in[0], b_mlstm_i[0], b_mlstm_f[0], b_fox_f[0], norm_mlstm_h[0],
                         w_branch_a[0], w_branch_b[0], w_out[0], norm_ffn[0], w_up[0], w_down[0], norm_final)
    bp = x_prompt.shape[0]
    fresh = (jnp.zeros((bp, M_HEADS, M_DQK, M_DV), f32), jnp.zeros((bp, M_HEADS, M_DQK), f32),
             jnp.zeros((bp, M_HEADS), f32))
    y_p, st_p = _layer(x_prompt, w, fresh, None)
    y_s, st_s = _layer(x_sample, w, (state_mlstm_c[0], state_mlstm_n[0], state_mlstm_m[0]),
                       (cache_fox_k[0], cache_fox_v[0], cache_fox_logf[0]))
    return (y_p, y_s) + tuple(a[None] for a in st_p) + tuple(a[None] for a in st_s)
```

```python
import functools

import jax
import jax.numpy as jnp
from jax import lax
from jax.experimental import pallas as pl
from jax.experimental.pallas import tpu as pltpu

M_HEADS = 4
M_DQK = 256
M_DV = 256
M_WIDTH = M_HEADS * M_DV
F_HEADS = 8
F_HEAD_DIM = 128
F_WIDTH = F_HEADS * F_HEAD_DIM
EPS = 1e-6
N_GATE_ROWS = 2 * M_HEADS + F_HEADS
LANES = 128
assert F_HEAD_DIM == LANES
MLSTM_BLOCK = 256
ATTN_BLOCK = 512
LOG2E = 1.4426950408889634
VMEM_LIMIT_BYTES = 56 * 1024 * 1024

f32 = jnp.float32
bf16 = jnp.bfloat16

_NT = (((1,), (1,)), ((), ()))
_TN = (((0,), (0,)), ((), ()))


def _tile(n, pref):
    t = min(n, pref)
    while n % t:
        t //= 2
    return t


def _params(*sem):
    return pltpu.CompilerParams(dimension_semantics=sem, vmem_limit_bytes=VMEM_LIMIT_BYTES)


def _log_sigmoid(z):
    return jnp.minimum(z, 0.0) - jnp.log1p(jnp.exp(-jnp.abs(z)))


def _cumsum_lanes(x):
    rows, n = x.shape
    pad = -rows % 16
    if rows == 1:
        x16 = jnp.broadcast_to(x, (16, n))
    elif pad == 0:
        x16 = x
    else:
        x16 = jnp.concatenate([x, jnp.zeros((pad, n), f32)], axis=0)
    r = lax.broadcasted_iota(jnp.int32, (n, n), 0)
    c = lax.broadcasted_iota(jnp.int32, (n, n), 1)
    u = jnp.where(r <= c, 1.0, 0.0).astype(bf16)
    hi = x16.astype(bf16)
    rem = x16 - hi.astype(f32)
    mid = rem.astype(bf16)
    lo = (rem - mid.astype(f32)).astype(bf16)
    out = (jnp.dot(hi, u, preferred_element_type=f32) + jnp.dot(mid, u, preferred_element_type=f32)
           + jnp.dot(lo, u, preferred_element_type=f32))
    return out[:rows]


def _norm_kernel(x_ref, g_ref, wg_ref, xn_ref, gt_ref):
    x = x_ref[...]
    y = x * lax.rsqrt(jnp.mean(x * x, axis=-1, keepdims=True) + EPS) * g_ref[...]
    xn = y.astype(bf16)
    xn_ref[...] = xn
    gt_ref[...] = lax.dot_general(wg_ref[...], xn, _NT, preferred_element_type=f32)


def _norm_gates(x, g, wg_t):
    t, d = x.shape
    tm = _tile(t, 512)
    return pl.pallas_call(
        _norm_kernel,
        grid=(t // tm,),
        in_specs=[pl.BlockSpec((tm, d), lambda i: (i, 0)),
                  pl.BlockSpec((1, d), lambda i: (0, 0)),
                  pl.BlockSpec((N_GATE_ROWS, d), lambda i: (0, 0))],
        out_specs=[pl.BlockSpec((tm, d), lambda i: (i, 0)),
                   pl.BlockSpec((N_GATE_ROWS, tm), lambda i: (0, i))],
        out_shape=[jax.ShapeDtypeStruct((t, d), bf16), jax.ShapeDtypeStruct((N_GATE_ROWS, t), f32)],
        compiler_params=_params("parallel"),
        name="norm_gates",
    )(x, g, wg_t)


def _mm_kernel(a_ref, w_ref, *o_refs):
    r = jnp.dot(a_ref[...], w_ref[...], preferred_element_type=f32)
    for o in o_refs:
        o[...] = r.astype(o.dtype)


def _matmul(a, w, out_dtypes, name):
    t, k = a.shape
    n = w.shape[1]
    tm = _tile(t, 1024)
    tn = _tile(n, 1024)
    return pl.pallas_call(
        _mm_kernel,
        grid=(t // tm, n // tn),
        in_specs=[pl.BlockSpec((tm, k), lambda i, j: (i, 0)),
                  pl.BlockSpec((k, tn), lambda i, j: (0, j))],
        out_specs=[pl.BlockSpec((tm, tn), lambda i, j: (i, j)) for _ in out_dtypes],
        out_shape=[jax.ShapeDtypeStruct((t, n), dt) for dt in out_dtypes],
        compiler_params=_params("parallel", "arbitrary"),
        name=name,
    )(a, w)


def _scan_kernel(x_ref, bias_ref, logf_ref, cum_ref, carry_ref, *, apply_log_sigmoid):
    @pl.when(pl.program_id(1) == 0)
    def _():
        carry_ref[...] = jnp.zeros_like(carry_ref)

    x = x_ref[...]
    if apply_log_sigmoid:
        x = _log_sigmoid(x + bias_ref[...])
    logf_ref[...] = x
    cum = _cumsum_lanes(x) + carry_ref[:, :1]
    cum_ref[...] = cum * LOG2E
    carry_ref[...] = jnp.broadcast_to(cum[:, -1:], carry_ref.shape)


def _scan_rows(x, bias, *, row_block, rows, n_streams, apply_log_sigmoid):
    total = x.shape[1]
    s = total // n_streams
    tb = _tile(s, 512)
    nb = s // tb
    spec = pl.BlockSpec((rows, tb), lambda b, j: (0, b * nb + j))
    return pl.pallas_call(
        functools.partial(_scan_kernel, apply_log_sigmoid=apply_log_sigmoid),
        grid=(n_streams, nb),
        in_specs=[pl.BlockSpec((rows, tb), lambda b, j: (row_block, b * nb + j)),
                  pl.BlockSpec((rows, 1), lambda b, j: (0, 0))],
        out_specs=[spec, spec],
        out_shape=[jax.ShapeDtypeStruct((rows, total), f32)] * 2,
        scratch_shapes=[pltpu.VMEM((rows, LANES), f32)],
        compiler_params=_params("arbitrary", "arbitrary"),
        name="logf_scan",
    )(x, bias)


def _mlstm_kernel(bias_ref, q_ref, k_ref, v_ref, o_ref, ig_ref, fg_ref, nh_ref, c0_ref, n0_ref, m0_ref,
                  h_ref, c_out_ref, n_out_ref, m_out_ref, caug_ref, m_ref):
    head = pl.program_id(1)
    blk = pl.program_id(2)
    n_blk = pl.num_programs(2)
    L = q_ref.shape[0]

    @pl.when(blk == 0)
    def _():
        lane = lax.broadcasted_iota(jnp.int32, (M_DQK, LANES), 1)
        caug_ref[:, :M_DV] = c0_ref[...]
        caug_ref[:, M_DV:] = jnp.where(lane == 0, n0_ref[...], 0.0)
        m_ref[...] = m0_ref[...]

    m0 = m_ref[:, :1]
    ig = ig_ref[...] + bias_ref[head]
    lf = _log_sigmoid(fg_ref[...] + bias_ref[M_HEADS + head])
    b_row = _cumsum_lanes(lf)
    a_row = ig - b_row

    t_idx = lax.broadcasted_iota(jnp.int32, (L, L), 0)
    s_idx = lax.broadcasted_iota(jnp.int32, (L, L), 1)
    causal = s_idx <= t_idx
    a_mat = jnp.where(causal, a_row, -jnp.inf)
    g_col = jnp.maximum(m0, jnp.max(a_mat, axis=1, keepdims=True))
    b_col = jnp.sum(jnp.where(causal, lf, 0.0), axis=1, keepdims=True)
    w_intra = jnp.exp(a_mat - g_col)
    w_inter = jnp.exp(m0 - g_col)

    q = q_ref[...]
    k = k_ref[...]
    ones_col = jnp.where(lax.broadcasted_iota(jnp.int32, (L, LANES), 1) == 0, 1.0, 0.0).astype(bf16)
    v_aug = jnp.concatenate([v_ref[...], ones_col], axis=1)

    s = lax.dot_general(q, k, _NT, preferred_element_type=f32)
    sw = (s * w_intra).astype(bf16)
    num_aug = (jnp.dot(sw, v_aug, preferred_element_type=f32)
               + w_inter * jnp.dot(q, caug_ref[...].astype(bf16), preferred_element_type=f32))
    num = num_aug[:, :M_DV]
    den = num_aug[:, M_DV:M_DV + 1]
    den = jnp.maximum(jnp.abs(den), jnp.exp(-(b_col + g_col)))
    h = num / den
    h = h * lax.rsqrt(jnp.mean(h * h, axis=-1, keepdims=True) + EPS)
    h = h * nh_ref[...] * jax.nn.sigmoid(o_ref[...].astype(f32))
    h_ref[...] = h.astype(h_ref.dtype)

    g_end = jnp.maximum(m0, jnp.max(a_row, axis=1, keepdims=True))
    b_end = jnp.sum(lf, axis=1, keepdims=True)
    diag = s_idx == t_idx
    ig_col = jnp.sum(jnp.where(diag, ig, 0.0), axis=1, keepdims=True)
    w_tok = jnp.exp(ig_col - b_col - g_end)
    w_state = jnp.exp(m0 - g_end)
    kw = (k.astype(f32) * w_tok).astype(bf16)
    caug_ref[...] = w_state * caug_ref[...] + lax.dot_general(kw, v_aug, _TN, preferred_element_type=f32)
    m_ref[...] = jnp.broadcast_to(b_end + g_end, m_ref.shape)

    @pl.when(blk == n_blk - 1)
    def _():
        c_out_ref[...] = caug_ref[:, :M_DV]
        n_out_ref[...] = caug_ref[:, M_DV:]
        m_out_ref[...] = m_ref[...]


def _mlstm(z, gates_t, gate_bias, norm_h, c0, n0, m0, *, n_streams):
    t = z.shape[0]
    s = t // n_streams
    L = _tile(s, MLSTM_BLOCK)
    nb = s // L
    gates4 = gates_t.reshape(N_GATE_ROWS, n_streams * nb, 1, L)
    n0c = n0.reshape(n_streams, M_HEADS, M_DQK, 1)
    m0b = jnp.broadcast_to(m0.reshape(n_streams, M_HEADS, 1, 1), (n_streams, M_HEADS, 1, LANES))

    def zcol(group):
        return pl.BlockSpec((L, M_DV), lambda b, h, c: (b * nb + c, group * M_HEADS + h))

    def gate(group):
        return pl.BlockSpec((None, None, 1, L), lambda b, h, c: (group * M_HEADS + h, b * nb + c, 0, 0))

    state_c = pl.BlockSpec((None, None, M_DQK, M_DV), lambda b, h, c: (b, h, 0, 0))
    state_n1 = pl.BlockSpec((None, None, M_DQK, 1), lambda b, h, c: (b, h, 0, 0))
    state_n = pl.BlockSpec((None, None, M_DQK, LANES), lambda b, h, c: (b, h, 0, 0))
    state_m = pl.BlockSpec((None, None, 1, LANES), lambda b, h, c: (b, h, 0, 0))
    return pl.pallas_call(
        _mlstm_kernel,
        grid=(n_streams, M_HEADS, nb),
        in_specs=[pl.BlockSpec(memory_space=pltpu.SMEM),
                  zcol(0), zcol(1), zcol(2), zcol(3), gate(0), gate(1),
                  pl.BlockSpec((1, M_DV), lambda b, h, c: (0, h)),
                  state_c, state_n1, state_m],
        out_specs=[pl.BlockSpec((L, M_DV), lambda b, h, c: (b * nb + c, h)), state_c, state_n, state_m],
        out_shape=[jax.ShapeDtypeStruct((t, M_WIDTH), bf16),
                   jax.ShapeDtypeStruct((n_streams, M_HEADS, M_DQK, M_DV), f32),
                   jax.ShapeDtypeStruct((n_streams, M_HEADS, M_DQK, LANES), f32),
                   jax.ShapeDtypeStruct((n_streams, M_HEADS, 1, LANES), f32)],
        scratch_shapes=[pltpu.VMEM((M_DQK, M_DV + LANES), f32), pltpu.VMEM((1, LANES), f32)],
        compiler_params=_params("parallel", "parallel", "arbitrary"),
        name="mlstm",
    )(gate_bias, z, z, z, z, gates4, gates4, norm_h, c0, n0c, m0b)


def _fox_prompt_kernel(q_ref, k_ref, v_ref, cum_ref, o_ref,
                       sa_ref, sb_ref, pa_ref, pb_ref, xa_ref, xb_ref, m_ref, l_ref, acc_ref):
    i = pl.program_id(2)
    tq = q_ref.shape[0]

    def rows(ref, j):
        return ref[pl.ds(pl.multiple_of(j * tq, tq), tq), :]

    def causal(s):
        row = lax.broadcasted_iota(jnp.int32, (tq, tq), 0)
        col = lax.broadcasted_iota(jnp.int32, (tq, tq), 1)
        return jnp.where(col <= row, s, -jnp.inf)

    def scores(j, s_ref, x_ref, masked):
        s = lax.dot_general(q_ref[...], rows(k_ref, j), _NT, preferred_element_type=f32) - cum_ref[j]
        if masked:
            s = causal(s)
        s_ref[...] = s
        x_ref[...] = jnp.broadcast_to(jnp.max(s, axis=1, keepdims=True), x_ref.shape)

    def values(j, p_ref):
        return jnp.dot(p_ref[...], rows(v_ref, j), preferred_element_type=f32)

    def softmax(s_ref, x_ref, p_ref, pv_prev, mask_now=False):
        s = s_ref[...]
        if mask_now:
            s = causal(s)
            x = jnp.max(s, axis=1, keepdims=True)
        else:
            x = x_ref[...]
        m_old = m_ref[...]
        m_new = jnp.maximum(m_old, x)
        alpha = jnp.exp2(m_old - m_new)
        p = jnp.exp2(s - jnp.concatenate([m_new] * (tq // LANES), axis=1))
        p_ref[...] = p.astype(bf16)
        l_ref[...] = alpha * l_ref[...] + jnp.sum(p, axis=1, keepdims=True)
        acc_ref[...] = alpha * (acc_ref[...] + pv_prev)
        m_ref[...] = m_new

    m_ref[...] = jnp.full(m_ref.shape, -jnp.inf, f32)
    l_ref[...] = jnp.zeros(l_ref.shape, f32)
    acc_ref[...] = jnp.zeros(acc_ref.shape, f32)
    pb_ref[...] = jnp.zeros(pb_ref.shape, bf16)
    scores(0, sa_ref, xa_ref, False)

    def pair(j0):
        scores(j0 + 1, sb_ref, xb_ref, False)
        softmax(sa_ref, xa_ref, pa_ref, values(jnp.maximum(j0 - 1, 0), pb_ref))
        scores(j0 + 2, sa_ref, xa_ref, False)
        softmax(sb_ref, xb_ref, pb_ref, values(j0, pa_ref))

    def quad(jj, carry):
        pair(4 * jj)
        pair(4 * jj + 2)
        return carry

    def last_pair(jj, carry):
        pair(4 * (i // 4))
        return carry

    lax.fori_loop(0, i // 4, quad, 0)
    lax.fori_loop(0, (i % 4) // 2, last_pair, 0)

    @pl.when(i % 2 == 0)
    def _():
        softmax(sa_ref, xa_ref, pa_ref, values(jnp.maximum(i - 1, 0), pb_ref), mask_now=True)
        acc_ref[...] += values(i, pa_ref)

    @pl.when(i % 2 == 1)
    def _():
        scores(i, sb_ref, xb_ref, True)
        softmax(sa_ref, xa_ref, pa_ref, values(jnp.maximum(i - 2, 0), pb_ref))
        softmax(sb_ref, xb_ref, pb_ref, values(i - 1, pa_ref))
        acc_ref[...] += values(i, pb_ref)

    o_ref[...] = (acc_ref[...] / l_ref[...]).astype(o_ref.dtype)


def _fox_prompt(z, k, v, cum_t, *, q_col0, n_streams):
    t = z.shape[0]
    s = t // n_streams
    tq = _tile(s, ATTN_BLOCK)
    nq = s // tq
    cum4 = cum_t.reshape(F_HEADS, n_streams * nq, 1, tq)
    q0 = q_col0 // F_HEAD_DIM
    return pl.pallas_call(
        _fox_prompt_kernel,
        grid=(n_streams, F_HEADS, nq),
        in_specs=[pl.BlockSpec((tq, F_HEAD_DIM), lambda b, h, i: (b * nq + i, q0 + h)),
                  pl.BlockSpec((s, F_HEAD_DIM), lambda b, h, i: (b, h)),
                  pl.BlockSpec((s, F_HEAD_DIM), lambda b, h, i: (b, h)),
                  pl.BlockSpec((None, nq, 1, tq), lambda b, h, i: (h, b, 0, 0))],
        out_specs=pl.BlockSpec((tq, F_HEAD_DIM), lambda b, h, i: (b * nq + i, h)),
        out_shape=jax.ShapeDtypeStruct((t, F_WIDTH), bf16),
        scratch_shapes=[pltpu.VMEM((tq, tq), f32)] * 2 + [pltpu.VMEM((tq, tq), bf16)] * 2
        + [pltpu.VMEM((tq, LANES), f32)] * 5,
        compiler_params=_params("parallel", "parallel", "arbitrary"),
        name="fox_prompt",
    )(z, k, v, cum4)


def _fox_decode_kernel(bias_ref, q_ref, kn_ref, vn_ref, kc_ref, vc_ref, cumc_ref, fg_ref, o_ref, logf_ref):
    head = pl.program_id(1)
    L = q_ref.shape[0]
    q = q_ref[...]
    logf = _log_sigmoid(fg_ref[...] + bias_ref[head])
    logf_ref[...] = logf
    cum_new = _cumsum_lanes(logf) * LOG2E
    cum_c = cumc_ref[...]
    cum_c = cum_c - cum_c[:, -1:]

    p_len = kc_ref.shape[0] // F_HEADS
    kc = kc_ref[pl.ds(head, p_len, stride=F_HEADS), :].astype(bf16)
    vc = vc_ref[pl.ds(head, p_len, stride=F_HEADS), :].astype(bf16)
    s_c = lax.dot_general(q, kc, _NT, preferred_element_type=f32) - cum_c
    s_n = lax.dot_general(q, kn_ref[...], _NT, preferred_element_type=f32) - cum_new
    row = lax.broadcasted_iota(jnp.int32, (L, L), 0)
    col = lax.broadcasted_iota(jnp.int32, (L, L), 1)
    s_n = jnp.where(col <= row, s_n, -jnp.inf)
    m = jnp.maximum(jnp.max(s_c, axis=1, keepdims=True), jnp.max(s_n, axis=1, keepdims=True))
    p_c = jnp.exp2(s_c - m)
    p_n = jnp.exp2(s_n - m)
    l = jnp.sum(p_c, axis=1, keepdims=True) + jnp.sum(p_n, axis=1, keepdims=True)
    acc = (jnp.dot(p_c.astype(bf16), vc, preferred_element_type=f32)
           + jnp.dot(p_n.astype(bf16), vn_ref[...], preferred_element_type=f32))
    o_ref[...] = (acc / l).astype(o_ref.dtype)


def _fox_decode(z, k, v, gates_t, bias, cache_k, cache_v, cum_cache, *, q_col0, n_streams):
    t = z.shape[0]
    L = t // n_streams
    p = cache_k.shape[1] // F_HEADS
    gates4 = gates_t.reshape(N_GATE_ROWS, n_streams, 1, L)
    q0 = q_col0 // F_HEAD_DIM
    return pl.pallas_call(
        _fox_decode_kernel,
        grid=(n_streams, F_HEADS),
        in_specs=[pl.BlockSpec(memory_space=pltpu.SMEM),
                  pl.BlockSpec((L, F_HEAD_DIM), lambda b, h: (b, q0 + h)),
                  pl.BlockSpec((L, F_HEAD_DIM), lambda b, h: (b, h)),
                  pl.BlockSpec((L, F_HEAD_DIM), lambda b, h: (b, h)),
                  pl.BlockSpec((None, p * F_HEADS, F_HEAD_DIM), lambda b, h: (b, 0, 0)),
                  pl.BlockSpec((None, p * F_HEADS, F_HEAD_DIM), lambda b, h: (b, 0, 0)),
                  pl.BlockSpec((None, 1, p), lambda b, h: (b * F_HEADS + h, 0, 0)),
                  pl.BlockSpec((None, None, 1, L), lambda b, h: (2 * M_HEADS + h, b, 0, 0))],
        out_specs=[pl.BlockSpec((L, F_HEAD_DIM), lambda b, h: (b, h)),
                   pl.BlockSpec((None, None, 1, L), lambda b, h: (b, h, 0, 0))],
        out_shape=[jax.ShapeDtypeStruct((t, F_WIDTH), bf16),
                   jax.ShapeDtypeStruct((n_streams, F_HEADS, 1, L), f32)],
        compiler_params=_params("parallel", "parallel"),
        name="fox_decode",
    )(bias, z, k, v, cache_k, cache_v, cum_cache, gates4)


def _merge_kernel(x_ref, ha_ref, hb_ref, ga_ref, gb_ref, wa_ref, wb_ref, wo_ref, g_ref, x1_ref, hn_ref):
    pa = jnp.dot(ha_ref[...], wa_ref[...], preferred_element_type=f32)
    pb = jnp.dot(hb_ref[...], wb_ref[...], preferred_element_type=f32)
    merged = (jax.nn.sigmoid(ga_ref[...].astype(f32)) * pa + jax.nn.sigmoid(gb_ref[...].astype(f32)) * pb)
    x1 = x_ref[...] + jnp.dot(merged.astype(bf16), wo_ref[...], preferred_element_type=f32)
    x1_ref[...] = x1
    hn = x1 * lax.rsqrt(jnp.mean(x1 * x1, axis=-1, keepdims=True) + EPS) * g_ref[...]
    hn_ref[...] = hn.astype(hn_ref.dtype)


def _merge(x, h_a, h_b, z, w_a, w_b, w_o, g, *, gate_col0):
    t, d = x.shape
    tm = _tile(t, 256)
    ga_blk = gate_col0 // d
    row = lambda i: (i, 0)
    fixed = lambda i: (0, 0)
    resident = functools.partial(pl.BlockSpec, index_map=fixed, pipeline_mode=pl.Buffered(1))
    return pl.pallas_call(
        _merge_kernel,
        grid=(t // tm,),
        in_specs=[pl.BlockSpec((tm, d), row),
                  pl.BlockSpec((tm, M_WIDTH), row),
                  pl.BlockSpec((tm, F_WIDTH), row),
                  pl.BlockSpec((tm, d), lambda i: (i, ga_blk)),
                  pl.BlockSpec((tm, d), lambda i: (i, ga_blk + 1)),
                  resident(w_a.shape), resident(w_b.shape), resident(w_o.shape),
                  pl.BlockSpec((1, d), fixed)],
        out_specs=[pl.BlockSpec((tm, d), row), pl.BlockSpec((tm, d), row)],
        out_shape=[jax.ShapeDtypeStruct((t, d), f32), jax.ShapeDtypeStruct((t, d), bf16)],
        compiler_params=_params("parallel"),
        name="merge",
    )(x, h_a, h_b, z, z, w_a, w_b, w_o, g)


def _ffn_kernel(hn_ref, x1_ref, wu_ref, wd_ref, g_ref, y_ref):
    f = pl.program_id(1)

    @pl.when(f == 0)
    def _():
        y_ref[...] = x1_ref[...]

    u = jnp.maximum(jnp.dot(hn_ref[...], wu_ref[...], preferred_element_type=f32), 0.0)
    y_ref[...] += jnp.dot((u * u).astype(bf16), wd_ref[...], preferred_element_type=f32)

    @pl.when(f == pl.num_programs(1) - 1)
    def _():
        x2 = y_ref[...]
        y_ref[...] = x2 * lax.rsqrt(jnp.mean(x2 * x2, axis=-1, keepdims=True) + EPS) * g_ref[...]


def _ffn(hn, x1, w_up, w_down, g):
    t, d = x1.shape
    dff = w_up.shape[1]
    tm = _tile(t, 512)
    tf = _tile(dff, 1024)
    return pl.pallas_call(
        _ffn_kernel,
        grid=(t // tm, dff // tf),
        in_specs=[pl.BlockSpec((tm, d), lambda i, f: (i, 0)),
                  pl.BlockSpec((tm, d), lambda i, f: (i, 0)),
                  pl.BlockSpec((d, tf), lambda i, f: (0, f)),
                  pl.BlockSpec((tf, d), lambda i, f: (f, 0)),
                  pl.BlockSpec((1, d), lambda i, f: (0, 0))],
        out_specs=pl.BlockSpec((tm, d), lambda i, f: (i, 0)),
        out_shape=jax.ShapeDtypeStruct((t, d), f32),
        compiler_params=_params("parallel", "arbitrary"),
        name="ffn",
    )(hn, x1, w_up, w_down, g)


def _layer(x3, w, mstate, fox_past):
    nstr, frames, d = x3.shape
    t = nstr * frames
    x = x3.reshape(t, d)
    xn, gates_t = _norm_gates(x, w["norm_mix"], w["w_gates_t"])
    k32, k16 = _matmul(xn, w["w_fk"], (f32, bf16), "proj_k")
    v32, v16 = _matmul(xn, w["w_fv"], (f32, bf16), "proj_v")
    (z,) = _matmul(xn, w["w_rest"], (bf16,), "proj_rest")

    c0, n0, m0 = mstate
    h_a, c_new, n_slab, m_slab = _mlstm(z, gates_t, w["mlstm_bias"], w["norm_mlstm_h"], c0, n0, m0,
                                        n_streams=nstr)
    n_new = n_slab[..., 0]
    m_new = m_slab[:, :, 0, 0]

    if fox_past is None:
        logf_t, cum_t = _scan_rows(gates_t, w["b_fox_f_col"], row_block=1, rows=F_HEADS, n_streams=nstr,
                                   apply_log_sigmoid=True)
        h_b = _fox_prompt(z, k16, v16, cum_t, q_col0=w["q_col0"], n_streams=nstr)
        logf = logf_t.T.reshape(nstr, frames, F_HEADS)
    else:
        ck, cv, clf = fox_past
        p = ck.shape[1]
        clf_t = jnp.transpose(clf, (0, 2, 1)).reshape(nstr * F_HEADS, p)
        _, cum_c = _scan_rows(clf_t, jnp.zeros((nstr * F_HEADS, 1), f32), row_block=0, rows=nstr * F_HEADS,
                              n_streams=1, apply_log_sigmoid=False)
        h_b, logf4 = _fox_decode(z, k16, v16, gates_t, w["b_fox_f"], ck.reshape(nstr, p * F_HEADS, F_HEAD_DIM),
                                 cv.reshape(nstr, p * F_HEADS, F_HEAD_DIM), cum_c.reshape(nstr * F_HEADS, 1, p),
                                 q_col0=w["q_col0"], n_streams=nstr)
        logf = jnp.transpose(logf4[:, :, 0, :], (0, 2, 1))

    x1, hn = _merge(x, h_a, h_b, z, w["w_branch_a"], w["w_branch_b"], w["w_out"], w["norm_ffn"],
                    gate_col0=w["gate_col0"])
    y = _ffn(hn, x1, w["w_up"], w["w_down"], w["norm_final"])

    k_rows = k32.reshape(nstr, frames, F_HEADS, F_HEAD_DIM)
    v_rows = v32.reshape(nstr, frames, F_HEADS, F_HEAD_DIM)
    return y.reshape(nstr, frames, d), (k_rows, v_rows, logf, c_new, n_new, m_new)


def _prepare_weights(norm_mix, w_in, b_mlstm_i, b_mlstm_f, b_fox_f, norm_mlstm_h, w_branch_a, w_branch_b,
                     w_out, norm_ffn, w_up, w_down, norm_final):
    d = w_in.shape[0]
    sizes = (M_HEADS * M_DQK, M_HEADS * M_DQK, M_WIDTH, M_WIDTH, M_HEADS, M_HEADS,
             F_WIDTH, F_WIDTH, F_WIDTH, F_HEADS, d, d)
    offs = [0]
    for s in sizes:
        offs.append(offs[-1] + s)
    mq, mk, mv, mo, mi, mf, fq, fk, fv, ff, ga, gb = (w_in[:, offs[i]:offs[i + 1]] for i in range(12))
    w_rest = jnp.concatenate([mq, mk * (M_DQK ** -0.5), mv, mo, ga, gb, fq * (F_HEAD_DIM ** -0.5 * LOG2E)], axis=1)
    return {
        "norm_mix": norm_mix.reshape(1, d),
        "w_gates_t": jnp.concatenate([mi, mf, ff], axis=1).T.astype(bf16),
        "w_fk": fk.astype(bf16),
        "w_fv": fv.astype(bf16),
        "w_rest": w_rest.astype(bf16),
        "gate_col0": 4 * M_WIDTH,
        "q_col0": 4 * M_WIDTH + 2 * d,
        "mlstm_bias": jnp.concatenate([b_mlstm_i, b_mlstm_f]).astype(f32),
        "b_fox_f": b_fox_f.astype(f32),
        "b_fox_f_col": b_fox_f.astype(f32).reshape(F_HEADS, 1),
        "norm_mlstm_h": norm_mlstm_h.reshape(1, M_WIDTH),
        "w_branch_a": w_branch_a.astype(bf16),
        "w_branch_b": w_branch_b.astype(bf16),
        "w_out": w_out.astype(bf16),
        "norm_ffn": norm_ffn.reshape(1, d),
        "w_up": w_up.astype(bf16),
        "w_down": w_down.astype(bf16),
        "norm_final": norm_final.reshape(1, d),
    }


def kernel(x_prompt, x_sample, cache_fox_k, cache_fox_v, cache_fox_logf, state_mlstm_c, state_mlstm_n, state_mlstm_m, norm_mix, w_in, b_mlstm_i, b_mlstm_f, b_fox_f, norm_mlstm_h, w_branch_a, w_branch_b, w_out, norm_ffn, w_up, w_down, norm_final):
    depth = w_in.shape[0]
    assert depth == 1, "the final norm is fused into the layer's FFN kernel"
    w = _prepare_weights(norm_mix[0], w_in[0], b_mlstm_i[0], b_mlstm_f[0], b_fox_f[0], norm_mlstm_h[0],
                         w_branch_a[0], w_branch_b[0], w_out[0], norm_ffn[0], w_up[0], w_down[0], norm_final)
    bp = x_prompt.shape[0]
    fresh = (jnp.zeros((bp, M_HEADS, M_DQK, M_DV), f32), jnp.zeros((bp, M_HEADS, M_DQK), f32),
             jnp.zeros((bp, M_HEADS), f32))
    y_p, st_p = _layer(x_prompt, w, fresh, None)
    y_s, st_s = _layer(x_sample, w, (state_mlstm_c[0], state_mlstm_n[0], state_mlstm_m[0]),
                       (cache_fox_k[0], cache_fox_v[0], cache_fox_logf[0]))
    return (y_p, y_s) + tuple(a[None] for a in st_p) + tuple(a[None] for a in st_s)
```

```python
import functools

import jax
import jax.numpy as jnp
from jax import lax
from jax.experimental import pallas as pl
from jax.experimental.pallas import tpu as pltpu

M_HEADS = 4
M_DQK = 256
M_DV = 256
M_WIDTH = M_HEADS * M_DV
F_HEADS = 8
F_HEAD_DIM = 128
F_WIDTH = F_HEADS * F_HEAD_DIM
EPS = 1e-6
N_GATE_ROWS = 2 * M_HEADS + F_HEADS
LANES = 128
assert F_HEAD_DIM == LANES
MLSTM_BLOCK = 256
ATTN_BLOCK = 512
ATTN_HEADS = 1
LOG2E = 1.4426950408889634
VMEM_LIMIT_BYTES = 56 * 1024 * 1024

f32 = jnp.float32
bf16 = jnp.bfloat16

_NT = (((1,), (1,)), ((), ()))
_TN = (((0,), (0,)), ((), ()))


def _tile(n, pref):
    t = min(n, pref)
    while n % t:
        t //= 2
    return t


def _params(*sem):
    return pltpu.CompilerParams(dimension_semantics=sem, vmem_limit_bytes=VMEM_LIMIT_BYTES)


def _log_sigmoid(z):
    return jnp.minimum(z, 0.0) - jnp.log1p(jnp.exp(-jnp.abs(z)))


def _cumsum_lanes(x):
    rows, n = x.shape
    pad = -rows % 16
    if rows == 1:
        x16 = jnp.broadcast_to(x, (16, n))
    elif pad == 0:
        x16 = x
    else:
        x16 = jnp.concatenate([x, jnp.zeros((pad, n), f32)], axis=0)
    r = lax.broadcasted_iota(jnp.int32, (n, n), 0)
    c = lax.broadcasted_iota(jnp.int32, (n, n), 1)
    u = jnp.where(r <= c, 1.0, 0.0).astype(bf16)
    hi = x16.astype(bf16)
    rem = x16 - hi.astype(f32)
    mid = rem.astype(bf16)
    lo = (rem - mid.astype(f32)).astype(bf16)
    out = (jnp.dot(hi, u, preferred_element_type=f32) + jnp.dot(mid, u, preferred_element_type=f32)
           + jnp.dot(lo, u, preferred_element_type=f32))
    return out[:rows]


def _norm_kernel(x_ref, g_ref, wg_ref, xn_ref, gt_ref):
    x = x_ref[...]
    y = x * lax.rsqrt(jnp.mean(x * x, axis=-1, keepdims=True) + EPS) * g_ref[...]
    xn = y.astype(bf16)
    xn_ref[...] = xn
    gt_ref[...] = lax.dot_general(wg_ref[...], xn, _NT, preferred_element_type=f32)


def _norm_gates(x, g, wg_t):
    t, d = x.shape
    tm = _tile(t, 512)
    return pl.pallas_call(
        _norm_kernel,
        grid=(t // tm,),
        in_specs=[pl.BlockSpec((tm, d), lambda i: (i, 0)),
                  pl.BlockSpec((1, d), lambda i: (0, 0)),
                  pl.BlockSpec((N_GATE_ROWS, d), lambda i: (0, 0))],
        out_specs=[pl.BlockSpec((tm, d), lambda i: (i, 0)),
                   pl.BlockSpec((N_GATE_ROWS, tm), lambda i: (0, i))],
        out_shape=[jax.ShapeDtypeStruct((t, d), bf16), jax.ShapeDtypeStruct((N_GATE_ROWS, t), f32)],
        compiler_params=_params("parallel"),
        name="norm_gates",
    )(x, g, wg_t)


def _mm_kernel(a_ref, w_ref, *o_refs):
    r = jnp.dot(a_ref[...], w_ref[...], preferred_element_type=f32)
    for o in o_refs:
        o[...] = r.astype(o.dtype)


def _matmul(a, w, out_dtypes, name):
    t, k = a.shape
    n = w.shape[1]
    tm = _tile(t, 1024)
    tn = _tile(n, 1024)
    return pl.pallas_call(
        _mm_kernel,
        grid=(t // tm, n // tn),
        in_specs=[pl.BlockSpec((tm, k), lambda i, j: (i, 0)),
                  pl.BlockSpec((k, tn), lambda i, j: (0, j))],
        out_specs=[pl.BlockSpec((tm, tn), lambda i, j: (i, j)) for _ in out_dtypes],
        out_shape=[jax.ShapeDtypeStruct((t, n), dt) for dt in out_dtypes],
        compiler_params=_params("parallel", "arbitrary"),
        name=name,
    )(a, w)


def _scan_kernel(x_ref, bias_ref, logf_ref, cum_ref, carry_ref, *, apply_log_sigmoid):
    @pl.when(pl.program_id(1) == 0)
    def _():
        carry_ref[...] = jnp.zeros_like(carry_ref)

    x = x_ref[...]
    if apply_log_sigmoid:
        x = _log_sigmoid(x + bias_ref[...])
    logf_ref[...] = x
    cum = _cumsum_lanes(x) + carry_ref[:, :1]
    cum_ref[...] = cum * LOG2E
    carry_ref[...] = jnp.broadcast_to(cum[:, -1:], carry_ref.shape)


def _scan_rows(x, bias, *, row_block, rows, n_streams, apply_log_sigmoid):
    total = x.shape[1]
    s = total // n_streams
    tb = _tile(s, 512)
    nb = s // tb
    spec = pl.BlockSpec((rows, tb), lambda b, j: (0, b * nb + j))
    return pl.pallas_call(
        functools.partial(_scan_kernel, apply_log_sigmoid=apply_log_sigmoid),
        grid=(n_streams, nb),
        in_specs=[pl.BlockSpec((rows, tb), lambda b, j: (row_block, b * nb + j)),
                  pl.BlockSpec((rows, 1), lambda b, j: (0, 0))],
        out_specs=[spec, spec],
        out_shape=[jax.ShapeDtypeStruct((rows, total), f32)] * 2,
        scratch_shapes=[pltpu.VMEM((rows, LANES), f32)],
        compiler_params=_params("arbitrary", "arbitrary"),
        name="logf_scan",
    )(x, bias)


def _mlstm_kernel(bias_ref, q_ref, k_ref, v_ref, o_ref, gates_ref, nh_ref, c0_ref, n0_ref, m0_ref,
                  h_ref, c_out_ref, n_out_ref, m_out_ref, caug_ref, m_ref):
    blk = pl.program_id(1)
    n_blk = pl.num_programs(1)
    L = q_ref.shape[0]

    @pl.when(blk == 0)
    def _():
        lane = lax.broadcasted_iota(jnp.int32, (M_HEADS, M_DQK, LANES), 2)
        caug_ref[:, :, :M_DV] = c0_ref[...]
        caug_ref[:, :, M_DV:] = jnp.where(lane == 0, n0_ref[...], 0.0)
        m_ref[...] = m0_ref[...]

    gates = gates_ref[...]
    ig_all = gates[:M_HEADS] + bias_ref[:M_HEADS]
    lf_all = _log_sigmoid(gates[M_HEADS:2 * M_HEADS] + bias_ref[M_HEADS:])
    a_all = ig_all - _cumsum_lanes(lf_all)

    t_idx = lax.broadcasted_iota(jnp.int32, (L, L), 0)
    s_idx = lax.broadcasted_iota(jnp.int32, (L, L), 1)
    causal = s_idx <= t_idx
    diag = s_idx == t_idx
    ones_col = jnp.where(lax.broadcasted_iota(jnp.int32, (L, LANES), 1) == 0, 1.0, 0.0).astype(bf16)

    for h in range(M_HEADS):
        cols = slice(h * M_DV, (h + 1) * M_DV)
        m0 = m_ref[h:h + 1, :1]
        ig = ig_all[h:h + 1]
        lf = lf_all[h:h + 1]
        a_row = a_all[h:h + 1]
        a_mat = jnp.where(causal, a_row, -jnp.inf)
        g_col = jnp.maximum(m0, jnp.max(a_mat, axis=1, keepdims=True))
        b_col = jnp.sum(jnp.where(causal, lf, 0.0), axis=1, keepdims=True)
        w_intra = jnp.exp(a_mat - g_col)
        w_inter = jnp.exp(m0 - g_col)

        q = q_ref[:, cols]
        k = k_ref[:, cols]
        v_aug = jnp.concatenate([v_ref[:, cols], ones_col], axis=1)
        c_aug = caug_ref[h]

        s = lax.dot_general(q, k, _NT, preferred_element_type=f32)
        sw = (s * w_intra).astype(bf16)
        num_aug = (jnp.dot(sw, v_aug, preferred_element_type=f32)
                   + w_inter * jnp.dot(q, c_aug.astype(bf16), preferred_element_type=f32))
        num = num_aug[:, :M_DV]
        den = num_aug[:, M_DV:M_DV + 1]
        den = jnp.maximum(jnp.abs(den), jnp.exp(-(b_col + g_col)))
        hh = num / den
        hh = hh * lax.rsqrt(jnp.mean(hh * hh, axis=-1, keepdims=True) + EPS)
        hh = hh * nh_ref[:, cols] * jax.nn.sigmoid(o_ref[:, cols].astype(f32))
        h_ref[:, cols] = hh.astype(h_ref.dtype)

        g_end = jnp.maximum(m0, jnp.max(a_row, axis=1, keepdims=True))
        b_end = jnp.sum(lf, axis=1, keepdims=True)
        ig_col = jnp.sum(jnp.where(diag, ig, 0.0), axis=1, keepdims=True)
        w_tok = jnp.exp(ig_col - b_col - g_end)
        w_state = jnp.exp(m0 - g_end)
        kw = (k.astype(f32) * w_tok).astype(bf16)
        caug_ref[h] = w_state * c_aug + lax.dot_general(kw, v_aug, _TN, preferred_element_type=f32)
        m_ref[h:h + 1, :] = jnp.broadcast_to(b_end + g_end, (1, LANES))

    @pl.when(blk == n_blk - 1)
    def _():
        c_out_ref[...] = caug_ref[:, :, :M_DV]
        n_out_ref[...] = caug_ref[:, :, M_DV:]
        m_out_ref[...] = m_ref[...]


def _mlstm(z, gates_t, gate_bias, norm_h, c0, n0, m0, *, n_streams):
    t = z.shape[0]
    s = t // n_streams
    L = _tile(s, MLSTM_BLOCK)
    nb = s // L
    gates3 = gates_t.reshape(N_GATE_ROWS, n_streams * nb, L).transpose(1, 0, 2)
    n0c = n0.reshape(n_streams, M_HEADS, M_DQK, 1)
    m0b = jnp.broadcast_to(m0.reshape(n_streams, M_HEADS, 1), (n_streams, M_HEADS, LANES))

    def zcols(group):
        return pl.BlockSpec((L, M_WIDTH), lambda b, c: (b * nb + c, group))

    def state(*tail):
        return pl.BlockSpec((None, M_HEADS) + tail, lambda b, c: (b, 0) + (0,) * len(tail))

    return pl.pallas_call(
        _mlstm_kernel,
        grid=(n_streams, nb),
        in_specs=[pl.BlockSpec((2 * M_HEADS, 1), lambda b, c: (0, 0)),
                  zcols(0), zcols(1), zcols(2), zcols(3),
                  pl.BlockSpec((None, N_GATE_ROWS, L), lambda b, c: (b * nb + c, 0, 0)),
                  pl.BlockSpec((1, M_WIDTH), lambda b, c: (0, 0)),
                  state(M_DQK, M_DV), state(M_DQK, 1), state(LANES)],
        out_specs=[pl.BlockSpec((L, M_WIDTH), lambda b, c: (b * nb + c, 0)),
                   state(M_DQK, M_DV), state(M_DQK, LANES), state(LANES)],
        out_shape=[jax.ShapeDtypeStruct((t, M_WIDTH), bf16),
                   jax.ShapeDtypeStruct((n_streams, M_HEADS, M_DQK, M_DV), f32),
                   jax.ShapeDtypeStruct((n_streams, M_HEADS, M_DQK, LANES), f32),
                   jax.ShapeDtypeStruct((n_streams, M_HEADS, LANES), f32)],
        scratch_shapes=[pltpu.VMEM((M_HEADS, M_DQK, M_DV + LANES), f32), pltpu.VMEM((M_HEADS, LANES), f32)],
        compiler_params=_params("parallel", "arbitrary"),
        name="mlstm",
    )(gate_bias, z, z, z, z, gates3, norm_h, c0, n0c, m0b)


def _fox_prompt_kernel(q_ref, k_ref, v_ref, cum_ref, o_ref,
                       sa_ref, sb_ref, pa_ref, pb_ref, xa_ref, xb_ref, m_ref, l_ref, acc_ref):
    i = pl.program_id(2)
    tq = o_ref.shape[0]
    heads = range(ATTN_HEADS)

    def head_cols(g):
        return slice(g * F_HEAD_DIM, (g + 1) * F_HEAD_DIM)

    def rows(ref, j, g):
        return ref[pl.ds(pl.multiple_of(j * tq, tq), tq), head_cols(g)]

    def causal(s):
        row = lax.broadcasted_iota(jnp.int32, (tq, tq), 0)
        col = lax.broadcasted_iota(jnp.int32, (tq, tq), 1)
        return jnp.where(col <= row, s, -jnp.inf)

    def scores(j, s_ref, x_ref, masked, q_row=i):
        for g in heads:
            s = lax.dot_general(rows(q_ref, q_row, g), rows(k_ref, j, g), _NT,
                                preferred_element_type=f32) - cum_ref[g, j]
            if masked:
                s = causal(s)
            s_ref[g] = s
            x_ref[g] = jnp.broadcast_to(jnp.max(s, axis=1, keepdims=True), (tq, LANES))

    def values(j, p_ref):
        return [jnp.dot(p_ref[g], rows(v_ref, j, g), preferred_element_type=f32) for g in heads]

    def softmax(s_ref, x_ref, p_ref, pv_prev, mask_now=False):
        for g in heads:
            s = s_ref[g]
            if mask_now:
                s = causal(s)
                x = jnp.max(s, axis=1, keepdims=True)
            else:
                x = x_ref[g]
            m_old = m_ref[g]
            m_new = jnp.maximum(m_old, x)
            alpha = jnp.exp2(m_old - m_new)
            p = jnp.exp2(s - jnp.concatenate([m_new] * (tq // LANES), axis=1))
            p_ref[g] = p.astype(bf16)
            l_ref[g] = alpha * l_ref[g] + jnp.sum(p, axis=1, keepdims=True)
            acc_ref[g] = alpha * (acc_ref[g] + pv_prev[g])
            m_ref[g] = m_new

    def add_values(j, p_ref):
        for g, pv in enumerate(values(j, p_ref)):
            acc_ref[g] += pv

    m_ref[...] = jnp.full(m_ref.shape, -jnp.inf, f32)
    l_ref[...] = jnp.zeros(l_ref.shape, f32)
    acc_ref[...] = jnp.zeros(acc_ref.shape, f32)
    pb_ref[...] = jnp.zeros(pb_ref.shape, bf16)

    @pl.when(i == 0)
    def _():
        scores(0, sa_ref, xa_ref, False)

    def scores_of_next_row():
        scores(0, sa_ref, xa_ref, False, q_row=jnp.minimum(i + 1, pl.num_programs(2) - 1))

    def pair(j0):
        scores(j0 + 1, sb_ref, xb_ref, False)
        softmax(sa_ref, xa_ref, pa_ref, values(jnp.maximum(j0 - 1, 0), pb_ref))
        scores(j0 + 2, sa_ref, xa_ref, False)
        softmax(sb_ref, xb_ref, pb_ref, values(j0, pa_ref))

    def quad(jj, carry):
        pair(4 * jj)
        pair(4 * jj + 2)
        return carry

    def last_pair(jj, carry):
        pair(4 * (i // 4))
        return carry

    lax.fori_loop(0, i // 4, quad, 0)
    lax.fori_loop(0, (i % 4) // 2, last_pair, 0)

    @pl.when(i % 2 == 0)
    def _():
        softmax(sa_ref, xa_ref, pa_ref, values(jnp.maximum(i - 1, 0), pb_ref), mask_now=True)
        scores_of_next_row()
        add_values(i, pa_ref)

    @pl.when(i % 2 == 1)
    def _():
        scores(i, sb_ref, xb_ref, True)
        softmax(sa_ref, xa_ref, pa_ref, values(jnp.maximum(i - 2, 0), pb_ref))
        scores_of_next_row()
        softmax(sb_ref, xb_ref, pb_ref, values(i - 1, pa_ref))
        add_values(i, pb_ref)

    for g in heads:
        o_ref[:, head_cols(g)] = (acc_ref[g] / l_ref[g]).astype(o_ref.dtype)


def _fox_prompt(z, k, v, cum_t, *, q_col0, n_streams):
    t = z.shape[0]
    s = t // n_streams
    tq = _tile(s, ATTN_BLOCK)
    nq = s // tq
    cum4 = cum_t.reshape(F_HEADS, n_streams * nq, 1, tq)
    width = ATTN_HEADS * F_HEAD_DIM
    q0 = q_col0 // width
    return pl.pallas_call(
        _fox_prompt_kernel,
        grid=(n_streams, F_HEADS // ATTN_HEADS, nq),
        in_specs=[pl.BlockSpec((s, width), lambda b, h, i: (b, q0 + h)),
                  pl.BlockSpec((s, width), lambda b, h, i: (b, h)),
                  pl.BlockSpec((s, width), lambda b, h, i: (b, h)),
                  pl.BlockSpec((ATTN_HEADS, nq, 1, tq), lambda b, h, i: (h, b, 0, 0))],
        out_specs=pl.BlockSpec((tq, width), lambda b, h, i: (b * nq + i, h)),
        out_shape=jax.ShapeDtypeStruct((t, F_WIDTH), bf16),
        scratch_shapes=[pltpu.VMEM((ATTN_HEADS, tq, tq), f32)] * 2 + [pltpu.VMEM((ATTN_HEADS, tq, tq), bf16)] * 2
        + [pltpu.VMEM((ATTN_HEADS, tq, LANES), f32)] * 5,
        compiler_params=_params("parallel", "parallel", "arbitrary"),
        name="fox_prompt",
    )(z, k, v, cum4)


def _fox_decode_kernel(bias_ref, q_ref, kn_ref, vn_ref, kc_ref, vc_ref, cumc_ref, fg_ref, o_ref, logf_ref):
    head = pl.program_id(1)
    L = q_ref.shape[0]
    q = q_ref[...]
    logf = _log_sigmoid(fg_ref[...] + bias_ref[head])
    logf_ref[...] = logf
    cum_new = _cumsum_lanes(logf) * LOG2E
    cum_c = cumc_ref[...]
    cum_c = cum_c - cum_c[:, -1:]

    p_len = kc_ref.shape[0] // F_HEADS
    kc = kc_ref[pl.ds(head, p_len, stride=F_HEADS), :].astype(bf16)
    vc = vc_ref[pl.ds(head, p_len, stride=F_HEADS), :].astype(bf16)
    s_c = lax.dot_general(q, kc, _NT, preferred_element_type=f32) - cum_c
    s_n = lax.dot_general(q, kn_ref[...], _NT, preferred_element_type=f32) - cum_new
    row = lax.broadcasted_iota(jnp.int32, (L, L), 0)
    col = lax.broadcasted_iota(jnp.int32, (L, L), 1)
    s_n = jnp.where(col <= row, s_n, -jnp.inf)
    m = jnp.maximum(jnp.max(s_c, axis=1, keepdims=True), jnp.max(s_n, axis=1, keepdims=True))
    p_c = jnp.exp2(s_c - m)
    p_n = jnp.exp2(s_n - m)
    l = jnp.sum(p_c, axis=1, keepdims=True) + jnp.sum(p_n, axis=1, keepdims=True)
    acc = (jnp.dot(p_c.astype(bf16), vc, preferred_element_type=f32)
           + jnp.dot(p_n.astype(bf16), vn_ref[...], preferred_element_type=f32))
    o_ref[...] = (acc / l).astype(o_ref.dtype)


def _fox_decode(z, k, v, gates_t, bias, cache_k, cache_v, cum_cache, *, q_col0, n_streams):
    t = z.shape[0]
    L = t // n_streams
    p = cache_k.shape[1] // F_HEADS
    gates4 = gates_t.reshape(N_GATE_ROWS, n_streams, 1, L)
    q0 = q_col0 // F_HEAD_DIM
    return pl.pallas_call(
        _fox_decode_kernel,
        grid=(n_streams, F_HEADS),
        in_specs=[pl.BlockSpec(memory_space=pltpu.SMEM),
                  pl.BlockSpec((L, F_HEAD_DIM), lambda b, h: (b, q0 + h)),
                  pl.BlockSpec((L, F_HEAD_DIM), lambda b, h: (b, h)),
                  pl.BlockSpec((L, F_HEAD_DIM), lambda b, h: (b, h)),
                  pl.BlockSpec((None, p * F_HEADS, F_HEAD_DIM), lambda b, h: (b, 0, 0)),
                  pl.BlockSpec((None, p * F_HEADS, F_HEAD_DIM), lambda b, h: (b, 0, 0)),
                  pl.BlockSpec((None, 1, p), lambda b, h: (b * F_HEADS + h, 0, 0)),
                  pl.BlockSpec((None, None, 1, L), lambda b, h: (2 * M_HEADS + h, b, 0, 0))],
        out_specs=[pl.BlockSpec((L, F_HEAD_DIM), lambda b, h: (b, h)),
                   pl.BlockSpec((None, None, 1, L), lambda b, h: (b, h, 0, 0))],
        out_shape=[jax.ShapeDtypeStruct((t, F_WIDTH), bf16),
                   jax.ShapeDtypeStruct((n_streams, F_HEADS, 1, L), f32)],
        compiler_params=_params("parallel", "parallel"),
        name="fox_decode",
    )(bias, z, k, v, cache_k, cache_v, cum_cache, gates4)


def _merge_kernel(x_ref, ha_ref, hb_ref, ga_ref, gb_ref, wa_ref, wb_ref, wo_ref, g_ref, x1_ref, hn_ref):
    pa = jnp.dot(ha_ref[...], wa_ref[...], preferred_element_type=f32)
    pb = jnp.dot(hb_ref[...], wb_ref[...], preferred_element_type=f32)
    merged = (jax.nn.sigmoid(ga_ref[...].astype(f32)) * pa + jax.nn.sigmoid(gb_ref[...].astype(f32)) * pb)
    x1 = x_ref[...] + jnp.dot(merged.astype(bf16), wo_ref[...], preferred_element_type=f32)
    x1_ref[...] = x1
    hn = x1 * lax.rsqrt(jnp.mean(x1 * x1, axis=-1, keepdims=True) + EPS) * g_ref[...]
    hn_ref[...] = hn.astype(hn_ref.dtype)


def _merge(x, h_a, h_b, z, w_a, w_b, w_o, g, *, gate_col0):
    t, d = x.shape
    tm = _tile(t, 256)
    ga_blk = gate_col0 // d
    row = lambda i: (i, 0)
    fixed = lambda i: (0, 0)
    resident = functools.partial(pl.BlockSpec, index_map=fixed, pipeline_mode=pl.Buffered(1))
    return pl.pallas_call(
        _merge_kernel,
        grid=(t // tm,),
        in_specs=[pl.BlockSpec((tm, d), row),
                  pl.BlockSpec((tm, M_WIDTH), row),
                  pl.BlockSpec((tm, F_WIDTH), row),
                  pl.BlockSpec((tm, d), lambda i: (i, ga_blk)),
                  pl.BlockSpec((tm, d), lambda i: (i, ga_blk + 1)),
                  resident(w_a.shape), resident(w_b.shape), resident(w_o.shape),
                  pl.BlockSpec((1, d), fixed)],
        out_specs=[pl.BlockSpec((tm, d), row), pl.BlockSpec((tm, d), row)],
        out_shape=[jax.ShapeDtypeStruct((t, d), f32), jax.ShapeDtypeStruct((t, d), bf16)],
        compiler_params=_params("parallel"),
        name="merge",
    )(x, h_a, h_b, z, z, w_a, w_b, w_o, g)


def _ffn_kernel(hn_ref, x1_ref, wu_ref, wd_ref, g_ref, y_ref):
    f = pl.program_id(1)

    @pl.when(f == 0)
    def _():
        y_ref[...] = x1_ref[...]

    u = jnp.maximum(jnp.dot(hn_ref[...], wu_ref[...], preferred_element_type=f32), 0.0)
    y_ref[...] += jnp.dot((u * u).astype(bf16), wd_ref[...], preferred_element_type=f32)

    @pl.when(f == pl.num_programs(1) - 1)
    def _():
        x2 = y_ref[...]
        y_ref[...] = x2 * lax.rsqrt(jnp.mean(x2 * x2, axis=-1, keepdims=True) + EPS) * g_ref[...]


def _ffn(hn, x1, w_up, w_down, g):
    t, d = x1.shape
    dff = w_up.shape[1]
    tm = _tile(t, 512)
    tf = _tile(dff, 1024)
    return pl.pallas_call(
        _ffn_kernel,
        grid=(t // tm, dff // tf),
        in_specs=[pl.BlockSpec((tm, d), lambda i, f: (i, 0)),
                  pl.BlockSpec((tm, d), lambda i, f: (i, 0)),
                  pl.BlockSpec((d, tf), lambda i, f: (0, f)),
                  pl.BlockSpec((tf, d), lambda i, f: (f, 0)),
                  pl.BlockSpec((1, d), lambda i, f: (0, 0))],
        out_specs=pl.BlockSpec((tm, d), lambda i, f: (i, 0)),
        out_shape=jax.ShapeDtypeStruct((t, d), f32),
        compiler_params=_params("parallel", "arbitrary"),
        name="ffn",
    )(hn, x1, w_up, w_down, g)


def _layer(x3, w, mstate, fox_past):
    nstr, frames, d = x3.shape
    t = nstr * frames
    x = x3.reshape(t, d)
    xn, gates_t = _norm_gates(x, w["norm_mix"], w["w_gates_t"])
    k32, k16 = _matmul(xn, w["w_fk"], (f32, bf16), "proj_k")
    v32, v16 = _matmul(xn, w["w_fv"], (f32, bf16), "proj_v")
    (z,) = _matmul(xn, w["w_rest"], (bf16,), "proj_rest")

    c0, n0, m0 = mstate
    h_a, c_new, n_slab, m_slab = _mlstm(z, gates_t, w["mlstm_bias"], w["norm_mlstm_h"], c0, n0, m0,
                                        n_streams=nstr)
    n_new = n_slab[..., 0]
    m_new = m_slab[:, :, 0]

    if fox_past is None:
        logf_t, cum_t = _scan_rows(gates_t, w["b_fox_f_col"], row_block=1, rows=F_HEADS, n_streams=nstr,
                                   apply_log_sigmoid=True)
        h_b = _fox_prompt(z, k16, v16, cum_t, q_col0=w["q_col0"], n_streams=nstr)
        logf = logf_t.T.reshape(nstr, frames, F_HEADS)
    else:
        ck, cv, clf = fox_past
        p = ck.shape[1]
        clf_t = jnp.transpose(clf, (0, 2, 1)).reshape(nstr * F_HEADS, p)
        _, cum_c = _scan_rows(clf_t, jnp.zeros((nstr * F_HEADS, 1), f32), row_block=0, rows=nstr * F_HEADS,
                              n_streams=1, apply_log_sigmoid=False)
        h_b, logf4 = _fox_decode(z, k16, v16, gates_t, w["b_fox_f"], ck.reshape(nstr, p * F_HEADS, F_HEAD_DIM),
                                 cv.reshape(nstr, p * F_HEADS, F_HEAD_DIM), cum_c.reshape(nstr * F_HEADS, 1, p),
                                 q_col0=w["q_col0"], n_streams=nstr)
        logf = jnp.transpose(logf4[:, :, 0, :], (0, 2, 1))

    x1, hn = _merge(x, h_a, h_b, z, w["w_branch_a"], w["w_branch_b"], w["w_out"], w["norm_ffn"],
                    gate_col0=w["gate_col0"])
    y = _ffn(hn, x1, w["w_up"], w["w_down"], w["norm_final"])

    k_rows = k32.reshape(nstr, frames, F_HEADS, F_HEAD_DIM)
    v_rows = v32.reshape(nstr, frames, F_HEADS, F_HEAD_DIM)
    return y.reshape(nstr, frames, d), (k_rows, v_rows, logf, c_new, n_new, m_new)


def _prepare_weights(norm_mix, w_in, b_mlstm_i, b_mlstm_f, b_fox_f, norm_mlstm_h, w_branch_a, w_branch_b,
                     w_out, norm_ffn, w_up, w_down, norm_final):
    d = w_in.shape[0]
    sizes = (M_HEADS * M_DQK, M_HEADS * M_DQK, M_WIDTH, M_WIDTH, M_HEADS, M_HEADS,
             F_WIDTH, F_WIDTH, F_WIDTH, F_HEADS, d, d)
    offs = [0]
    for s in sizes:
        offs.append(offs[-1] + s)
    mq, mk, mv, mo, mi, mf, fq, fk, fv, ff, ga, gb = (w_in[:, offs[i]:offs[i + 1]] for i in range(12))
    w_rest = jnp.concatenate([mq, mk * (M_DQK ** -0.5), mv, mo, ga, gb, fq * (F_HEAD_DIM ** -0.5 * LOG2E)], axis=1)
    return {
        "norm_mix": norm_mix.reshape(1, d),
        "w_gates_t": jnp.concatenate([mi, mf, ff], axis=1).T.astype(bf16),
        "w_fk": fk.astype(bf16),
        "w_fv": fv.astype(bf16),
        "w_rest": w_rest.astype(bf16),
        "gate_col0": 4 * M_WIDTH,
        "q_col0": 4 * M_WIDTH + 2 * d,
        "mlstm_bias": jnp.concatenate([b_mlstm_i, b_mlstm_f]).astype(f32).reshape(2 * M_HEADS, 1),
        "b_fox_f": b_fox_f.astype(f32),
        "b_fox_f_col": b_fox_f.astype(f32).reshape(F_HEADS, 1),
        "norm_mlstm_h": norm_mlstm_h.reshape(1, M_WIDTH),
        "w_branch_a": w_branch_a.astype(bf16),
        "w_branch_b": w_branch_b.astype(bf16),
        "w_out": w_out.astype(bf16),
        "norm_ffn": norm_ffn.reshape(1, d),
        "w_up": w_up.astype(bf16),
        "w_down": w_down.astype(bf16),
        "norm_final": norm_final.reshape(1, d),
    }


def kernel(x_prompt, x_sample, cache_fox_k, cache_fox_v, cache_fox_logf, state_mlstm_c, state_mlstm_n, state_mlstm_m, norm_mix, w_in, b_mlstm_i, b_mlstm_f, b_fox_f, norm_mlstm_h, w_branch_a, w_branch_b, w_out, norm_ffn, w_up, w_down, norm_final):
    depth = w_in.shape[0]
    assert depth == 1, "the final norm is fused into the layer's FFN kernel"
    w = _prepare_weights(norm_mix[0], w_in[0], b_mlstm_i[0], b_mlstm_f[0], b_fox_f[0], norm_mlstm_h[0],
                         w_branch_a[0], w_branch_b[0], w_out[0], norm_ffn[0], w_up[0], w_down[0], norm_final)
    bp = x_prompt.shape[0]
    fresh = (jnp.zeros((bp, M_HEADS, M_DQK, M_DV), f32), jnp.zeros((bp, M_HEADS, M_DQK), f32),
             jnp.zeros((bp, M_HEADS), f32))
    y_p, st_p = _layer(x_prompt, w, fresh, None)
    y_s, st_s = _layer(x_sample, w, (state_mlstm_c[0], state_mlstm_n[0], state_mlstm_m[0]),
                       (cache_fox_k[0], cache_fox_v[0], cache_fox_logf[0]))
    return (y_p, y_s) + tuple(a[None] for a in st_p) + tuple(a[None] for a in st_s)
```

```python
import functools

import jax
import jax.numpy as jnp
from jax import lax
from jax.experimental import pallas as pl
from jax.experimental.pallas import tpu as pltpu

M_HEADS = 4
M_DQK = 256
M_DV = 256
M_WIDTH = M_HEADS * M_DV
F_HEADS = 8
F_HEAD_DIM = 128
F_WIDTH = F_HEADS * F_HEAD_DIM
EPS = 1e-6
N_GATE_ROWS = 2 * M_HEADS + F_HEADS
LANES = 128
assert F_HEAD_DIM == LANES
MLSTM_BLOCK = 256
ATTN_BLOCK = 512
ATTN_TRIP = 4
ATTN_HEADS = 1
LOG2E = 1.4426950408889634
VMEM_LIMIT_BYTES = 56 * 1024 * 1024

f32 = jnp.float32
bf16 = jnp.bfloat16

_NT = (((1,), (1,)), ((), ()))
_TN = (((0,), (0,)), ((), ()))


def _tile(n, pref):
    t = min(n, pref)
    while n % t:
        t //= 2
    return t


def _params(*sem):
    return pltpu.CompilerParams(dimension_semantics=sem, vmem_limit_bytes=VMEM_LIMIT_BYTES)


def _log_sigmoid(z):
    return jnp.minimum(z, 0.0) - jnp.log1p(jnp.exp(-jnp.abs(z)))


def _cumsum_lanes(x):
    rows, n = x.shape
    pad = -rows % 16
    if rows == 1:
        x16 = jnp.broadcast_to(x, (16, n))
    elif pad == 0:
        x16 = x
    else:
        x16 = jnp.concatenate([x, jnp.zeros((pad, n), f32)], axis=0)
    r = lax.broadcasted_iota(jnp.int32, (n, n), 0)
    c = lax.broadcasted_iota(jnp.int32, (n, n), 1)
    u = jnp.where(r <= c, 1.0, 0.0).astype(bf16)
    hi = x16.astype(bf16)
    rem = x16 - hi.astype(f32)
    mid = rem.astype(bf16)
    lo = (rem - mid.astype(f32)).astype(bf16)
    out = (jnp.dot(hi, u, preferred_element_type=f32) + jnp.dot(mid, u, preferred_element_type=f32)
           + jnp.dot(lo, u, preferred_element_type=f32))
    return out[:rows]


def _norm_kernel(x_ref, g_ref, wg_ref, xn_ref, gt_ref):
    x = x_ref[...]
    y = x * lax.rsqrt(jnp.mean(x * x, axis=-1, keepdims=True) + EPS) * g_ref[...]
    xn = y.astype(bf16)
    xn_ref[...] = xn
    gt_ref[...] = lax.dot_general(wg_ref[...], xn, _NT, preferred_element_type=f32)


def _norm_gates(x, g, wg_t):
    t, d = x.shape
    tm = _tile(t, 512)
    return pl.pallas_call(
        _norm_kernel,
        grid=(t // tm,),
        in_specs=[pl.BlockSpec((tm, d), lambda i: (i, 0)),
                  pl.BlockSpec((1, d), lambda i: (0, 0)),
                  pl.BlockSpec((N_GATE_ROWS, d), lambda i: (0, 0))],
        out_specs=[pl.BlockSpec((tm, d), lambda i: (i, 0)),
                   pl.BlockSpec((N_GATE_ROWS, tm), lambda i: (0, i))],
        out_shape=[jax.ShapeDtypeStruct((t, d), bf16), jax.ShapeDtypeStruct((N_GATE_ROWS, t), f32)],
        compiler_params=_params("parallel"),
        name="norm_gates",
    )(x, g, wg_t)


def _proj_kernel(a_ref, w_ref, k32_ref, k16_ref, v32_ref, v16_ref, z_ref):
    j = pl.program_id(1)

    def product():
        return jnp.dot(a_ref[...], w_ref[...], preferred_element_type=f32)

    @pl.when(j == 0)
    def _():
        r = product()
        k32_ref[...] = r
        k16_ref[...] = r.astype(bf16)

    @pl.when(j == 1)
    def _():
        r = product()
        v32_ref[...] = r
        v16_ref[...] = r.astype(bf16)

    @pl.when(j >= 2)
    def _():
        z_ref[...] = product().astype(bf16)


def _project(a, w):
    t, k = a.shape
    n = w.shape[1] - 2 * F_WIDTH
    tm = _tile(t, 1024)
    tn = F_WIDTH
    assert n % tn == 0
    head = lambda i, j: (i, 0)
    rest = lambda i, j: (i, jnp.maximum(j - 2, 0))
    return pl.pallas_call(
        _proj_kernel,
        grid=(t // tm, 2 + n // tn),
        in_specs=[pl.BlockSpec((tm, k), lambda i, j: (i, 0)),
                  pl.BlockSpec((k, tn), lambda i, j: (0, j))],
        out_specs=[pl.BlockSpec((tm, tn), head)] * 4 + [pl.BlockSpec((tm, tn), rest)],
        out_shape=[jax.ShapeDtypeStruct((t, F_WIDTH), f32), jax.ShapeDtypeStruct((t, F_WIDTH), bf16),
                   jax.ShapeDtypeStruct((t, F_WIDTH), f32), jax.ShapeDtypeStruct((t, F_WIDTH), bf16),
                   jax.ShapeDtypeStruct((t, n), bf16)],
        compiler_params=_params("parallel", "arbitrary"),
        name="proj",
    )(a, w)


def _scan_kernel(x_ref, bias_ref, logf_ref, cum_ref, carry_ref, *, apply_log_sigmoid):
    @pl.when(pl.program_id(1) == 0)
    def _():
        carry_ref[...] = jnp.zeros_like(carry_ref)

    x = x_ref[...]
    if apply_log_sigmoid:
        x = _log_sigmoid(x + bias_ref[...])
    logf_ref[...] = x
    cum = _cumsum_lanes(x) + carry_ref[:, :1]
    cum_ref[...] = cum * LOG2E
    carry_ref[...] = jnp.broadcast_to(cum[:, -1:], carry_ref.shape)


def _scan_rows(x, bias, *, row_block, rows, n_streams, apply_log_sigmoid):
    total = x.shape[1]
    s = total // n_streams
    tb = _tile(s, 512)
    nb = s // tb
    spec = pl.BlockSpec((rows, tb), lambda b, j: (0, b * nb + j))
    return pl.pallas_call(
        functools.partial(_scan_kernel, apply_log_sigmoid=apply_log_sigmoid),
        grid=(n_streams, nb),
        in_specs=[pl.BlockSpec((rows, tb), lambda b, j: (row_block, b * nb + j)),
                  pl.BlockSpec((rows, 1), lambda b, j: (0, 0))],
        out_specs=[spec, spec],
        out_shape=[jax.ShapeDtypeStruct((rows, total), f32)] * 2,
        scratch_shapes=[pltpu.VMEM((rows, LANES), f32)],
        compiler_params=_params("arbitrary", "arbitrary"),
        name="logf_scan",
    )(x, bias)


def _mlstm_kernel(bias_ref, q_ref, k_ref, v_ref, o_ref, gates_ref, nh_ref, c0_ref, n0_ref, m0_ref,
                  h_ref, c_out_ref, n_out_ref, m_out_ref, caug_ref, m_ref):
    blk = pl.program_id(1)
    n_blk = pl.num_programs(1)
    L = q_ref.shape[0]

    @pl.when(blk == 0)
    def _():
        lane = lax.broadcasted_iota(jnp.int32, (M_HEADS, M_DQK, LANES), 2)
        caug_ref[:, :, :M_DV] = c0_ref[...]
        caug_ref[:, :, M_DV:] = jnp.where(lane == 0, n0_ref[...], 0.0)
        m_ref[...] = m0_ref[...]

    gates = gates_ref[...]
    ig_all = gates[:M_HEADS] + bias_ref[:M_HEADS]
    lf_all = _log_sigmoid(gates[M_HEADS:2 * M_HEADS] + bias_ref[M_HEADS:])
    a_all = ig_all - _cumsum_lanes(lf_all)

    t_idx = lax.broadcasted_iota(jnp.int32, (L, L), 0)
    s_idx = lax.broadcasted_iota(jnp.int32, (L, L), 1)
    causal = s_idx <= t_idx
    diag = s_idx == t_idx
    ones_col = jnp.where(lax.broadcasted_iota(jnp.int32, (L, LANES), 1) == 0, 1.0, 0.0).astype(bf16)

    for h in range(M_HEADS):
        cols = slice(h * M_DV, (h + 1) * M_DV)
        m0 = m_ref[h:h + 1, :1]
        ig = ig_all[h:h + 1]
        lf = lf_all[h:h + 1]
        a_row = a_all[h:h + 1]
        a_mat = jnp.where(causal, a_row, -jnp.inf)
        g_col = jnp.maximum(m0, jnp.max(a_mat, axis=1, keepdims=True))
        b_col = jnp.sum(jnp.where(causal, lf, 0.0), axis=1, keepdims=True)
        w_intra = jnp.exp(a_mat - g_col)
        w_inter = jnp.exp(m0 - g_col)

        q = q_ref[:, cols]
        k = k_ref[:, cols]
        v_aug = jnp.concatenate([v_ref[:, cols], ones_col], axis=1)
        c_aug = caug_ref[h]

        s = lax.dot_general(q, k, _NT, preferred_element_type=f32)
        sw = (s * w_intra).astype(bf16)
        num_aug = (jnp.dot(sw, v_aug, preferred_element_type=f32)
                   + w_inter * jnp.dot(q, c_aug.astype(bf16), preferred_element_type=f32))
        num = num_aug[:, :M_DV]
        den = num_aug[:, M_DV:M_DV + 1]
        den = jnp.maximum(jnp.abs(den), jnp.exp(-(b_col + g_col)))
        hh = num / den
        hh = hh * lax.rsqrt(jnp.mean(hh * hh, axis=-1, keepdims=True) + EPS)
        hh = hh * nh_ref[:, cols] * jax.nn.sigmoid(o_ref[:, cols].astype(f32))
        h_ref[:, cols] = hh.astype(h_ref.dtype)

        g_end = jnp.maximum(m0, jnp.max(a_row, axis=1, keepdims=True))
        b_end = jnp.sum(lf, axis=1, keepdims=True)
        ig_col = jnp.sum(jnp.where(diag, ig, 0.0), axis=1, keepdims=True)
        w_tok = jnp.exp(ig_col - b_col - g_end)
        w_state = jnp.exp(m0 - g_end)
        kw = (k.astype(f32) * w_tok).astype(bf16)
        caug_ref[h] = w_state * c_aug + lax.dot_general(kw, v_aug, _TN, preferred_element_type=f32)
        m_ref[h:h + 1, :] = jnp.broadcast_to(b_end + g_end, (1, LANES))

    @pl.when(blk == n_blk - 1)
    def _():
        c_out_ref[...] = caug_ref[:, :, :M_DV]
        n_out_ref[...] = caug_ref[:, :, M_DV:]
        m_out_ref[...] = m_ref[...]


def _mlstm(z, gates_t, gate_bias, norm_h, c0, n0, m0, *, n_streams):
    t = z.shape[0]
    s = t // n_streams
    L = _tile(s, MLSTM_BLOCK)
    nb = s // L
    gates3 = gates_t.reshape(N_GATE_ROWS, n_streams * nb, L).transpose(1, 0, 2)
    n0c = n0.reshape(n_streams, M_HEADS, M_DQK, 1)
    m0b = jnp.broadcast_to(m0.reshape(n_streams, M_HEADS, 1), (n_streams, M_HEADS, LANES))

    def zcols(group):
        return pl.BlockSpec((L, M_WIDTH), lambda b, c: (b * nb + c, group))

    def state(*tail):
        return pl.BlockSpec((None, M_HEADS) + tail, lambda b, c: (b, 0) + (0,) * len(tail))

    return pl.pallas_call(
        _mlstm_kernel,
        grid=(n_streams, nb),
        in_specs=[pl.BlockSpec((2 * M_HEADS, 1), lambda b, c: (0, 0)),
                  zcols(0), zcols(1), zcols(2), zcols(3),
                  pl.BlockSpec((None, N_GATE_ROWS, L), lambda b, c: (b * nb + c, 0, 0)),
                  pl.BlockSpec((1, M_WIDTH), lambda b, c: (0, 0)),
                  state(M_DQK, M_DV), state(M_DQK, 1), state(LANES)],
        out_specs=[pl.BlockSpec((L, M_WIDTH), lambda b, c: (b * nb + c, 0)),
                   state(M_DQK, M_DV), state(M_DQK, LANES), state(LANES)],
        out_shape=[jax.ShapeDtypeStruct((t, M_WIDTH), bf16),
                   jax.ShapeDtypeStruct((n_streams, M_HEADS, M_DQK, M_DV), f32),
                   jax.ShapeDtypeStruct((n_streams, M_HEADS, M_DQK, LANES), f32),
                   jax.ShapeDtypeStruct((n_streams, M_HEADS, LANES), f32)],
        scratch_shapes=[pltpu.VMEM((M_HEADS, M_DQK, M_DV + LANES), f32), pltpu.VMEM((M_HEADS, LANES), f32)],
        compiler_params=_params("parallel", "arbitrary"),
        name="mlstm",
    )(gate_bias, z, z, z, z, gates3, norm_h, c0, n0c, m0b)


def _fox_prompt_kernel(q_ref, k_ref, v_ref, cum_ref, o_ref,
                       sa_ref, sb_ref, pa_ref, pb_ref, xa_ref, xb_ref, m_ref, l_ref, acc_ref):
    i = pl.program_id(2)
    tq = o_ref.shape[0]
    heads = range(ATTN_HEADS)

    def head_cols(g):
        return slice(g * F_HEAD_DIM, (g + 1) * F_HEAD_DIM)

    def rows(ref, j, g):
        return ref[pl.ds(pl.multiple_of(j * tq, tq), tq), head_cols(g)]

    def causal(s):
        row = lax.broadcasted_iota(jnp.int32, (tq, tq), 0)
        col = lax.broadcasted_iota(jnp.int32, (tq, tq), 1)
        return jnp.where(col <= row, s, -jnp.inf)

    def scores(j, s_ref, x_ref, masked, q_row=i):
        for g in heads:
            s = lax.dot_general(rows(q_ref, q_row, g), rows(k_ref, j, g), _NT,
                                preferred_element_type=f32) - cum_ref[g, j]
            if masked:
                s = causal(s)
            s_ref[g] = s
            x_ref[g] = jnp.broadcast_to(jnp.max(s, axis=1, keepdims=True), (tq, LANES))

    def values(j, p_ref):
        return [jnp.dot(p_ref[g], rows(v_ref, j, g), preferred_element_type=f32) for g in heads]

    def softmax(s_ref, x_ref, p_ref, pv_prev):
        for g in heads:
            s = s_ref[g]
            x = x_ref[g]
            m_old = m_ref[g]
            m_new = jnp.maximum(m_old, x)
            alpha = jnp.exp2(m_old - m_new)
            p = jnp.exp2(s - jnp.concatenate([m_new] * (tq // LANES), axis=1))
            p_ref[g] = p.astype(bf16)
            l_ref[g] = alpha * l_ref[g] + jnp.sum(p, axis=1, keepdims=True)
            acc_ref[g] = alpha * (acc_ref[g] + pv_prev[g])
            m_ref[g] = m_new

    def add_values(j, p_ref):
        for g, pv in enumerate(values(j, p_ref)):
            acc_ref[g] += pv

    m_ref[...] = jnp.full(m_ref.shape, -jnp.inf, f32)
    l_ref[...] = jnp.zeros(l_ref.shape, f32)
    acc_ref[...] = jnp.zeros(acc_ref.shape, f32)
    pb_ref[...] = jnp.zeros(pb_ref.shape, bf16)

    n = i + 1

    def block_at(t):
        return jnp.where(t == 0, i, t - 1)

    @pl.when(i == 0)
    def _():
        scores(0, sa_ref, xa_ref, True)

    def scores_of_next_row():
        nxt = jnp.minimum(i + 1, pl.num_programs(2) - 1)
        scores(nxt, sa_ref, xa_ref, True, q_row=nxt)

    def pair(t0):
        scores(t0, sb_ref, xb_ref, False)
        softmax(sa_ref, xa_ref, pa_ref, values(jnp.maximum(t0 - 2, 0), pb_ref))
        scores(jnp.minimum(t0 + 1, i), sa_ref, xa_ref, False)
        softmax(sb_ref, xb_ref, pb_ref, values(block_at(t0), pa_ref))

    def long_trip(jj, carry):
        for u in range(0, ATTN_TRIP, 2):
            pair(ATTN_TRIP * jj + u)
        return carry

    def short_trip(jj, carry):
        pair(ATTN_TRIP * (n // ATTN_TRIP) + 2 * jj)
        return carry

    lax.fori_loop(0, n // ATTN_TRIP, long_trip, 0)
    lax.fori_loop(0, (n % ATTN_TRIP) // 2, short_trip, 0)

    @pl.when(n % 2 == 0)
    def _():
        scores_of_next_row()
        add_values(i - 1, pb_ref)

    @pl.when(n % 2 == 1)
    def _():
        softmax(sa_ref, xa_ref, pa_ref, values(jnp.maximum(i - 2, 0), pb_ref))
        scores_of_next_row()
        add_values(block_at(i), pa_ref)

    for g in heads:
        o_ref[:, head_cols(g)] = (acc_ref[g] / l_ref[g]).astype(o_ref.dtype)


def _fox_prompt(z, k, v, cum_t, *, q_col0, n_streams):
    t = z.shape[0]
    s = t // n_streams
    tq = _tile(s, ATTN_BLOCK)
    nq = s // tq
    cum4 = cum_t.reshape(F_HEADS, n_streams * nq, 1, tq)
    width = ATTN_HEADS * F_HEAD_DIM
    q0 = q_col0 // width
    return pl.pallas_call(
        _fox_prompt_kernel,
        grid=(n_streams, F_HEADS // ATTN_HEADS, nq),
        in_specs=[pl.BlockSpec((s, width), lambda b, h, i: (b, q0 + h)),
                  pl.BlockSpec((s, width), lambda b, h, i: (b, h)),
                  pl.BlockSpec((s, width), lambda b, h, i: (b, h)),
                  pl.BlockSpec((ATTN_HEADS, nq, 1, tq), lambda b, h, i: (h, b, 0, 0))],
        out_specs=pl.BlockSpec((tq, width), lambda b, h, i: (b * nq + i, h)),
        out_shape=jax.ShapeDtypeStruct((t, F_WIDTH), bf16),
        scratch_shapes=[pltpu.VMEM((ATTN_HEADS, tq, tq), f32)] * 2 + [pltpu.VMEM((ATTN_HEADS, tq, tq), bf16)] * 2
        + [pltpu.VMEM((ATTN_HEADS, tq, LANES), f32)] * 5,
        compiler_params=_params("parallel", "parallel", "arbitrary"),
        name="fox_prompt",
    )(z, k, v, cum4)


def _fox_decode_kernel(bias_ref, q_ref, kn_ref, vn_ref, kc_ref, vc_ref, cumc_ref, fg_ref, o_ref, logf_ref):
    head = pl.program_id(1)
    L = q_ref.shape[0]
    q = q_ref[...]
    logf = _log_sigmoid(fg_ref[...] + bias_ref[head])
    logf_ref[...] = logf
    cum_new = _cumsum_lanes(logf) * LOG2E
    cum_c = cumc_ref[...]
    cum_c = cum_c - cum_c[:, -1:]

    p_len = kc_ref.shape[0] // F_HEADS
    kc = kc_ref[pl.ds(head, p_len, stride=F_HEADS), :].astype(bf16)
    vc = vc_ref[pl.ds(head, p_len, stride=F_HEADS), :].astype(bf16)
    s_c = lax.dot_general(q, kc, _NT, preferred_element_type=f32) - cum_c
    s_n = lax.dot_general(q, kn_ref[...], _NT, preferred_element_type=f32) - cum_new
    row = lax.broadcasted_iota(jnp.int32, (L, L), 0)
    col = lax.broadcasted_iota(jnp.int32, (L, L), 1)
    s_n = jnp.where(col <= row, s_n, -jnp.inf)
    m = jnp.maximum(jnp.max(s_c, axis=1, keepdims=True), jnp.max(s_n, axis=1, keepdims=True))
    p_c = jnp.exp2(s_c - m)
    p_n = jnp.exp2(s_n - m)
    l = jnp.sum(p_c, axis=1, keepdims=True) + jnp.sum(p_n, axis=1, keepdims=True)
    acc = (jnp.dot(p_c.astype(bf16), vc, preferred_element_type=f32)
           + jnp.dot(p_n.astype(bf16), vn_ref[...], preferred_element_type=f32))
    o_ref[...] = (acc / l).astype(o_ref.dtype)


def _fox_decode(z, k, v, gates_t, bias, cache_k, cache_v, cum_cache, *, q_col0, n_streams):
    t = z.shape[0]
    L = t // n_streams
    p = cache_k.shape[1] // F_HEADS
    gates4 = gates_t.reshape(N_GATE_ROWS, n_streams, 1, L)
    q0 = q_col0 // F_HEAD_DIM
    return pl.pallas_call(
        _fox_decode_kernel,
        grid=(n_streams, F_HEADS),
        in_specs=[pl.BlockSpec(memory_space=pltpu.SMEM),
                  pl.BlockSpec((L, F_HEAD_DIM), lambda b, h: (b, q0 + h)),
                  pl.BlockSpec((L, F_HEAD_DIM), lambda b, h: (b, h)),
                  pl.BlockSpec((L, F_HEAD_DIM), lambda b, h: (b, h)),
                  pl.BlockSpec((None, p * F_HEADS, F_HEAD_DIM), lambda b, h: (b, 0, 0)),
                  pl.BlockSpec((None, p * F_HEADS, F_HEAD_DIM), lambda b, h: (b, 0, 0)),
                  pl.BlockSpec((None, 1, p), lambda b, h: (b * F_HEADS + h, 0, 0)),
                  pl.BlockSpec((None, None, 1, L), lambda b, h: (2 * M_HEADS + h, b, 0, 0))],
        out_specs=[pl.BlockSpec((L, F_HEAD_DIM), lambda b, h: (b, h)),
                   pl.BlockSpec((None, None, 1, L), lambda b, h: (b, h, 0, 0))],
        out_shape=[jax.ShapeDtypeStruct((t, F_WIDTH), bf16),
                   jax.ShapeDtypeStruct((n_streams, F_HEADS, 1, L), f32)],
        compiler_params=_params("parallel", "parallel"),
        name="fox_decode",
    )(bias, z, k, v, cache_k, cache_v, cum_cache, gates4)


def _merge_kernel(x_ref, ha_ref, hb_ref, ga_ref, gb_ref, wa_ref, wb_ref, wo_ref, g_ref, x1_ref, hn_ref):
    pa = jnp.dot(ha_ref[...], wa_ref[...], preferred_element_type=f32)
    pb = jnp.dot(hb_ref[...], wb_ref[...], preferred_element_type=f32)
    merged = (jax.nn.sigmoid(ga_ref[...].astype(f32)) * pa + jax.nn.sigmoid(gb_ref[...].astype(f32)) * pb)
    x1 = x_ref[...] + jnp.dot(merged.astype(bf16), wo_ref[...], preferred_element_type=f32)
    x1_ref[...] = x1
    hn = x1 * lax.rsqrt(jnp.mean(x1 * x1, axis=-1, keepdims=True) + EPS) * g_ref[...]
    hn_ref[...] = hn.astype(hn_ref.dtype)


def _merge(x, h_a, h_b, z, w_a, w_b, w_o, g, *, gate_col0):
    t, d = x.shape
    tm = _tile(t, 256)
    ga_blk = gate_col0 // d
    row = lambda i: (i, 0)
    fixed = lambda i: (0, 0)
    resident = functools.partial(pl.BlockSpec, index_map=fixed, pipeline_mode=pl.Buffered(1))
    return pl.pallas_call(
        _merge_kernel,
        grid=(t // tm,),
        in_specs=[pl.BlockSpec((tm, d), row),
                  pl.BlockSpec((tm, M_WIDTH), row),
                  pl.BlockSpec((tm, F_WIDTH), row),
                  pl.BlockSpec((tm, d), lambda i: (i, ga_blk)),
                  pl.BlockSpec((tm, d), lambda i: (i, ga_blk + 1)),
                  resident(w_a.shape), resident(w_b.shape), resident(w_o.shape),
                  pl.BlockSpec((1, d), fixed)],
        out_specs=[pl.BlockSpec((tm, d), row), pl.BlockSpec((tm, d), row)],
        out_shape=[jax.ShapeDtypeStruct((t, d), f32), jax.ShapeDtypeStruct((t, d), bf16)],
        compiler_params=_params("parallel"),
        name="merge",
    )(x, h_a, h_b, z, z, w_a, w_b, w_o, g)


def _ffn_kernel(hn_ref, x1_ref, wu_ref, wd_ref, g_ref, y_ref):
    f = pl.program_id(1)

    @pl.when(f == 0)
    def _():
        y_ref[...] = x1_ref[...]

    u = jnp.maximum(jnp.dot(hn_ref[...], wu_ref[...], preferred_element_type=f32), 0.0)
    y_ref[...] += jnp.dot((u * u).astype(bf16), wd_ref[...], preferred_element_type=f32)

    @pl.when(f == pl.num_programs(1) - 1)
    def _():
        x2 = y_ref[...]
        y_ref[...] = x2 * lax.rsqrt(jnp.mean(x2 * x2, axis=-1, keepdims=True) + EPS) * g_ref[...]


def _ffn(hn, x1, w_up, w_down, g):
    t, d = x1.shape
    dff = w_up.shape[1]
    tm = _tile(t, 512)
    tf = _tile(dff, 1024)
    return pl.pallas_call(
        _ffn_kernel,
        grid=(t // tm, dff // tf),
        in_specs=[pl.BlockSpec((tm, d), lambda i, f: (i, 0)),
                  pl.BlockSpec((tm, d), lambda i, f: (i, 0)),
                  pl.BlockSpec((d, tf), lambda i, f: (0, f)),
                  pl.BlockSpec((tf, d), lambda i, f: (f, 0)),
                  pl.BlockSpec((1, d), lambda i, f: (0, 0))],
        out_specs=pl.BlockSpec((tm, d), lambda i, f: (i, 0)),
        out_shape=jax.ShapeDtypeStruct((t, d), f32),
        compiler_params=_params("parallel", "arbitrary"),
        name="ffn",
    )(hn, x1, w_up, w_down, g)


def _layer(x3, w, mstate, fox_past):
    nstr, frames, d = x3.shape
    t = nstr * frames
    x = x3.reshape(t, d)
    xn, gates_t = _norm_gates(x, w["norm_mix"], w["w_gates_t"])
    k32, k16, v32, v16, z = _project(xn, w["w_proj"])

    c0, n0, m0 = mstate
    h_a, c_new, n_slab, m_slab = _mlstm(z, gates_t, w["mlstm_bias"], w["norm_mlstm_h"], c0, n0, m0,
                                        n_streams=nstr)
    n_new = n_slab[..., 0]
    m_new = m_slab[:, :, 0]

    if fox_past is None:
        logf_t, cum_t = _scan_rows(gates_t, w["b_fox_f_col"], row_block=1, rows=F_HEADS, n_streams=nstr,
                                   apply_log_sigmoid=True)
        h_b = _fox_prompt(z, k16, v16, cum_t, q_col0=w["q_col0"], n_streams=nstr)
        logf = logf_t.T.reshape(nstr, frames, F_HEADS)
    else:
        ck, cv, clf = fox_past
        p = ck.shape[1]
        clf_t = jnp.transpose(clf, (0, 2, 1)).reshape(nstr * F_HEADS, p)
        _, cum_c = _scan_rows(clf_t, jnp.zeros((nstr * F_HEADS, 1), f32), row_block=0, rows=nstr * F_HEADS,
                              n_streams=1, apply_log_sigmoid=False)
        h_b, logf4 = _fox_decode(z, k16, v16, gates_t, w["b_fox_f"], ck.reshape(nstr, p * F_HEADS, F_HEAD_DIM),
                                 cv.reshape(nstr, p * F_HEADS, F_HEAD_DIM), cum_c.reshape(nstr * F_HEADS, 1, p),
                                 q_col0=w["q_col0"], n_streams=nstr)
        logf = jnp.transpose(logf4[:, :, 0, :], (0, 2, 1))

    x1, hn = _merge(x, h_a, h_b, z, w["w_branch_a"], w["w_branch_b"], w["w_out"], w["norm_ffn"],
                    gate_col0=w["gate_col0"])
    y = _ffn(hn, x1, w["w_up"], w["w_down"], w["norm_final"])

    k_rows = k32.reshape(nstr, frames, F_HEADS, F_HEAD_DIM)
    v_rows = v32.reshape(nstr, frames, F_HEADS, F_HEAD_DIM)
    return y.reshape(nstr, frames, d), (k_rows, v_rows, logf, c_new, n_new, m_new)


def _prepare_weights(norm_mix, w_in, b_mlstm_i, b_mlstm_f, b_fox_f, norm_mlstm_h, w_branch_a, w_branch_b,
                     w_out, norm_ffn, w_up, w_down, norm_final):
    d = w_in.shape[0]
    sizes = (M_HEADS * M_DQK, M_HEADS * M_DQK, M_WIDTH, M_WIDTH, M_HEADS, M_HEADS,
             F_WIDTH, F_WIDTH, F_WIDTH, F_HEADS, d, d)
    offs = [0]
    for s in sizes:
        offs.append(offs[-1] + s)
    mq, mk, mv, mo, mi, mf, fq, fk, fv, ff, ga, gb = (w_in[:, offs[i]:offs[i + 1]] for i in range(12))
    w_proj = jnp.concatenate([fk, fv, mq, mk * (M_DQK ** -0.5), mv, mo, ga, gb,
                              fq * (F_HEAD_DIM ** -0.5 * LOG2E)], axis=1)
    return {
        "norm_mix": norm_mix.reshape(1, d),
        "w_gates_t": jnp.concatenate([mi, mf, ff], axis=1).T.astype(bf16),
        "w_proj": w_proj.astype(bf16),
        "gate_col0": 4 * M_WIDTH,
        "q_col0": 4 * M_WIDTH + 2 * d,
        "mlstm_bias": jnp.concatenate([b_mlstm_i, b_mlstm_f]).astype(f32).reshape(2 * M_HEADS, 1),
        "b_fox_f": b_fox_f.astype(f32),
        "b_fox_f_col": b_fox_f.astype(f32).reshape(F_HEADS, 1),
        "norm_mlstm_h": norm_mlstm_h.reshape(1, M_WIDTH),
        "w_branch_a": w_branch_a.astype(bf16),
        "w_branch_b": w_branch_b.astype(bf16),
        "w_out": w_out.astype(bf16),
        "norm_ffn": norm_ffn.reshape(1, d),
        "w_up": w_up.astype(bf16),
        "w_down": w_down.astype(bf16),
        "norm_final": norm_final.reshape(1, d),
    }


def kernel(x_prompt, x_sample, cache_fox_k, cache_fox_v, cache_fox_logf, state_mlstm_c, state_mlstm_n, state_mlstm_m, norm_mix, w_in, b_mlstm_i, b_mlstm_f, b_fox_f, norm_mlstm_h, w_branch_a, w_branch_b, w_out, norm_ffn, w_up, w_down, norm_final):
    depth = w_in.shape[0]
    assert depth == 1, "the final norm is fused into the layer's FFN kernel"
    w = _prepare_weights(norm_mix[0], w_in[0], b_mlstm_i[0], b_mlstm_f[0], b_fox_f[0], norm_mlstm_h[0],
                         w_branch_a[0], w_branch_b[0], w_out[0], norm_ffn[0], w_up[0], w_down[0], norm_final)
    bp = x_prompt.shape[0]
    fresh = (jnp.zeros((bp, M_HEADS, M_DQK, M_DV), f32), jnp.zeros((bp, M_HEADS, M_DQK), f32),
             jnp.zeros((bp, M_HEADS), f32))
    y_p, st_p = _layer(x_prompt, w, fresh, None)
    y_s, st_s = _layer(x_sample, w, (state_mlstm_c[0], state_mlstm_n[0], state_mlstm_m[0]),
                       (cache_fox_k[0], cache_fox_v[0], cache_fox_logf[0]))
    return (y_p, y_s) + tuple(a[None] for a in st_p) + tuple(a[None] for a in st_s)
```

```python
import functools

import jax
import jax.numpy as jnp
from jax import lax
from jax.experimental import pallas as pl
from jax.experimental.pallas import tpu as pltpu

M_HEADS = 4
M_DQK = 256
M_DV = 256
M_WIDTH = M_HEADS * M_DV
F_HEADS = 8
F_HEAD_DIM = 128
F_WIDTH = F_HEADS * F_HEAD_DIM
EPS = 1e-6
N_GATE_ROWS = 2 * M_HEADS + F_HEADS
LANES = 128
assert F_HEAD_DIM == LANES
MLSTM_BLOCK = 256
ATTN_BLOCK = 512
ATTN_TRIP = 4
ATTN_HEADS = 1
LOG2E = 1.4426950408889634
VMEM_LIMIT_BYTES = 56 * 1024 * 1024

f32 = jnp.float32
bf16 = jnp.bfloat16

_NT = (((1,), (1,)), ((), ()))
_TN = (((0,), (0,)), ((), ()))


def _tile(n, pref):
    t = min(n, pref)
    while n % t:
        t //= 2
    return t


def _params(*sem):
    return pltpu.CompilerParams(dimension_semantics=sem, vmem_limit_bytes=VMEM_LIMIT_BYTES)


def _log_sigmoid(z):
    return jnp.minimum(z, 0.0) - jnp.log1p(jnp.exp(-jnp.abs(z)))


def _cumsum_lanes(x):
    rows, n = x.shape
    pad = -rows % 16
    if rows == 1:
        x16 = jnp.broadcast_to(x, (16, n))
    elif pad == 0:
        x16 = x
    else:
        x16 = jnp.concatenate([x, jnp.zeros((pad, n), f32)], axis=0)
    r = lax.broadcasted_iota(jnp.int32, (n, n), 0)
    c = lax.broadcasted_iota(jnp.int32, (n, n), 1)
    u = jnp.where(r <= c, 1.0, 0.0).astype(bf16)
    hi = x16.astype(bf16)
    rem = x16 - hi.astype(f32)
    mid = rem.astype(bf16)
    lo = (rem - mid.astype(f32)).astype(bf16)
    out = (jnp.dot(hi, u, preferred_element_type=f32) + jnp.dot(mid, u, preferred_element_type=f32)
           + jnp.dot(lo, u, preferred_element_type=f32))
    return out[:rows]


def _norm_kernel(x_ref, g_ref, wg_ref, xn_ref, gt_ref):
    x = x_ref[...]
    y = x * lax.rsqrt(jnp.mean(x * x, axis=-1, keepdims=True) + EPS) * g_ref[...]
    xn = y.astype(bf16)
    xn_ref[...] = xn
    gt_ref[...] = lax.dot_general(wg_ref[...], xn, _NT, preferred_element_type=f32)


def _norm_gates(x, g, wg_t):
    t, d = x.shape
    tm = _tile(t, 512)
    return pl.pallas_call(
        _norm_kernel,
        grid=(t // tm,),
        in_specs=[pl.BlockSpec((tm, d), lambda i: (i, 0)),
                  pl.BlockSpec((1, d), lambda i: (0, 0)),
                  pl.BlockSpec((N_GATE_ROWS, d), lambda i: (0, 0))],
        out_specs=[pl.BlockSpec((tm, d), lambda i: (i, 0)),
                   pl.BlockSpec((N_GATE_ROWS, tm), lambda i: (0, i))],
        out_shape=[jax.ShapeDtypeStruct((t, d), bf16), jax.ShapeDtypeStruct((N_GATE_ROWS, t), f32)],
        compiler_params=_params("parallel"),
        name="norm_gates",
    )(x, g, wg_t)


def _mm_kernel(a_ref, w_ref, *o_refs):
    r = jnp.dot(a_ref[...], w_ref[...], preferred_element_type=f32)
    for o in o_refs:
        o[...] = r.astype(o.dtype)


def _matmul(a, w, out_dtypes, name):
    t, k = a.shape
    n = w.shape[1]
    tm = _tile(t, 1024)
    tn = _tile(n, 1024)
    return pl.pallas_call(
        _mm_kernel,
        grid=(t // tm, n // tn),
        in_specs=[pl.BlockSpec((tm, k), lambda i, j: (i, 0)),
                  pl.BlockSpec((k, tn), lambda i, j: (0, j))],
        out_specs=[pl.BlockSpec((tm, tn), lambda i, j: (i, j)) for _ in out_dtypes],
        out_shape=[jax.ShapeDtypeStruct((t, n), dt) for dt in out_dtypes],
        compiler_params=_params("parallel", "arbitrary"),
        name=name,
    )(a, w)


def _scan_kernel(x_ref, bias_ref, logf_ref, cum_ref, carry_ref, *, apply_log_sigmoid):
    @pl.when(pl.program_id(1) == 0)
    def _():
        carry_ref[...] = jnp.zeros_like(carry_ref)

    x = x_ref[...]
    if apply_log_sigmoid:
        x = _log_sigmoid(x + bias_ref[...])
    logf_ref[...] = x
    cum = _cumsum_lanes(x) + carry_ref[:, :1]
    cum_ref[...] = cum * LOG2E
    carry_ref[...] = jnp.broadcast_to(cum[:, -1:], carry_ref.shape)


def _scan_rows(x, bias, *, row_block, rows, n_streams, apply_log_sigmoid):
    total = x.shape[1]
    s = total // n_streams
    tb = _tile(s, 512)
    nb = s // tb
    spec = pl.BlockSpec((rows, tb), lambda b, j: (0, b * nb + j))
    return pl.pallas_call(
        functools.partial(_scan_kernel, apply_log_sigmoid=apply_log_sigmoid),
        grid=(n_streams, nb),
        in_specs=[pl.BlockSpec((rows, tb), lambda b, j: (row_block, b * nb + j)),
                  pl.BlockSpec((rows, 1), lambda b, j: (0, 0))],
        out_specs=[spec, spec],
        out_shape=[jax.ShapeDtypeStruct((rows, total), f32)] * 2,
        scratch_shapes=[pltpu.VMEM((rows, LANES), f32)],
        compiler_params=_params("arbitrary", "arbitrary"),
        name="logf_scan",
    )(x, bias)


def _mlstm_kernel(bias_ref, q_ref, k_ref, v_ref, o_ref, gates_ref, nh_ref, c0_ref, n0_ref, m0_ref,
                  h_ref, c_out_ref, n_out_ref, m_out_ref, caug_ref, m_ref):
    blk = pl.program_id(1)
    n_blk = pl.num_programs(1)
    L = q_ref.shape[0]

    @pl.when(blk == 0)
    def _():
        lane = lax.broadcasted_iota(jnp.int32, (M_HEADS, M_DQK, LANES), 2)
        caug_ref[:, :, :M_DV] = c0_ref[...]
        caug_ref[:, :, M_DV:] = jnp.where(lane == 0, n0_ref[...], 0.0)
        m_ref[...] = m0_ref[...]

    gates = gates_ref[...]
    ig_all = gates[:M_HEADS] + bias_ref[:M_HEADS]
    lf_all = _log_sigmoid(gates[M_HEADS:2 * M_HEADS] + bias_ref[M_HEADS:])
    a_all = ig_all - _cumsum_lanes(lf_all)

    t_idx = lax.broadcasted_iota(jnp.int32, (L, L), 0)
    s_idx = lax.broadcasted_iota(jnp.int32, (L, L), 1)
    causal = s_idx <= t_idx
    diag = s_idx == t_idx
    ones_col = jnp.where(lax.broadcasted_iota(jnp.int32, (L, LANES), 1) == 0, 1.0, 0.0).astype(bf16)

    for h in range(M_HEADS):
        cols = slice(h * M_DV, (h + 1) * M_DV)
        m0 = m_ref[h:h + 1, :1]
        ig = ig_all[h:h + 1]
        lf = lf_all[h:h + 1]
        a_row = a_all[h:h + 1]
        a_mat = jnp.where(causal, a_row, -jnp.inf)
        g_col = jnp.maximum(m0, jnp.max(a_mat, axis=1, keepdims=True))
        b_col = jnp.sum(jnp.where(causal, lf, 0.0), axis=1, keepdims=True)
        w_intra = jnp.exp(a_mat - g_col)
        w_inter = jnp.exp(m0 - g_col)

        q = q_ref[:, cols]
        k = k_ref[:, cols]
        v_aug = jnp.concatenate([v_ref[:, cols], ones_col], axis=1)
        c_aug = caug_ref[h]

        s = lax.dot_general(q, k, _NT, preferred_element_type=f32)
        sw = (s * w_intra).astype(bf16)
        num_aug = (jnp.dot(sw, v_aug, preferred_element_type=f32)
                   + w_inter * jnp.dot(q, c_aug.astype(bf16), preferred_element_type=f32))
        num = num_aug[:, :M_DV]
        den = num_aug[:, M_DV:M_DV + 1]
        den = jnp.maximum(jnp.abs(den), jnp.exp(-(b_col + g_col)))
        hh = num / den
        hh = hh * lax.rsqrt(jnp.mean(hh * hh, axis=-1, keepdims=True) + EPS)
        hh = hh * nh_ref[:, cols] * jax.nn.sigmoid(o_ref[:, cols].astype(f32))
        h_ref[:, cols] = hh.astype(h_ref.dtype)

        g_end = jnp.maximum(m0, jnp.max(a_row, axis=1, keepdims=True))
        b_end = jnp.sum(lf, axis=1, keepdims=True)
        ig_col = jnp.sum(jnp.where(diag, ig, 0.0), axis=1, keepdims=True)
        w_tok = jnp.exp(ig_col - b_col - g_end)
        w_state = jnp.exp(m0 - g_end)
        kw = (k.astype(f32) * w_tok).astype(bf16)
        caug_ref[h] = w_state * c_aug + lax.dot_general(kw, v_aug, _TN, preferred_element_type=f32)
        m_ref[h:h + 1, :] = jnp.broadcast_to(b_end + g_end, (1, LANES))

    @pl.when(blk == n_blk - 1)
    def _():
        c_out_ref[...] = caug_ref[:, :, :M_DV]
        n_out_ref[...] = caug_ref[:, :, M_DV:]
        m_out_ref[...] = m_ref[...]


def _mlstm(z, gates_t, gate_bias, norm_h, c0, n0, m0, *, n_streams):
    t = z.shape[0]
    s = t // n_streams
    L = _tile(s, MLSTM_BLOCK)
    nb = s // L
    gates3 = gates_t.reshape(N_GATE_ROWS, n_streams * nb, L).transpose(1, 0, 2)
    n0c = n0.reshape(n_streams, M_HEADS, M_DQK, 1)
    m0b = jnp.broadcast_to(m0.reshape(n_streams, M_HEADS, 1), (n_streams, M_HEADS, LANES))

    def zcols(group):
        return pl.BlockSpec((L, M_WIDTH), lambda b, c: (b * nb + c, group))

    def state(*tail):
        return pl.BlockSpec((None, M_HEADS) + tail, lambda b, c: (b, 0) + (0,) * len(tail))

    return pl.pallas_call(
        _mlstm_kernel,
        grid=(n_streams, nb),
        in_specs=[pl.BlockSpec((2 * M_HEADS, 1), lambda b, c: (0, 0)),
                  zcols(0), zcols(1), zcols(2), zcols(3),
                  pl.BlockSpec((None, N_GATE_ROWS, L), lambda b, c: (b * nb + c, 0, 0)),
                  pl.BlockSpec((1, M_WIDTH), lambda b, c: (0, 0)),
                  state(M_DQK, M_DV), state(M_DQK, 1), state(LANES)],
        out_specs=[pl.BlockSpec((L, M_WIDTH), lambda b, c: (b * nb + c, 0)),
                   state(M_DQK, M_DV), state(M_DQK, LANES), state(LANES)],
        out_shape=[jax.ShapeDtypeStruct((t, M_WIDTH), bf16),
                   jax.ShapeDtypeStruct((n_streams, M_HEADS, M_DQK, M_DV), f32),
                   jax.ShapeDtypeStruct((n_streams, M_HEADS, M_DQK, LANES), f32),
                   jax.ShapeDtypeStruct((n_streams, M_HEADS, LANES), f32)],
        scratch_shapes=[pltpu.VMEM((M_HEADS, M_DQK, M_DV + LANES), f32), pltpu.VMEM((M_HEADS, LANES), f32)],
        compiler_params=_params("parallel", "arbitrary"),
        name="mlstm",
    )(gate_bias, z, z, z, z, gates3, norm_h, c0, n0c, m0b)


def _fox_prompt_kernel(q_ref, k_ref, v_ref, cum_ref, o_ref,
                       sa_ref, sb_ref, pa_ref, pb_ref, xa_ref, xb_ref, m_ref, l_ref, acc_ref):
    i = pl.program_id(2)
    tq = o_ref.shape[0]
    heads = range(ATTN_HEADS)

    def head_cols(g):
        return slice(g * F_HEAD_DIM, (g + 1) * F_HEAD_DIM)

    def rows(ref, j, g):
        return ref[pl.ds(pl.multiple_of(j * tq, tq), tq), head_cols(g)]

    def causal(s):
        row = lax.broadcasted_iota(jnp.int32, (tq, tq), 0)
        col = lax.broadcasted_iota(jnp.int32, (tq, tq), 1)
        return jnp.where(col <= row, s, -jnp.inf)

    def scores(j, s_ref, x_ref, masked, q_row=i):
        for g in heads:
            s = lax.dot_general(rows(q_ref, q_row, g), rows(k_ref, j, g), _NT,
                                preferred_element_type=f32) - cum_ref[g, j]
            if masked:
                s = causal(s)
            s_ref[g] = s
            x_ref[g] = jnp.broadcast_to(jnp.max(s, axis=1, keepdims=True), (tq, LANES))

    def values(j, p_ref):
        return [jnp.dot(p_ref[g], rows(v_ref, j, g), preferred_element_type=f32) for g in heads]

    def softmax(s_ref, x_ref, p_ref, pv_prev):
        for g in heads:
            s = s_ref[g]
            x = x_ref[g]
            m_old = m_ref[g]
            m_new = jnp.maximum(m_old, x)
            alpha = jnp.exp2(m_old - m_new)
            p = jnp.exp2(s - jnp.concatenate([m_new] * (tq // LANES), axis=1))
            p_ref[g] = p.astype(bf16)
            l_ref[g] = alpha * l_ref[g] + jnp.sum(p, axis=1, keepdims=True)
            acc_ref[g] = alpha * (acc_ref[g] + pv_prev[g])
            m_ref[g] = m_new

    def add_values(j, p_ref):
        for g, pv in enumerate(values(j, p_ref)):
            acc_ref[g] += pv

    m_ref[...] = jnp.full(m_ref.shape, -jnp.inf, f32)
    l_ref[...] = jnp.zeros(l_ref.shape, f32)
    acc_ref[...] = jnp.zeros(acc_ref.shape, f32)
    pb_ref[...] = jnp.zeros(pb_ref.shape, bf16)

    n = i + 1

    def block_at(t):
        return jnp.where(t == 0, i, t - 1)

    @pl.when(i == 0)
    def _():
        scores(0, sa_ref, xa_ref, True)

    def scores_of_next_row():
        nxt = jnp.minimum(i + 1, pl.num_programs(2) - 1)
        scores(nxt, sa_ref, xa_ref, True, q_row=nxt)

    def pair(t0):
        scores(t0, sb_ref, xb_ref, False)
        softmax(sa_ref, xa_ref, pa_ref, values(jnp.maximum(t0 - 2, 0), pb_ref))
        scores(jnp.minimum(t0 + 1, i), sa_ref, xa_ref, False)
        softmax(sb_ref, xb_ref, pb_ref, values(block_at(t0), pa_ref))

    def long_trip(jj, carry):
        for u in range(0, ATTN_TRIP, 2):
            pair(ATTN_TRIP * jj + u)
        return carry

    def short_trip(jj, carry):
        pair(ATTN_TRIP * (n // ATTN_TRIP) + 2 * jj)
        return carry

    lax.fori_loop(0, n // ATTN_TRIP, long_trip, 0)
    lax.fori_loop(0, (n % ATTN_TRIP) // 2, short_trip, 0)

    @pl.when(n % 2 == 0)
    def _():
        scores_of_next_row()
        add_values(i - 1, pb_ref)

    @pl.when(n % 2 == 1)
    def _():
        softmax(sa_ref, xa_ref, pa_ref, values(jnp.maximum(i - 2, 0), pb_ref))
        scores_of_next_row()
        add_values(block_at(i), pa_ref)

    for g in heads:
        o_ref[:, head_cols(g)] = (acc_ref[g] / l_ref[g]).astype(o_ref.dtype)


def _fox_prompt(z, k, v, cum_t, *, q_col0, n_streams):
    t = z.shape[0]
    s = t // n_streams
    tq = _tile(s, ATTN_BLOCK)
    nq = s // tq
    cum4 = cum_t.reshape(F_HEADS, n_streams * nq, 1, tq)
    width = ATTN_HEADS * F_HEAD_DIM
    q0 = q_col0 // width
    return pl.pallas_call(
        _fox_prompt_kernel,
        grid=(n_streams, F_HEADS // ATTN_HEADS, nq),
        in_specs=[pl.BlockSpec((s, width), lambda b, h, i: (b, q0 + h)),
                  pl.BlockSpec((s, width), lambda b, h, i: (b, h)),
                  pl.BlockSpec((s, width), lambda b, h, i: (b, h)),
                  pl.BlockSpec((ATTN_HEADS, nq, 1, tq), lambda b, h, i: (h, b, 0, 0))],
        out_specs=pl.BlockSpec((tq, width), lambda b, h, i: (b * nq + i, h)),
        out_shape=jax.ShapeDtypeStruct((t, F_WIDTH), bf16),
        scratch_shapes=[pltpu.VMEM((ATTN_HEADS, tq, tq), f32)] * 2 + [pltpu.VMEM((ATTN_HEADS, tq, tq), bf16)] * 2
        + [pltpu.VMEM((ATTN_HEADS, tq, LANES), f32)] * 5,
        compiler_params=_params("parallel", "parallel", "arbitrary"),
        name="fox_prompt",
    )(z, k, v, cum4)


def _fox_decode_kernel(bias_ref, q_ref, kn_ref, vn_ref, kc_ref, vc_ref, cumc_ref, fg_ref, o_ref, logf_ref):
    head = pl.program_id(1)
    L = q_ref.shape[0]
    q = q_ref[...]
    logf = _log_sigmoid(fg_ref[...] + bias_ref[head])
    logf_ref[...] = logf
    cum_new = _cumsum_lanes(logf) * LOG2E
    cum_c = cumc_ref[...]
    cum_c = cum_c - cum_c[:, -1:]

    p_len = kc_ref.shape[0] // F_HEADS
    kc = kc_ref[pl.ds(head, p_len, stride=F_HEADS), :].astype(bf16)
    vc = vc_ref[pl.ds(head, p_len, stride=F_HEADS), :].astype(bf16)
    s_c = lax.dot_general(q, kc, _NT, preferred_element_type=f32) - cum_c
    s_n = lax.dot_general(q, kn_ref[...], _NT, preferred_element_type=f32) - cum_new
    row = lax.broadcasted_iota(jnp.int32, (L, L), 0)
    col = lax.broadcasted_iota(jnp.int32, (L, L), 1)
    s_n = jnp.where(col <= row, s_n, -jnp.inf)
    m = jnp.maximum(jnp.max(s_c, axis=1, keepdims=True), jnp.max(s_n, axis=1, keepdims=True))
    p_c = jnp.exp2(s_c - m)
    p_n = jnp.exp2(s_n - m)
    l = jnp.sum(p_c, axis=1, keepdims=True) + jnp.sum(p_n, axis=1, keepdims=True)
    acc = (jnp.dot(p_c.astype(bf16), vc, preferred_element_type=f32)
           + jnp.dot(p_n.astype(bf16), vn_ref[...], preferred_element_type=f32))
    o_ref[...] = (acc / l).astype(o_ref.dtype)


def _fox_decode(z, k, v, gates_t, bias, cache_k, cache_v, cum_cache, *, q_col0, n_streams):
    t = z.shape[0]
    L = t // n_streams
    p = cache_k.shape[1] // F_HEADS
    gates4 = gates_t.reshape(N_GATE_ROWS, n_streams, 1, L)
    q0 = q_col0 // F_HEAD_DIM
    return pl.pallas_call(
        _fox_decode_kernel,
        grid=(n_streams, F_HEADS),
        in_specs=[pl.BlockSpec(memory_space=pltpu.SMEM),
                  pl.BlockSpec((L, F_HEAD_DIM), lambda b, h: (b, q0 + h)),
                  pl.BlockSpec((L, F_HEAD_DIM), lambda b, h: (b, h)),
                  pl.BlockSpec((L, F_HEAD_DIM), lambda b, h: (b, h)),
                  pl.BlockSpec((None, p * F_HEADS, F_HEAD_DIM), lambda b, h: (b, 0, 0)),
                  pl.BlockSpec((None, p * F_HEADS, F_HEAD_DIM), lambda b, h: (b, 0, 0)),
                  pl.BlockSpec((None, 1, p), lambda b, h: (b * F_HEADS + h, 0, 0)),
                  pl.BlockSpec((None, None, 1, L), lambda b, h: (2 * M_HEADS + h, b, 0, 0))],
        out_specs=[pl.BlockSpec((L, F_HEAD_DIM), lambda b, h: (b, h)),
                   pl.BlockSpec((None, None, 1, L), lambda b, h: (b, h, 0, 0))],
        out_shape=[jax.ShapeDtypeStruct((t, F_WIDTH), bf16),
                   jax.ShapeDtypeStruct((n_streams, F_HEADS, 1, L), f32)],
        compiler_params=_params("parallel", "parallel"),
        name="fox_decode",
    )(bias, z, k, v, cache_k, cache_v, cum_cache, gates4)


def _merge_kernel(x_ref, ha_ref, hb_ref, ga_ref, gb_ref, wa_ref, wb_ref, wo_ref, g_ref, x1_ref, hn_ref):
    pa = jnp.dot(ha_ref[...], wa_ref[...], preferred_element_type=f32)
    pb = jnp.dot(hb_ref[...], wb_ref[...], preferred_element_type=f32)
    merged = (jax.nn.sigmoid(ga_ref[...].astype(f32)) * pa + jax.nn.sigmoid(gb_ref[...].astype(f32)) * pb)
    x1 = x_ref[...] + jnp.dot(merged.astype(bf16), wo_ref[...], preferred_element_type=f32)
    x1_ref[...] = x1
    hn = x1 * lax.rsqrt(jnp.mean(x1 * x1, axis=-1, keepdims=True) + EPS) * g_ref[...]
    hn_ref[...] = hn.astype(hn_ref.dtype)


def _merge(x, h_a, h_b, z, w_a, w_b, w_o, g, *, gate_col0):
    t, d = x.shape
    tm = _tile(t, 256)
    ga_blk = gate_col0 // d
    row = lambda i: (i, 0)
    fixed = lambda i: (0, 0)
    resident = functools.partial(pl.BlockSpec, index_map=fixed, pipeline_mode=pl.Buffered(1))
    return pl.pallas_call(
        _merge_kernel,
        grid=(t // tm,),
        in_specs=[pl.BlockSpec((tm, d), row),
                  pl.BlockSpec((tm, M_WIDTH), row),
                  pl.BlockSpec((tm, F_WIDTH), row),
                  pl.BlockSpec((tm, d), lambda i: (i, ga_blk)),
                  pl.BlockSpec((tm, d), lambda i: (i, ga_blk + 1)),
                  resident(w_a.shape), resident(w_b.shape), resident(w_o.shape),
                  pl.BlockSpec((1, d), fixed)],
        out_specs=[pl.BlockSpec((tm, d), row), pl.BlockSpec((tm, d), row)],
        out_shape=[jax.ShapeDtypeStruct((t, d), f32), jax.ShapeDtypeStruct((t, d), bf16)],
        compiler_params=_params("parallel"),
        name="merge",
    )(x, h_a, h_b, z, z, w_a, w_b, w_o, g)


def _ffn_kernel(hn_ref, x1_ref, wu_ref, wd_ref, g_ref, y_ref):
    f = pl.program_id(1)

    @pl.when(f == 0)
    def _():
        y_ref[...] = x1_ref[...]

    u = jnp.maximum(jnp.dot(hn_ref[...], wu_ref[...], preferred_element_type=f32), 0.0)
    y_ref[...] += jnp.dot((u * u).astype(bf16), wd_ref[...], preferred_element_type=f32)

    @pl.when(f == pl.num_programs(1) - 1)
    def _():
        x2 = y_ref[...]
        y_ref[...] = x2 * lax.rsqrt(jnp.mean(x2 * x2, axis=-1, keepdims=True) + EPS) * g_ref[...]


def _ffn(hn, x1, w_up, w_down, g):
    t, d = x1.shape
    dff = w_up.shape[1]
    tm = _tile(t, 512)
    tf = _tile(dff, 1024)
    return pl.pallas_call(
        _ffn_kernel,
        grid=(t // tm, dff // tf),
        in_specs=[pl.BlockSpec((tm, d), lambda i, f: (i, 0)),
                  pl.BlockSpec((tm, d), lambda i, f: (i, 0)),
                  pl.BlockSpec((d, tf), lambda i, f: (0, f)),
                  pl.BlockSpec((tf, d), lambda i, f: (f, 0)),
                  pl.BlockSpec((1, d), lambda i, f: (0, 0))],
        out_specs=pl.BlockSpec((tm, d), lambda i, f: (i, 0)),
        out_shape=jax.ShapeDtypeStruct((t, d), f32),
        compiler_params=_params("parallel", "arbitrary"),
        name="ffn",
    )(hn, x1, w_up, w_down, g)


def _layer(x3, w, mstate, fox_past):
    nstr, frames, d = x3.shape
    t = nstr * frames
    x = x3.reshape(t, d)
    xn, gates_t = _norm_gates(x, w["norm_mix"], w["w_gates_t"])
    k32, k16 = _matmul(xn, w["w_fk"], (f32, bf16), "proj_k")
    v32, v16 = _matmul(xn, w["w_fv"], (f32, bf16), "proj_v")
    (z,) = _matmul(xn, w["w_rest"], (bf16,), "proj_rest")

    c0, n0, m0 = mstate
    h_a, c_new, n_slab, m_slab = _mlstm(z, gates_t, w["mlstm_bias"], w["norm_mlstm_h"], c0, n0, m0,
                                        n_streams=nstr)
    n_new = n_slab[..., 0]
    m_new = m_slab[:, :, 0]

    if fox_past is None:
        logf_t, cum_t = _scan_rows(gates_t, w["b_fox_f_col"], row_block=1, rows=F_HEADS, n_streams=nstr,
                                   apply_log_sigmoid=True)
        h_b = _fox_prompt(z, k16, v16, cum_t, q_col0=w["q_col0"], n_streams=nstr)
        logf = logf_t.T.reshape(nstr, frames, F_HEADS)
    else:
        ck, cv, clf = fox_past
        p = ck.shape[1]
        clf_t = jnp.transpose(clf, (0, 2, 1)).reshape(nstr * F_HEADS, p)
        _, cum_c = _scan_rows(clf_t, jnp.zeros((nstr * F_HEADS, 1), f32), row_block=0, rows=nstr * F_HEADS,
                              n_streams=1, apply_log_sigmoid=False)
        h_b, logf4 = _fox_decode(z, k16, v16, gates_t, w["b_fox_f"], ck.reshape(nstr, p * F_HEADS, F_HEAD_DIM),
                                 cv.reshape(nstr, p * F_HEADS, F_HEAD_DIM), cum_c.reshape(nstr * F_HEADS, 1, p),
                                 q_col0=w["q_col0"], n_streams=nstr)
        logf = jnp.transpose(logf4[:, :, 0, :], (0, 2, 1))

    x1, hn = _merge(x, h_a, h_b, z, w["w_branch_a"], w["w_branch_b"], w["w_out"], w["norm_ffn"],
                    gate_col0=w["gate_col0"])
    y = _ffn(hn, x1, w["w_up"], w["w_down"], w["norm_final"])

    k_rows = k32.reshape(nstr, frames, F_HEADS, F_HEAD_DIM)
    v_rows = v32.reshape(nstr, frames, F_HEADS, F_HEAD_DIM)
    return y.reshape(nstr, frames, d), (k_rows, v_rows, logf, c_new, n_new, m_new)


def _prepare_weights(norm_mix, w_in, b_mlstm_i, b_mlstm_f, b_fox_f, norm_mlstm_h, w_branch_a, w_branch_b,
                     w_out, norm_ffn, w_up, w_down, norm_final):
    d = w_in.shape[0]
    sizes = (M_HEADS * M_DQK, M_HEADS * M_DQK, M_WIDTH, M_WIDTH, M_HEADS, M_HEADS,
             F_WIDTH, F_WIDTH, F_WIDTH, F_HEADS, d, d)
    offs = [0]
    for s in sizes:
        offs.append(offs[-1] + s)
    mq, mk, mv, mo, mi, mf, fq, fk, fv, ff, ga, gb = (w_in[:, offs[i]:offs[i + 1]] for i in range(12))
    w_rest = jnp.concatenate([mq, mk * (M_DQK ** -0.5), mv, mo, ga, gb, fq * (F_HEAD_DIM ** -0.5 * LOG2E)], axis=1)
    return {
        "norm_mix": norm_mix.reshape(1, d),
        "w_gates_t": jnp.concatenate([mi, mf, ff], axis=1).T.astype(bf16),
        "w_fk": fk.astype(bf16),
        "w_fv": fv.astype(bf16),
        "w_rest": w_rest.astype(bf16),
        "gate_col0": 4 * M_WIDTH,
        "q_col0": 4 * M_WIDTH + 2 * d,
        "mlstm_bias": jnp.concatenate([b_mlstm_i, b_mlstm_f]).astype(f32).reshape(2 * M_HEADS, 1),
        "b_fox_f": b_fox_f.astype(f32),
        "b_fox_f_col": b_fox_f.astype(f32).reshape(F_HEADS, 1),
        "norm_mlstm_h": norm_mlstm_h.reshape(1, M_WIDTH),
        "w_branch_a": w_branch_a.astype(bf16),
        "w_branch_b": w_branch_b.astype(bf16),
        "w_out": w_out.astype(bf16),
        "norm_ffn": norm_ffn.reshape(1, d),
        "w_up": w_up.astype(bf16),
        "w_down": w_down.astype(bf16),
        "norm_final": norm_final.reshape(1, d),
    }


def kernel(x_prompt, x_sample, cache_fox_k, cache_fox_v, cache_fox_logf, state_mlstm_c, state_mlstm_n, state_mlstm_m, norm_mix, w_in, b_mlstm_i, b_mlstm_f, b_fox_f, norm_mlstm_h, w_branch_a, w_branch_b, w_out, norm_ffn, w_up, w_down, norm_final):
    depth = w_in.shape[0]
    assert depth == 1, "the final norm is fused into the layer's FFN kernel"
    w = _prepare_weights(norm_mix[0], w_in[0], b_mlstm_i[0], b_mlstm_f[0], b_fox_f[0], norm_mlstm_h[0],
                         w_branch_a[0], w_branch_b[0], w_out[0], norm_ffn[0], w_up[0], w_down[0], norm_final)
    bp = x_prompt.shape[0]
    fresh = (jnp.zeros((bp, M_HEADS, M_DQK, M_DV), f32), jnp.zeros((bp, M_HEADS, M_DQK), f32),
             jnp.zeros((bp, M_HEADS), f32))
    y_p, st_p = _layer(x_prompt, w, fresh, None)
    y_s, st_s = _layer(x_sample, w, (state_mlstm_c[0], state_mlstm_n[0], state_mlstm_m[0]),
                       (cache_fox_k[0], cache_fox_v[0], cache_fox_logf[0]))
    return (y_p, y_s) + tuple(a[None] for a in st_p) + tuple(a[None] for a in st_s)
```

```python
import functools

import jax
import jax.numpy as jnp
from jax import lax
from jax.experimental import pallas as pl
from jax.experimental.pallas import tpu as pltpu

M_HEADS = 4
M_DQK = 256
M_DV = 256
M_WIDTH = M_HEADS * M_DV
F_HEADS = 8
F_HEAD_DIM = 128
F_WIDTH = F_HEADS * F_HEAD_DIM
EPS = 1e-6
N_GATE_ROWS = 2 * M_HEADS + F_HEADS
LANES = 128
assert F_HEAD_DIM == LANES
MLSTM_BLOCK = 256
ATTN_BLOCK = 512
ATTN_TRIP = 4
LOG2E = 1.4426950408889634
VMEM_LIMIT_BYTES = 56 * 1024 * 1024

f32 = jnp.float32
bf16 = jnp.bfloat16

_NT = (((1,), (1,)), ((), ()))
_TN = (((0,), (0,)), ((), ()))


def _tile(n, pref):
    t = min(n, pref)
    while n % t:
        t //= 2
    return t


def _params(*sem):
    return pltpu.CompilerParams(dimension_semantics=sem, vmem_limit_bytes=VMEM_LIMIT_BYTES)


def _log_sigmoid(z):
    return jnp.minimum(z, 0.0) - jnp.log1p(jnp.exp(-jnp.abs(z)))


def _cumsum_lanes(x):
    rows, n = x.shape
    pad = -rows % 16
    if rows == 1:
        x16 = jnp.broadcast_to(x, (16, n))
    elif pad == 0:
        x16 = x
    else:
        x16 = jnp.concatenate([x, jnp.zeros((pad, n), f32)], axis=0)
    r = lax.broadcasted_iota(jnp.int32, (n, n), 0)
    c = lax.broadcasted_iota(jnp.int32, (n, n), 1)
    u = jnp.where(r <= c, 1.0, 0.0).astype(bf16)
    hi = x16.astype(bf16)
    rem = x16 - hi.astype(f32)
    mid = rem.astype(bf16)
    lo = (rem - mid.astype(f32)).astype(bf16)
    out = (jnp.dot(hi, u, preferred_element_type=f32) + jnp.dot(mid, u, preferred_element_type=f32)
           + jnp.dot(lo, u, preferred_element_type=f32))
    return out[:rows]


def _norm_kernel(x_ref, g_ref, wg_ref, xn_ref, gt_ref):
    x = x_ref[...]
    y = x * lax.rsqrt(jnp.mean(x * x, axis=-1, keepdims=True) + EPS) * g_ref[...]
    xn = y.astype(bf16)
    xn_ref[...] = xn
    gt_ref[...] = lax.dot_general(wg_ref[...], xn, _NT, preferred_element_type=f32)


def _norm_gates(x, g, wg_t):
    t, d = x.shape
    tm = _tile(t, 512)
    return pl.pallas_call(
        _norm_kernel,
        grid=(t // tm,),
        in_specs=[pl.BlockSpec((tm, d), lambda i: (i, 0)),
                  pl.BlockSpec((1, d), lambda i: (0, 0)),
                  pl.BlockSpec((N_GATE_ROWS, d), lambda i: (0, 0))],
        out_specs=[pl.BlockSpec((tm, d), lambda i: (i, 0)),
                   pl.BlockSpec((N_GATE_ROWS, tm), lambda i: (0, i))],
        out_shape=[jax.ShapeDtypeStruct((t, d), bf16), jax.ShapeDtypeStruct((N_GATE_ROWS, t), f32)],
        compiler_params=_params("parallel"),
        name="norm_gates",
    )(x, g, wg_t)


def _mm_kernel(a_ref, w_ref, *o_refs):
    r = jnp.dot(a_ref[...], w_ref[...], preferred_element_type=f32)
    for o in o_refs:
        o[...] = r.astype(o.dtype)


def _matmul(a, w, out_dtypes, name):
    t, k = a.shape
    n = w.shape[1]
    tm = _tile(t, 1024)
    tn = _tile(n, 1024)
    return pl.pallas_call(
        _mm_kernel,
        grid=(t // tm, n // tn),
        in_specs=[pl.BlockSpec((tm, k), lambda i, j: (i, 0)),
                  pl.BlockSpec((k, tn), lambda i, j: (0, j))],
        out_specs=[pl.BlockSpec((tm, tn), lambda i, j: (i, j)) for _ in out_dtypes],
        out_shape=[jax.ShapeDtypeStruct((t, n), dt) for dt in out_dtypes],
        compiler_params=_params("parallel", "arbitrary"),
        name=name,
    )(a, w)


def _mm_heads_kernel(a_ref, w_ref, rows_ref, o16_ref):
    r = jnp.dot(a_ref[...], w_ref[...], preferred_element_type=f32)
    o16_ref[...] = r.astype(bf16)
    tm = a_ref.shape[0]
    for h in range(F_HEADS):
        rows_ref[pl.ds(h, tm, stride=F_HEADS), :] = r[:, h * F_HEAD_DIM:(h + 1) * F_HEAD_DIM]


def _matmul_heads(a, w, name):
    t, k = a.shape
    assert w.shape[1] == F_WIDTH
    tm = _tile(t, 1024)
    return pl.pallas_call(
        _mm_heads_kernel,
        grid=(t // tm,),
        in_specs=[pl.BlockSpec((tm, k), lambda i: (i, 0)),
                  pl.BlockSpec((k, F_WIDTH), lambda i: (0, 0))],
        out_specs=[pl.BlockSpec((tm * F_HEADS, F_HEAD_DIM), lambda i: (i, 0)),
                   pl.BlockSpec((tm, F_WIDTH), lambda i: (i, 0))],
        out_shape=[jax.ShapeDtypeStruct((t * F_HEADS, F_HEAD_DIM), f32), jax.ShapeDtypeStruct((t, F_WIDTH), bf16)],
        compiler_params=_params("parallel"),
        name=name,
    )(a, w)


def _scan_kernel(x_ref, bias_ref, logf_ref, cum_ref, carry_ref, *, apply_log_sigmoid):
    @pl.when(pl.program_id(1) == 0)
    def _():
        carry_ref[...] = jnp.zeros_like(carry_ref)

    x = x_ref[...]
    if apply_log_sigmoid:
        x = _log_sigmoid(x + bias_ref[...])
    logf_ref[...] = x
    cum = _cumsum_lanes(x) + carry_ref[:, :1]
    cum_ref[...] = cum * LOG2E
    carry_ref[...] = jnp.broadcast_to(cum[:, -1:], carry_ref.shape)


def _scan_rows(x, bias, *, row_block, rows, n_streams, apply_log_sigmoid):
    total = x.shape[1]
    s = total // n_streams
    tb = _tile(s, 512)
    nb = s // tb
    spec = pl.BlockSpec((rows, tb), lambda b, j: (0, b * nb + j))
    return pl.pallas_call(
        functools.partial(_scan_kernel, apply_log_sigmoid=apply_log_sigmoid),
        grid=(n_streams, nb),
        in_specs=[pl.BlockSpec((rows, tb), lambda b, j: (row_block, b * nb + j)),
                  pl.BlockSpec((rows, 1), lambda b, j: (0, 0))],
        out_specs=[spec, spec],
        out_shape=[jax.ShapeDtypeStruct((rows, total), f32)] * 2,
        scratch_shapes=[pltpu.VMEM((rows, LANES), f32)],
        compiler_params=_params("arbitrary", "arbitrary"),
        name="logf_scan",
    )(x, bias)


def _mlstm_kernel(bias_ref, q_ref, k_ref, v_ref, o_ref, gates_ref, nh_ref, c0_ref, n0_ref, m0_ref,
                  h_ref, c_out_ref, n_out_ref, m_out_ref, caug_ref, m_ref):
    blk = pl.program_id(1)
    n_blk = pl.num_programs(1)
    L = q_ref.shape[0]

    @pl.when(blk == 0)
    def _():
        lane = lax.broadcasted_iota(jnp.int32, (M_HEADS, M_DQK, LANES), 2)
        caug_ref[:, :, :M_DV] = c0_ref[...]
        caug_ref[:, :, M_DV:] = jnp.where(lane == 0, n0_ref[...], 0.0)
        m_ref[...] = m0_ref[...]

    gates = gates_ref[...]
    ig_all = gates[:M_HEADS] + bias_ref[:M_HEADS]
    lf_all = _log_sigmoid(gates[M_HEADS:2 * M_HEADS] + bias_ref[M_HEADS:])
    a_all = ig_all - _cumsum_lanes(lf_all)

    t_idx = lax.broadcasted_iota(jnp.int32, (L, L), 0)
    s_idx = lax.broadcasted_iota(jnp.int32, (L, L), 1)
    causal = s_idx <= t_idx
    diag = s_idx == t_idx
    ones_col = jnp.where(lax.broadcasted_iota(jnp.int32, (L, LANES), 1) == 0, 1.0, 0.0).astype(bf16)

    for h in range(M_HEADS):
        cols = slice(h * M_DV, (h + 1) * M_DV)
        m0 = m_ref[h:h + 1, :1]
        ig = ig_all[h:h + 1]
        lf = lf_all[h:h + 1]
        a_row = a_all[h:h + 1]
        a_mat = jnp.where(causal, a_row, -jnp.inf)
        g_col = jnp.maximum(m0, jnp.max(a_mat, axis=1, keepdims=True))
        b_col = jnp.sum(jnp.where(causal, lf, 0.0), axis=1, keepdims=True)
        w_intra = jnp.exp(a_mat - g_col)
        w_inter = jnp.exp(m0 - g_col)

        q = q_ref[:, cols]
        k = k_ref[:, cols]
        v_aug = jnp.concatenate([v_ref[:, cols], ones_col], axis=1)
        c_aug = caug_ref[h]

        s = lax.dot_general(q, k, _NT, preferred_element_type=f32)
        sw = (s * w_intra).astype(bf16)
        num_aug = (jnp.dot(sw, v_aug, preferred_element_type=f32)
                   + w_inter * jnp.dot(q, c_aug.astype(bf16), preferred_element_type=f32))
        num = num_aug[:, :M_DV]
        den = num_aug[:, M_DV:M_DV + 1]
        den = jnp.maximum(jnp.abs(den), jnp.exp(-(b_col + g_col)))
        hh = num / den
        hh = hh * lax.rsqrt(jnp.mean(hh * hh, axis=-1, keepdims=True) + EPS)
        hh = hh * nh_ref[:, cols] * jax.nn.sigmoid(o_ref[:, cols].astype(f32))
        h_ref[:, cols] = hh.astype(h_ref.dtype)

        g_end = jnp.maximum(m0, jnp.max(a_row, axis=1, keepdims=True))
        b_end = jnp.sum(lf, axis=1, keepdims=True)
        ig_col = jnp.sum(jnp.where(diag, ig, 0.0), axis=1, keepdims=True)
        w_tok = jnp.exp(ig_col - b_col - g_end)
        w_state = jnp.exp(m0 - g_end)
        kw = (k.astype(f32) * w_tok).astype(bf16)
        caug_ref[h] = w_state * c_aug + lax.dot_general(kw, v_aug, _TN, preferred_element_type=f32)
        m_ref[h:h + 1, :] = jnp.broadcast_to(b_end + g_end, (1, LANES))

    @pl.when(blk == n_blk - 1)
    def _():
        c_out_ref[...] = caug_ref[:, :, :M_DV]
        n_out_ref[...] = caug_ref[:, :, M_DV:]
        m_out_ref[...] = m_ref[...]


def _mlstm(z, gates_t, gate_bias, norm_h, c0, n0, m0, *, n_streams):
    t = z.shape[0]
    s = t // n_streams
    L = _tile(s, MLSTM_BLOCK)
    nb = s // L
    gates3 = gates_t.reshape(N_GATE_ROWS, n_streams * nb, L).transpose(1, 0, 2)
    n0c = n0.reshape(n_streams, M_HEADS, M_DQK, 1)
    m0b = jnp.broadcast_to(m0.reshape(n_streams, M_HEADS, 1), (n_streams, M_HEADS, LANES))

    def zcols(group):
        return pl.BlockSpec((L, M_WIDTH), lambda b, c: (b * nb + c, group))

    def state(*tail):
        return pl.BlockSpec((None, M_HEADS) + tail, lambda b, c: (b, 0) + (0,) * len(tail))

    return pl.pallas_call(
        _mlstm_kernel,
        grid=(n_streams, nb),
        in_specs=[pl.BlockSpec((2 * M_HEADS, 1), lambda b, c: (0, 0)),
                  zcols(0), zcols(1), zcols(2), zcols(3),
                  pl.BlockSpec((None, N_GATE_ROWS, L), lambda b, c: (b * nb + c, 0, 0)),
                  pl.BlockSpec((1, M_WIDTH), lambda b, c: (0, 0)),
                  state(M_DQK, M_DV), state(M_DQK, 1), state(LANES)],
        out_specs=[pl.BlockSpec((L, M_WIDTH), lambda b, c: (b * nb + c, 0)),
                   state(M_DQK, M_DV), state(M_DQK, LANES), state(LANES)],
        out_shape=[jax.ShapeDtypeStruct((t, M_WIDTH), bf16),
                   jax.ShapeDtypeStruct((n_streams, M_HEADS, M_DQK, M_DV), f32),
                   jax.ShapeDtypeStruct((n_streams, M_HEADS, M_DQK, LANES), f32),
                   jax.ShapeDtypeStruct((n_streams, M_HEADS, LANES), f32)],
        scratch_shapes=[pltpu.VMEM((M_HEADS, M_DQK, M_DV + LANES), f32), pltpu.VMEM((M_HEADS, LANES), f32)],
        compiler_params=_params("parallel", "arbitrary"),
        name="mlstm",
    )(gate_bias, z, z, z, z, gates3, norm_h, c0, n0c, m0b)


def _fox_prompt_kernel(q_ref, k_ref, v_ref, cum_ref, o_ref,
                       sa_ref, sb_ref, pa_ref, pb_ref, xa_ref, xb_ref, m_ref, l_ref, acc_ref):
    i = pl.program_id(2)
    tq = o_ref.shape[0]
    tk = sa_ref.shape[1]
    assert tq == tk

    def q_rows(r):
        return q_ref[pl.ds(pl.multiple_of(r * tq, tq), tq), :]

    def k_rows(ref, j):
        return ref[pl.ds(pl.multiple_of(j * tk, tk), tk), :]

    def causal(s, j):
        ahead = (lax.broadcasted_iota(jnp.int32, (tq, tk), 1) - lax.broadcasted_iota(jnp.int32, (tq, tk), 0))
        return jnp.where(ahead <= i * tq - j * tk, s, -jnp.inf)

    def scores(j, s_ref, x_ref, masked, q_row=i):
        s = lax.dot_general(q_rows(q_row), k_rows(k_ref, j), _NT, preferred_element_type=f32) - cum_ref[j]
        if masked:
            s = causal(s, j)
        s_ref[...] = s
        x_ref[...] = jnp.broadcast_to(jnp.max(s, axis=1, keepdims=True), x_ref.shape)

    def values(j, p_ref):
        return jnp.dot(p_ref[...], k_rows(v_ref, j), preferred_element_type=f32)

    def softmax(s_ref, x_ref, p_ref, pv_prev, mask_block=None):
        s = s_ref[...]
        if mask_block is None:
            x = x_ref[...]
        else:
            s = causal(s, mask_block)
            x = jnp.max(s, axis=1, keepdims=True)
        m_old = m_ref[...]
        m_new = jnp.maximum(m_old, x)
        alpha = jnp.exp2(m_old - m_new)
        p = jnp.exp2(s - jnp.concatenate([m_new] * (tk // LANES), axis=1))
        p_ref[...] = p.astype(bf16)
        l_ref[...] = alpha * l_ref[...] + jnp.sum(p, axis=1, keepdims=True)
        acc_ref[...] = alpha * (acc_ref[...] + pv_prev)
        m_ref[...] = m_new

    m_ref[...] = jnp.full(m_ref.shape, -jnp.inf, f32)
    l_ref[...] = jnp.zeros(l_ref.shape, f32)
    acc_ref[...] = jnp.zeros(acc_ref.shape, f32)
    pb_ref[...] = jnp.zeros(pb_ref.shape, bf16)

    @pl.when(i == 0)
    def _():
        scores(0, sa_ref, xa_ref, False)

    def pair(j0):
        scores(j0 + 1, sb_ref, xb_ref, False)
        softmax(sa_ref, xa_ref, pa_ref, values(jnp.maximum(j0 - 1, 0), pb_ref))
        scores(j0 + 2, sa_ref, xa_ref, False)
        softmax(sb_ref, xb_ref, pb_ref, values(j0, pa_ref))

    def long_trip(jj, carry):
        for u in range(0, ATTN_TRIP, 2):
            pair(ATTN_TRIP * jj + u)
        return carry

    def short_trip(jj, carry):
        pair(ATTN_TRIP * (i // ATTN_TRIP) + 2 * jj)
        return carry

    lax.fori_loop(0, i // ATTN_TRIP, long_trip, 0)
    lax.fori_loop(0, (i % ATTN_TRIP) // 2, short_trip, 0)

    def scores_of_next_row():
        scores(0, sa_ref, xa_ref, False, q_row=jnp.minimum(i + 1, pl.num_programs(2) - 1))

    @pl.when(i % 2 == 0)
    def _():
        softmax(sa_ref, xa_ref, pa_ref, values(jnp.maximum(i - 1, 0), pb_ref), mask_block=i)
        scores_of_next_row()
        acc_ref[...] += values(i, pa_ref)

    @pl.when(i % 2 == 1)
    def _():
        scores(i, sb_ref, xb_ref, True)
        softmax(sa_ref, xa_ref, pa_ref, values(jnp.maximum(i - 2, 0), pb_ref))
        scores_of_next_row()
        softmax(sb_ref, xb_ref, pb_ref, values(i - 1, pa_ref))
        acc_ref[...] += values(i, pb_ref)

    o_ref[...] = (acc_ref[...] / l_ref[...]).astype(o_ref.dtype)


def _fox_prompt(z, k, v, cum_t, *, q_col0, n_streams):
    t = z.shape[0]
    s = t // n_streams
    tq = tk = _tile(s, ATTN_BLOCK)
    nq = nk = s // tk
    cum4 = cum_t.reshape(F_HEADS, n_streams * nk, 1, tk)
    q0 = q_col0 // F_HEAD_DIM
    return pl.pallas_call(
        _fox_prompt_kernel,
        grid=(n_streams, F_HEADS, nq),
        in_specs=[pl.BlockSpec((s, F_HEAD_DIM), lambda b, h, i: (b, q0 + h)),
                  pl.BlockSpec((s, F_HEAD_DIM), lambda b, h, i: (b, h)),
                  pl.BlockSpec((s, F_HEAD_DIM), lambda b, h, i: (b, h)),
                  pl.BlockSpec((None, nk, 1, tk), lambda b, h, i: (h, b, 0, 0))],
        out_specs=pl.BlockSpec((tq, F_HEAD_DIM), lambda b, h, i: (b * nq + i, h)),
        out_shape=jax.ShapeDtypeStruct((t, F_WIDTH), bf16),
        scratch_shapes=[pltpu.VMEM((tq, tk), f32)] * 2 + [pltpu.VMEM((tq, tk), bf16)] * 2
        + [pltpu.VMEM((tq, LANES), f32)] * 5,
        compiler_params=_params("parallel", "parallel", "arbitrary"),
        name="fox_prompt",
    )(z, k, v, cum4)


def _fox_decode_kernel(bias_ref, q_ref, kn_ref, vn_ref, kc_ref, vc_ref, cumc_ref, fg_ref, o_ref, logf_ref):
    head = pl.program_id(1)
    L = q_ref.shape[0]
    q = q_ref[...]
    logf = _log_sigmoid(fg_ref[...] + bias_ref[head])
    logf_ref[...] = logf
    cum_new = _cumsum_lanes(logf) * LOG2E
    cum_c = cumc_ref[...]
    cum_c = cum_c - cum_c[:, -1:]

    p_len = kc_ref.shape[0] // F_HEADS
    kc = kc_ref[pl.ds(head, p_len, stride=F_HEADS), :].astype(bf16)
    vc = vc_ref[pl.ds(head, p_len, stride=F_HEADS), :].astype(bf16)
    s_c = lax.dot_general(q, kc, _NT, preferred_element_type=f32) - cum_c
    s_n = lax.dot_general(q, kn_ref[...], _NT, preferred_element_type=f32) - cum_new
    row = lax.broadcasted_iota(jnp.int32, (L, L), 0)
    col = lax.broadcasted_iota(jnp.int32, (L, L), 1)
    s_n = jnp.where(col <= row, s_n, -jnp.inf)
    m = jnp.maximum(jnp.max(s_c, axis=1, keepdims=True), jnp.max(s_n, axis=1, keepdims=True))
    p_c = jnp.exp2(s_c - m)
    p_n = jnp.exp2(s_n - m)
    l = jnp.sum(p_c, axis=1, keepdims=True) + jnp.sum(p_n, axis=1, keepdims=True)
    acc = (jnp.dot(p_c.astype(bf16), vc, preferred_element_type=f32)
           + jnp.dot(p_n.astype(bf16), vn_ref[...], preferred_element_type=f32))
    o_ref[...] = (acc / l).astype(o_ref.dtype)


def _fox_decode(z, k, v, gates_t, bias, cache_k, cache_v, cum_cache, *, q_col0, n_streams):
    t = z.shape[0]
    L = t // n_streams
    p = cache_k.shape[1] // F_HEADS
    gates4 = gates_t.reshape(N_GATE_ROWS, n_streams, 1, L)
    q0 = q_col0 // F_HEAD_DIM
    return pl.pallas_call(
        _fox_decode_kernel,
        grid=(n_streams, F_HEADS),
        in_specs=[pl.BlockSpec(memory_space=pltpu.SMEM),
                  pl.BlockSpec((L, F_HEAD_DIM), lambda b, h: (b, q0 + h)),
                  pl.BlockSpec((L, F_HEAD_DIM), lambda b, h: (b, h)),
                  pl.BlockSpec((L, F_HEAD_DIM), lambda b, h: (b, h)),
                  pl.BlockSpec((None, p * F_HEADS, F_HEAD_DIM), lambda b, h: (b, 0, 0)),
                  pl.BlockSpec((None, p * F_HEADS, F_HEAD_DIM), lambda b, h: (b, 0, 0)),
                  pl.BlockSpec((None, 1, p), lambda b, h: (b * F_HEADS + h, 0, 0)),
                  pl.BlockSpec((None, None, 1, L), lambda b, h: (2 * M_HEADS + h, b, 0, 0))],
        out_specs=[pl.BlockSpec((L, F_HEAD_DIM), lambda b, h: (b, h)),
                   pl.BlockSpec((None, None, 1, L), lambda b, h: (b, h, 0, 0))],
        out_shape=[jax.ShapeDtypeStruct((t, F_WIDTH), bf16),
                   jax.ShapeDtypeStruct((n_streams, F_HEADS, 1, L), f32)],
        compiler_params=_params("parallel", "parallel"),
        name="fox_decode",
    )(bias, z, k, v, cache_k, cache_v, cum_cache, gates4)


def _merge_kernel(x_ref, ha_ref, hb_ref, ga_ref, gb_ref, wa_ref, wb_ref, wo_ref, g_ref, x1_ref, hn_ref):
    pa = jnp.dot(ha_ref[...], wa_ref[...], preferred_element_type=f32)
    pb = jnp.dot(hb_ref[...], wb_ref[...], preferred_element_type=f32)
    merged = (jax.nn.sigmoid(ga_ref[...].astype(f32)) * pa + jax.nn.sigmoid(gb_ref[...].astype(f32)) * pb)
    x1 = x_ref[...] + jnp.dot(merged.astype(bf16), wo_ref[...], preferred_element_type=f32)
    x1_ref[...] = x1
    hn = x1 * lax.rsqrt(jnp.mean(x1 * x1, axis=-1, keepdims=True) + EPS) * g_ref[...]
    hn_ref[...] = hn.astype(hn_ref.dtype)


def _merge(x, h_a, h_b, z, w_a, w_b, w_o, g, *, gate_col0):
    t, d = x.shape
    tm = _tile(t, 256)
    ga_blk = gate_col0 // d
    row = lambda i: (i, 0)
    fixed = lambda i: (0, 0)
    resident = functools.partial(pl.BlockSpec, index_map=fixed, pipeline_mode=pl.Buffered(1))
    return pl.pallas_call(
        _merge_kernel,
        grid=(t // tm,),
        in_specs=[pl.BlockSpec((tm, d), row),
                  pl.BlockSpec((tm, M_WIDTH), row),
                  pl.BlockSpec((tm, F_WIDTH), row),
                  pl.BlockSpec((tm, d), lambda i: (i, ga_blk)),
                  pl.BlockSpec((tm, d), lambda i: (i, ga_blk + 1)),
                  resident(w_a.shape), resident(w_b.shape), resident(w_o.shape),
                  pl.BlockSpec((1, d), fixed)],
        out_specs=[pl.BlockSpec((tm, d), row), pl.BlockSpec((tm, d), row)],
        out_shape=[jax.ShapeDtypeStruct((t, d), f32), jax.ShapeDtypeStruct((t, d), bf16)],
        compiler_params=_params("parallel"),
        name="merge",
    )(x, h_a, h_b, z, z, w_a, w_b, w_o, g)


def _ffn_kernel(hn_ref, x1_ref, wu_ref, wd_ref, g_ref, y_ref):
    f = pl.program_id(1)

    @pl.when(f == 0)
    def _():
        y_ref[...] = x1_ref[...]

    u = jnp.maximum(jnp.dot(hn_ref[...], wu_ref[...], preferred_element_type=f32), 0.0)
    y_ref[...] += jnp.dot((u * u).astype(bf16), wd_ref[...], preferred_element_type=f32)

    @pl.when(f == pl.num_programs(1) - 1)
    def _():
        x2 = y_ref[...]
        y_ref[...] = x2 * lax.rsqrt(jnp.mean(x2 * x2, axis=-1, keepdims=True) + EPS) * g_ref[...]


def _ffn(hn, x1, w_up, w_down, g):
    t, d = x1.shape
    dff = w_up.shape[1]
    tm = _tile(t, 512)
    tf = _tile(dff, 1024)
    return pl.pallas_call(
        _ffn_kernel,
        grid=(t // tm, dff // tf),
        in_specs=[pl.BlockSpec((tm, d), lambda i, f: (i, 0)),
                  pl.BlockSpec((tm, d), lambda i, f: (i, 0)),
                  pl.BlockSpec((d, tf), lambda i, f: (0, f)),
                  pl.BlockSpec((tf, d), lambda i, f: (f, 0)),
                  pl.BlockSpec((1, d), lambda i, f: (0, 0))],
        out_specs=pl.BlockSpec((tm, d), lambda i, f: (i, 0)),
        out_shape=jax.ShapeDtypeStruct((t, d), f32),
        compiler_params=_params("parallel", "arbitrary"),
        name="ffn",
    )(hn, x1, w_up, w_down, g)


def _layer(x3, w, mstate, fox_past):
    nstr, frames, d = x3.shape
    t = nstr * frames
    x = x3.reshape(t, d)
    xn, gates_t = _norm_gates(x, w["norm_mix"], w["w_gates_t"])
    k32, k16 = _matmul_heads(xn, w["w_fk"], "proj_k")
    v32, v16 = _matmul_heads(xn, w["w_fv"], "proj_v")
    (z,) = _matmul(xn, w["w_rest"], (bf16,), "proj_rest")

    c0, n0, m0 = mstate
    h_a, c_new, n_slab, m_slab = _mlstm(z, gates_t, w["mlstm_bias"], w["norm_mlstm_h"], c0, n0, m0,
                                        n_streams=nstr)
    n_new = n_slab[..., 0]
    m_new = m_slab[:, :, 0]

    if fox_past is None:
        logf_t, cum_t = _scan_rows(gates_t, w["b_fox_f_col"], row_block=1, rows=F_HEADS, n_streams=nstr,
                                   apply_log_sigmoid=True)
        h_b = _fox_prompt(z, k16, v16, cum_t, q_col0=w["q_col0"], n_streams=nstr)
        logf = logf_t.T.reshape(nstr, frames, F_HEADS)
    else:
        ck, cv, clf = fox_past
        p = ck.shape[1]
        clf_t = jnp.transpose(clf, (0, 2, 1)).reshape(nstr * F_HEADS, p)
        _, cum_c = _scan_rows(clf_t, jnp.zeros((nstr * F_HEADS, 1), f32), row_block=0, rows=nstr * F_HEADS,
                              n_streams=1, apply_log_sigmoid=False)
        h_b, logf4 = _fox_decode(z, k16, v16, gates_t, w["b_fox_f"], ck.reshape(nstr, p * F_HEADS, F_HEAD_DIM),
                                 cv.reshape(nstr, p * F_HEADS, F_HEAD_DIM), cum_c.reshape(nstr * F_HEADS, 1, p),
                                 q_col0=w["q_col0"], n_streams=nstr)
        logf = jnp.transpose(logf4[:, :, 0, :], (0, 2, 1))

    x1, hn = _merge(x, h_a, h_b, z, w["w_branch_a"], w["w_branch_b"], w["w_out"], w["norm_ffn"],
                    gate_col0=w["gate_col0"])
    y = _ffn(hn, x1, w["w_up"], w["w_down"], w["norm_final"])

    k_rows = k32.reshape(nstr, frames, F_HEADS, F_HEAD_DIM)
    v_rows = v32.reshape(nstr, frames, F_HEADS, F_HEAD_DIM)
    return y.reshape(nstr, frames, d), (k_rows, v_rows, logf, c_new, n_new, m_new)


def _regroup_kernel(w_ref, fk_ref, fv_ref, rest_ref, gates_ref, *, offs):
    mq, mk, mv, mo, mi, mf, fq, fk, fv, ff, ga, gb = range(12)

    def cols(seg):
        start, stop = offs[seg], offs[seg + 1]
        lo = start // LANES * LANES
        hi = min(-(-stop // LANES) * LANES, w_ref.shape[1])
        return w_ref[:, lo:hi][:, start - lo:stop - lo]

    fk_ref[...] = cols(fk).astype(bf16)
    fv_ref[...] = cols(fv).astype(bf16)
    pieces = [cols(mq), cols(mk) * (M_DQK ** -0.5), cols(mv), cols(mo), cols(ga), cols(gb),
              cols(fq) * (F_HEAD_DIM ** -0.5 * LOG2E)]
    at = 0
    for piece in pieces:
        rest_ref[:, at:at + piece.shape[1]] = piece.astype(bf16)
        at += piece.shape[1]
    gates_ref[...] = jnp.zeros(gates_ref.shape, bf16)
    at = 0
    for seg in (mi, mf, ff):
        piece = cols(seg)
        gates_ref[:, at:at + piece.shape[1]] = piece.astype(bf16)
        at += piece.shape[1]


def _regroup_w_in(w_in, offs):
    d, n = w_in.shape
    tr = _tile(d, 256)
    n_rest = 4 * M_WIDTH + 2 * d + F_WIDTH
    widths = (F_WIDTH, F_WIDTH, n_rest, LANES)
    return pl.pallas_call(
        functools.partial(_regroup_kernel, offs=offs),
        grid=(d // tr,),
        in_specs=[pl.BlockSpec((tr, n), lambda i: (i, 0))],
        out_specs=[pl.BlockSpec((tr, wd), lambda i: (i, 0)) for wd in widths],
        out_shape=[jax.ShapeDtypeStruct((d, wd), bf16) for wd in widths],
        compiler_params=_params("parallel"),
        name="regroup_w_in",
    )(w_in)


def _prepare_weights(norm_mix, w_in, b_mlstm_i, b_mlstm_f, b_fox_f, norm_mlstm_h, w_branch_a, w_branch_b,
                     w_out, norm_ffn, w_up, w_down, norm_final):
    d = w_in.shape[0]
    sizes = (M_HEADS * M_DQK, M_HEADS * M_DQK, M_WIDTH, M_WIDTH, M_HEADS, M_HEADS,
             F_WIDTH, F_WIDTH, F_WIDTH, F_HEADS, d, d)
    offs = [0]
    for s in sizes:
        offs.append(offs[-1] + s)
    w_fk, w_fv, w_rest, w_gates = _regroup_w_in(w_in, tuple(offs))
    return {
        "norm_mix": norm_mix.reshape(1, d),
        "w_gates_t": w_gates[:, :N_GATE_ROWS].T,
        "w_fk": w_fk,
        "w_fv": w_fv,
        "w_rest": w_rest,
        "gate_col0": 4 * M_WIDTH,
        "q_col0": 4 * M_WIDTH + 2 * d,
        "mlstm_bias": jnp.concatenate([b_mlstm_i, b_mlstm_f]).astype(f32).reshape(2 * M_HEADS, 1),
        "b_fox_f": b_fox_f.astype(f32),
        "b_fox_f_col": b_fox_f.astype(f32).reshape(F_HEADS, 1),
        "norm_mlstm_h": norm_mlstm_h.reshape(1, M_WIDTH),
        "w_branch_a": w_branch_a.astype(bf16),
        "w_branch_b": w_branch_b.astype(bf16),
        "w_out": w_out.astype(bf16),
        "norm_ffn": norm_ffn.reshape(1, d),
        "w_up": w_up.astype(bf16),
        "w_down": w_down.astype(bf16),
        "norm_final": norm_final.reshape(1, d),
    }


def kernel(x_prompt, x_sample, cache_fox_k, cache_fox_v, cache_fox_logf, state_mlstm_c, state_mlstm_n, state_mlstm_m, norm_mix, w_in, b_mlstm_i, b_mlstm_f, b_fox_f, norm_mlstm_h, w_branch_a, w_branch_b, w_out, norm_ffn, w_up, w_down, norm_final):
    depth = w_in.shape[0]
    assert depth == 1, "the final norm is fused into the layer's FFN kernel"
    w = _prepare_weights(norm_mix[0], w_in[0], b_mlstm_i[0], b_mlstm_f[0], b_fox_f[0], norm_mlstm_h[0],
                         w_branch_a[0], w_branch_b[0], w_out[0], norm_ffn[0], w_up[0], w_down[0], norm_final)
    bp = x_prompt.shape[0]
    fresh = (jnp.zeros((bp, M_HEADS, M_DQK, M_DV), f32), jnp.zeros((bp, M_HEADS, M_DQK), f32),
             jnp.zeros((bp, M_HEADS), f32))
    y_p, st_p = _layer(x_prompt, w, fresh, None)
    y_s, st_s = _layer(x_sample, w, (state_mlstm_c[0], state_mlstm_n[0], state_mlstm_m[0]),
                       (cache_fox_k[0], cache_fox_v[0], cache_fox_logf[0]))
    return (y_p, y_s) + tuple(a[None] for a in st_p) + tuple(a[None] for a in st_s)
```

```python
import functools

import jax
import jax.numpy as jnp
from jax import lax
from jax.experimental import pallas as pl
from jax.experimental.pallas import tpu as pltpu

M_HEADS = 4
M_DQK = 256
M_DV = 256
M_WIDTH = M_HEADS * M_DV
F_HEADS = 8
F_HEAD_DIM = 128
F_WIDTH = F_HEADS * F_HEAD_DIM
EPS = 1e-6
N_GATE_ROWS = 2 * M_HEADS + F_HEADS
LANES = 128
assert F_HEAD_DIM == LANES
MLSTM_BLOCK = 256
ATTN_BLOCK = 512
ATTN_TRIP = 4
LOG2E = 1.4426950408889634
VMEM_LIMIT_BYTES = 56 * 1024 * 1024

f32 = jnp.float32
bf16 = jnp.bfloat16

_NT = (((1,), (1,)), ((), ()))
_TN = (((0,), (0,)), ((), ()))


def _tile(n, pref):
    t = min(n, pref)
    while n % t:
        t //= 2
    return t


def _params(*sem):
    return pltpu.CompilerParams(dimension_semantics=sem, vmem_limit_bytes=VMEM_LIMIT_BYTES)


def _log_sigmoid(z):
    return jnp.minimum(z, 0.0) - jnp.log1p(jnp.exp(-jnp.abs(z)))


def _cumsum_lanes(x):
    rows, n = x.shape
    pad = -rows % 16
    if rows == 1:
        x16 = jnp.broadcast_to(x, (16, n))
    elif pad == 0:
        x16 = x
    else:
        x16 = jnp.concatenate([x, jnp.zeros((pad, n), f32)], axis=0)
    r = lax.broadcasted_iota(jnp.int32, (n, n), 0)
    c = lax.broadcasted_iota(jnp.int32, (n, n), 1)
    u = jnp.where(r <= c, 1.0, 0.0).astype(bf16)
    hi = x16.astype(bf16)
    rem = x16 - hi.astype(f32)
    mid = rem.astype(bf16)
    lo = (rem - mid.astype(f32)).astype(bf16)
    out = (jnp.dot(hi, u, preferred_element_type=f32) + jnp.dot(mid, u, preferred_element_type=f32)
           + jnp.dot(lo, u, preferred_element_type=f32))
    return out[:rows]


def _norm_kernel(x_ref, g_ref, wm_ref, wf_ref, xn_ref, gt_ref):
    x = x_ref[...]
    y = x * lax.rsqrt(jnp.mean(x * x, axis=-1, keepdims=True) + EPS) * g_ref[...]
    xn = y.astype(bf16)
    xn_ref[...] = xn
    wg = jnp.concatenate([wm_ref[...], wf_ref[...]], axis=0).astype(bf16)
    gt_ref[...] = lax.dot_general(wg, xn, _NT, preferred_element_type=f32)


def _norm_gates(x, g, w_t, mlstm_gate_row, fox_gate_row):
    t, d = x.shape
    tm = _tile(t, 512)
    assert 2 * M_HEADS == F_HEADS and mlstm_gate_row % F_HEADS == 0 and fox_gate_row % F_HEADS == 0
    return pl.pallas_call(
        _norm_kernel,
        grid=(t // tm,),
        in_specs=[pl.BlockSpec((tm, d), lambda i: (i, 0)),
                  pl.BlockSpec((1, d), lambda i: (0, 0)),
                  pl.BlockSpec((F_HEADS, d), lambda i: (mlstm_gate_row // F_HEADS, 0)),
                  pl.BlockSpec((F_HEADS, d), lambda i: (fox_gate_row // F_HEADS, 0))],
        out_specs=[pl.BlockSpec((tm, d), lambda i: (i, 0)),
                   pl.BlockSpec((N_GATE_ROWS, tm), lambda i: (0, i))],
        out_shape=[jax.ShapeDtypeStruct((t, d), bf16), jax.ShapeDtypeStruct((N_GATE_ROWS, t), f32)],
        compiler_params=_params("parallel"),
        name="norm_gates",
    )(x, g, w_t, w_t)


def _mm_kernel(a_ref, w_ref, o_ref):
    o_ref[...] = lax.dot_general(a_ref[...], w_ref[...], _NT, preferred_element_type=f32).astype(o_ref.dtype)


def _matmul_t(a, w_t, row0, n, name):
    t, k = a.shape
    tm = _tile(t, 1024)
    tn = _tile(n, 1024)
    assert row0 % tn == 0
    return pl.pallas_call(
        _mm_kernel,
        grid=(t // tm, n // tn),
        in_specs=[pl.BlockSpec((tm, k), lambda i, j: (i, 0)),
                  pl.BlockSpec((tn, k), lambda i, j: (row0 // tn + j, 0))],
        out_specs=pl.BlockSpec((tm, tn), lambda i, j: (i, j)),
        out_shape=jax.ShapeDtypeStruct((t, n), bf16),
        compiler_params=_params("parallel", "arbitrary"),
        name=name,
    )(a, w_t)


def _mm_heads_kernel(a_ref, w_ref, rows_ref, o16_ref):
    r = lax.dot_general(a_ref[...], w_ref[...], _NT, preferred_element_type=f32)
    o16_ref[...] = r.astype(bf16)
    tm = a_ref.shape[0]
    for h in range(F_HEADS):
        rows_ref[pl.ds(h, tm, stride=F_HEADS), :] = r[:, h * F_HEAD_DIM:(h + 1) * F_HEAD_DIM]


def _matmul_heads_t(a, w_t, row0, name):
    t, k = a.shape
    assert row0 % F_WIDTH == 0
    tm = _tile(t, 1024)
    return pl.pallas_call(
        _mm_heads_kernel,
        grid=(t // tm,),
        in_specs=[pl.BlockSpec((tm, k), lambda i: (i, 0)),
                  pl.BlockSpec((F_WIDTH, k), lambda i: (row0 // F_WIDTH, 0))],
        out_specs=[pl.BlockSpec((tm * F_HEADS, F_HEAD_DIM), lambda i: (i, 0)),
                   pl.BlockSpec((tm, F_WIDTH), lambda i: (i, 0))],
        out_shape=[jax.ShapeDtypeStruct((t * F_HEADS, F_HEAD_DIM), f32), jax.ShapeDtypeStruct((t, F_WIDTH), bf16)],
        compiler_params=_params("parallel"),
        name=name,
    )(a, w_t)


def _scan_kernel(x_ref, bias_ref, logf_ref, cum_ref, carry_ref, *, apply_log_sigmoid):
    @pl.when(pl.program_id(1) == 0)
    def _():
        carry_ref[...] = jnp.zeros_like(carry_ref)

    x = x_ref[...]
    if apply_log_sigmoid:
        x = _log_sigmoid(x + bias_ref[...])
    logf_ref[...] = x
    cum = _cumsum_lanes(x) + carry_ref[:, :1]
    cum_ref[...] = cum * LOG2E
    carry_ref[...] = jnp.broadcast_to(cum[:, -1:], carry_ref.shape)


def _scan_rows(x, bias, *, row_block, rows, n_streams, apply_log_sigmoid):
    total = x.shape[1]
    s = total // n_streams
    tb = _tile(s, 512)
    nb = s // tb
    spec = pl.BlockSpec((rows, tb), lambda b, j: (0, b * nb + j))
    return pl.pallas_call(
        functools.partial(_scan_kernel, apply_log_sigmoid=apply_log_sigmoid),
        grid=(n_streams, nb),
        in_specs=[pl.BlockSpec((rows, tb), lambda b, j: (row_block, b * nb + j)),
                  pl.BlockSpec((rows, 1), lambda b, j: (0, 0))],
        out_specs=[spec, spec],
        out_shape=[jax.ShapeDtypeStruct((rows, total), f32)] * 2,
        scratch_shapes=[pltpu.VMEM((rows, LANES), f32)],
        compiler_params=_params("arbitrary", "arbitrary"),
        name="logf_scan",
    )(x, bias)


def _mlstm_kernel(bias_ref, q_ref, k_ref, v_ref, o_ref, gates_ref, nh_ref, c0_ref, n0_ref, m0_ref,
                  h_ref, c_out_ref, n_out_ref, m_out_ref, caug_ref, m_ref):
    blk = pl.program_id(1)
    n_blk = pl.num_programs(1)
    L = q_ref.shape[0]

    @pl.when(blk == 0)
    def _():
        lane = lax.broadcasted_iota(jnp.int32, (M_HEADS, M_DQK, LANES), 2)
        caug_ref[:, :, :M_DV] = c0_ref[...]
        caug_ref[:, :, M_DV:] = jnp.where(lane == 0, n0_ref[...], 0.0)
        m_ref[...] = m0_ref[...]

    gates = gates_ref[...]
    ig_all = gates[:M_HEADS] + bias_ref[:M_HEADS]
    lf_all = _log_sigmoid(gates[M_HEADS:2 * M_HEADS] + bias_ref[M_HEADS:])
    a_all = ig_all - _cumsum_lanes(lf_all)

    t_idx = lax.broadcasted_iota(jnp.int32, (L, L), 0)
    s_idx = lax.broadcasted_iota(jnp.int32, (L, L), 1)
    causal = s_idx <= t_idx
    diag = s_idx == t_idx
    ones_col = jnp.where(lax.broadcasted_iota(jnp.int32, (L, LANES), 1) == 0, 1.0, 0.0).astype(bf16)

    for h in range(M_HEADS):
        cols = slice(h * M_DV, (h + 1) * M_DV)
        m0 = m_ref[h:h + 1, :1]
        ig = ig_all[h:h + 1]
        lf = lf_all[h:h + 1]
        a_row = a_all[h:h + 1]
        a_mat = jnp.where(causal, a_row, -jnp.inf)
        g_col = jnp.maximum(m0, jnp.max(a_mat, axis=1, keepdims=True))
        b_col = jnp.sum(jnp.where(causal, lf, 0.0), axis=1, keepdims=True)
        w_intra = jnp.exp(a_mat - g_col)
        w_inter = jnp.exp(m0 - g_col)

        q = q_ref[:, cols]
        k = k_ref[:, cols]
        v_aug = jnp.concatenate([v_ref[:, cols], ones_col], axis=1)
        c_aug = caug_ref[h]

        s = lax.dot_general(q, k, _NT, preferred_element_type=f32)
        sw = (s * w_intra).astype(bf16)
        num_aug = (jnp.dot(sw, v_aug, preferred_element_type=f32)
                   + w_inter * jnp.dot(q, c_aug.astype(bf16), preferred_element_type=f32))
        num = num_aug[:, :M_DV]
        den = num_aug[:, M_DV:M_DV + 1]
        den = jnp.maximum(jnp.abs(den), jnp.exp(-(b_col + g_col)))
        hh = num / den
        hh = hh * lax.rsqrt(jnp.mean(hh * hh, axis=-1, keepdims=True) + EPS)
        hh = hh * nh_ref[:, cols] * jax.nn.sigmoid(o_ref[:, cols].astype(f32))
        h_ref[:, cols] = hh.astype(h_ref.dtype)

        g_end = jnp.maximum(m0, jnp.max(a_row, axis=1, keepdims=True))
        b_end = jnp.sum(lf, axis=1, keepdims=True)
        ig_col = jnp.sum(jnp.where(diag, ig, 0.0), axis=1, keepdims=True)
        w_tok = jnp.exp(ig_col - b_col - g_end)
        w_state = jnp.exp(m0 - g_end)
        kw = (k.astype(f32) * w_tok).astype(bf16)
        caug_ref[h] = w_state * c_aug + lax.dot_general(kw, v_aug, _TN, preferred_element_type=f32)
        m_ref[h:h + 1, :] = jnp.broadcast_to(b_end + g_end, (1, LANES))

    @pl.when(blk == n_blk - 1)
    def _():
        c_out_ref[...] = caug_ref[:, :, :M_DV]
        n_out_ref[...] = caug_ref[:, :, M_DV:]
        m_out_ref[...] = m_ref[...]


def _mlstm(z, gates_t, gate_bias, norm_h, c0, n0, m0, *, n_streams):
    t = z.shape[0]
    s = t // n_streams
    L = _tile(s, MLSTM_BLOCK)
    nb = s // L
    gates3 = gates_t.reshape(N_GATE_ROWS, n_streams * nb, L).transpose(1, 0, 2)
    n0c = n0.reshape(n_streams, M_HEADS, M_DQK, 1)
    m0b = jnp.broadcast_to(m0.reshape(n_streams, M_HEADS, 1), (n_streams, M_HEADS, LANES))

    def zcols(group):
        return pl.BlockSpec((L, M_WIDTH), lambda b, c: (b * nb + c, group))

    def state(*tail):
        return pl.BlockSpec((None, M_HEADS) + tail, lambda b, c: (b, 0) + (0,) * len(tail))

    return pl.pallas_call(
        _mlstm_kernel,
        grid=(n_streams, nb),
        in_specs=[pl.BlockSpec((2 * M_HEADS, 1), lambda b, c: (0, 0)),
                  zcols(0), zcols(1), zcols(2), zcols(3),
                  pl.BlockSpec((None, N_GATE_ROWS, L), lambda b, c: (b * nb + c, 0, 0)),
                  pl.BlockSpec((1, M_WIDTH), lambda b, c: (0, 0)),
                  state(M_DQK, M_DV), state(M_DQK, 1), state(LANES)],
        out_specs=[pl.BlockSpec((L, M_WIDTH), lambda b, c: (b * nb + c, 0)),
                   state(M_DQK, M_DV), state(M_DQK, LANES), state(LANES)],
        out_shape=[jax.ShapeDtypeStruct((t, M_WIDTH), bf16),
                   jax.ShapeDtypeStruct((n_streams, M_HEADS, M_DQK, M_DV), f32),
                   jax.ShapeDtypeStruct((n_streams, M_HEADS, M_DQK, LANES), f32),
                   jax.ShapeDtypeStruct((n_streams, M_HEADS, LANES), f32)],
        scratch_shapes=[pltpu.VMEM((M_HEADS, M_DQK, M_DV + LANES), f32), pltpu.VMEM((M_HEADS, LANES), f32)],
        compiler_params=_params("parallel", "arbitrary"),
        name="mlstm",
    )(gate_bias, z, z, z, z, gates3, norm_h, c0, n0c, m0b)


def _fox_prompt_kernel(q_ref, k_ref, v_ref, cum_ref, o_ref,
                       sa_ref, sb_ref, pa_ref, pb_ref, xa_ref, xb_ref, m_ref, l_ref, acc_ref):
    i = pl.program_id(2)
    tq = o_ref.shape[0]
    tk = sa_ref.shape[1]
    assert tq == tk

    def q_rows(r):
        return q_ref[pl.ds(pl.multiple_of(r * tq, tq), tq), :]

    def k_rows(ref, j):
        return ref[pl.ds(pl.multiple_of(j * tk, tk), tk), :]

    def causal(s, j):
        ahead = (lax.broadcasted_iota(jnp.int32, (tq, tk), 1) - lax.broadcasted_iota(jnp.int32, (tq, tk), 0))
        return jnp.where(ahead <= i * tq - j * tk, s, -jnp.inf)

    def scores(j, s_ref, x_ref, masked, q_row=i):
        s = lax.dot_general(q_rows(q_row), k_rows(k_ref, j), _NT, preferred_element_type=f32) - cum_ref[j]
        if masked:
            s = causal(s, j)
        s_ref[...] = s
        x_ref[...] = jnp.broadcast_to(jnp.max(s, axis=1, keepdims=True), x_ref.shape)

    def values(j, p_ref):
        return jnp.dot(p_ref[...], k_rows(v_ref, j), preferred_element_type=f32)

    def softmax(s_ref, x_ref, p_ref, pv_prev, mask_block=None):
        s = s_ref[...]
        if mask_block is None:
            x = x_ref[...]
        else:
            s = causal(s, mask_block)
            x = jnp.max(s, axis=1, keepdims=True)
        m_old = m_ref[...]
        m_new = jnp.maximum(m_old, x)
        alpha = jnp.exp2(m_old - m_new)
        p = jnp.exp2(s - jnp.concatenate([m_new] * (tk // LANES), axis=1))
        p_ref[...] = p.astype(bf16)
        l_ref[...] = alpha * l_ref[...] + jnp.sum(p, axis=1, keepdims=True)
        acc_ref[...] = alpha * (acc_ref[...] + pv_prev)
        m_ref[...] = m_new

    m_ref[...] = jnp.full(m_ref.shape, -jnp.inf, f32)
    l_ref[...] = jnp.zeros(l_ref.shape, f32)
    acc_ref[...] = jnp.zeros(acc_ref.shape, f32)
    pb_ref[...] = jnp.zeros(pb_ref.shape, bf16)

    @pl.when(i == 0)
    def _():
        scores(0, sa_ref, xa_ref, False)

    def pair(j0):
        scores(j0 + 1, sb_ref, xb_ref, False)
        softmax(sa_ref, xa_ref, pa_ref, values(jnp.maximum(j0 - 1, 0), pb_ref))
        scores(j0 + 2, sa_ref, xa_ref, False)
        softmax(sb_ref, xb_ref, pb_ref, values(j0, pa_ref))

    def long_trip(jj, carry):
        for u in range(0, ATTN_TRIP, 2):
            pair(ATTN_TRIP * jj + u)
        return carry

    def short_trip(jj, carry):
        pair(ATTN_TRIP * (i // ATTN_TRIP) + 2 * jj)
        return carry

    lax.fori_loop(0, i // ATTN_TRIP, long_trip, 0)
    lax.fori_loop(0, (i % ATTN_TRIP) // 2, short_trip, 0)

    def scores_of_next_row():
        scores(0, sa_ref, xa_ref, False, q_row=jnp.minimum(i + 1, pl.num_programs(2) - 1))

    @pl.when(i % 2 == 0)
    def _():
        softmax(sa_ref, xa_ref, pa_ref, values(jnp.maximum(i - 1, 0), pb_ref), mask_block=i)
        scores_of_next_row()
        acc_ref[...] += values(i, pa_ref)

    @pl.when(i % 2 == 1)
    def _():
        scores(i, sb_ref, xb_ref, True)
        softmax(sa_ref, xa_ref, pa_ref, values(jnp.maximum(i - 2, 0), pb_ref))
        scores_of_next_row()
        softmax(sb_ref, xb_ref, pb_ref, values(i - 1, pa_ref))
        acc_ref[...] += values(i, pb_ref)

    o_ref[...] = (acc_ref[...] / l_ref[...]).astype(o_ref.dtype)


def _fox_prompt(z, k, v, cum_t, *, q_col0, n_streams):
    t = z.shape[0]
    s = t // n_streams
    tq = tk = _tile(s, ATTN_BLOCK)
    nq = nk = s // tk
    cum4 = cum_t.reshape(F_HEADS, n_streams * nk, 1, tk)
    q0 = q_col0 // F_HEAD_DIM
    return pl.pallas_call(
        _fox_prompt_kernel,
        grid=(n_streams, F_HEADS, nq),
        in_specs=[pl.BlockSpec((s, F_HEAD_DIM), lambda b, h, i: (b, q0 + h)),
                  pl.BlockSpec((s, F_HEAD_DIM), lambda b, h, i: (b, h)),
                  pl.BlockSpec((s, F_HEAD_DIM), lambda b, h, i: (b, h)),
                  pl.BlockSpec((None, nk, 1, tk), lambda b, h, i: (h, b, 0, 0))],
        out_specs=pl.BlockSpec((tq, F_HEAD_DIM), lambda b, h, i: (b * nq + i, h)),
        out_shape=jax.ShapeDtypeStruct((t, F_WIDTH), bf16),
        scratch_shapes=[pltpu.VMEM((tq, tk), f32)] * 2 + [pltpu.VMEM((tq, tk), bf16)] * 2
        + [pltpu.VMEM((tq, LANES), f32)] * 5,
        compiler_params=_params("parallel", "parallel", "arbitrary"),
        name="fox_prompt",
    )(z, k, v, cum4)


def _fox_decode_kernel(bias_ref, q_ref, kn_ref, vn_ref, kc_ref, vc_ref, cumc_ref, fg_ref, o_ref, logf_ref):
    head = pl.program_id(1)
    L = q_ref.shape[0]
    q = q_ref[...]
    logf = _log_sigmoid(fg_ref[...] + bias_ref[head])
    logf_ref[...] = logf
    cum_new = _cumsum_lanes(logf) * LOG2E
    cum_c = cumc_ref[...]
    cum_c = cum_c - cum_c[:, -1:]

    p_len = kc_ref.shape[0] // F_HEADS
    kc = kc_ref[pl.ds(head, p_len, stride=F_HEADS), :].astype(bf16)
    vc = vc_ref[pl.ds(head, p_len, stride=F_HEADS), :].astype(bf16)
    s_c = lax.dot_general(q, kc, _NT, preferred_element_type=f32) - cum_c
    s_n = lax.dot_general(q, kn_ref[...], _NT, preferred_element_type=f32) - cum_new
    row = lax.broadcasted_iota(jnp.int32, (L, L), 0)
    col = lax.broadcasted_iota(jnp.int32, (L, L), 1)
    s_n = jnp.where(col <= row, s_n, -jnp.inf)
    m = jnp.maximum(jnp.max(s_c, axis=1, keepdims=True), jnp.max(s_n, axis=1, keepdims=True))
    p_c = jnp.exp2(s_c - m)
    p_n = jnp.exp2(s_n - m)
    l = jnp.sum(p_c, axis=1, keepdims=True) + jnp.sum(p_n, axis=1, keepdims=True)
    acc = (jnp.dot(p_c.astype(bf16), vc, preferred_element_type=f32)
           + jnp.dot(p_n.astype(bf16), vn_ref[...], preferred_element_type=f32))
    o_ref[...] = (acc / l).astype(o_ref.dtype)


def _fox_decode(z, k, v, gates_t, bias, cache_k, cache_v, cum_cache, *, q_col0, n_streams):
    t = z.shape[0]
    L = t // n_streams
    p = cache_k.shape[1] // F_HEADS
    gates4 = gates_t.reshape(N_GATE_ROWS, n_streams, 1, L)
    q0 = q_col0 // F_HEAD_DIM
    return pl.pallas_call(
        _fox_decode_kernel,
        grid=(n_streams, F_HEADS),
        in_specs=[pl.BlockSpec(memory_space=pltpu.SMEM),
                  pl.BlockSpec((L, F_HEAD_DIM), lambda b, h: (b, q0 + h)),
                  pl.BlockSpec((L, F_HEAD_DIM), lambda b, h: (b, h)),
                  pl.BlockSpec((L, F_HEAD_DIM), lambda b, h: (b, h)),
                  pl.BlockSpec((None, p * F_HEADS, F_HEAD_DIM), lambda b, h: (b, 0, 0)),
                  pl.BlockSpec((None, p * F_HEADS, F_HEAD_DIM), lambda b, h: (b, 0, 0)),
                  pl.BlockSpec((None, 1, p), lambda b, h: (b * F_HEADS + h, 0, 0)),
                  pl.BlockSpec((None, None, 1, L), lambda b, h: (2 * M_HEADS + h, b, 0, 0))],
        out_specs=[pl.BlockSpec((L, F_HEAD_DIM), lambda b, h: (b, h)),
                   pl.BlockSpec((None, None, 1, L), lambda b, h: (b, h, 0, 0))],
        out_shape=[jax.ShapeDtypeStruct((t, F_WIDTH), bf16),
                   jax.ShapeDtypeStruct((n_streams, F_HEADS, 1, L), f32)],
        compiler_params=_params("parallel", "parallel"),
        name="fox_decode",
    )(bias, z, k, v, cache_k, cache_v, cum_cache, gates4)


def _merge_kernel(x_ref, ha_ref, hb_ref, ga_ref, gb_ref, wa_ref, wb_ref, wo_ref, g_ref, x1_ref, hn_ref):
    pa = jnp.dot(ha_ref[...], wa_ref[...], preferred_element_type=f32)
    pb = jnp.dot(hb_ref[...], wb_ref[...], preferred_element_type=f32)
    merged = (jax.nn.sigmoid(ga_ref[...].astype(f32)) * pa + jax.nn.sigmoid(gb_ref[...].astype(f32)) * pb)
    x1 = x_ref[...] + jnp.dot(merged.astype(bf16), wo_ref[...], preferred_element_type=f32)
    x1_ref[...] = x1
    hn = x1 * lax.rsqrt(jnp.mean(x1 * x1, axis=-1, keepdims=True) + EPS) * g_ref[...]
    hn_ref[...] = hn.astype(hn_ref.dtype)


def _merge(x, h_a, h_b, z, w_a, w_b, w_o, g, *, gate_col0):
    t, d = x.shape
    tm = _tile(t, 256)
    ga_blk = gate_col0 // d
    row = lambda i: (i, 0)
    fixed = lambda i: (0, 0)
    resident = functools.partial(pl.BlockSpec, index_map=fixed, pipeline_mode=pl.Buffered(1))
    return pl.pallas_call(
        _merge_kernel,
        grid=(t // tm,),
        in_specs=[pl.BlockSpec((tm, d), row),
                  pl.BlockSpec((tm, M_WIDTH), row),
                  pl.BlockSpec((tm, F_WIDTH), row),
                  pl.BlockSpec((tm, d), lambda i: (i, ga_blk)),
                  pl.BlockSpec((tm, d), lambda i: (i, ga_blk + 1)),
                  resident(w_a.shape), resident(w_b.shape), resident(w_o.shape),
                  pl.BlockSpec((1, d), fixed)],
        out_specs=[pl.BlockSpec((tm, d), row), pl.BlockSpec((tm, d), row)],
        out_shape=[jax.ShapeDtypeStruct((t, d), f32), jax.ShapeDtypeStruct((t, d), bf16)],
        compiler_params=_params("parallel"),
        name="merge",
    )(x, h_a, h_b, z, z, w_a, w_b, w_o, g)


def _ffn_kernel(hn_ref, x1_ref, wu_ref, wd_ref, g_ref, y_ref):
    f = pl.program_id(1)

    @pl.when(f == 0)
    def _():
        y_ref[...] = x1_ref[...]

    u = jnp.maximum(jnp.dot(hn_ref[...], wu_ref[...], preferred_element_type=f32), 0.0)
    y_ref[...] += jnp.dot((u * u).astype(bf16), wd_ref[...], preferred_element_type=f32)

    @pl.when(f == pl.num_programs(1) - 1)
    def _():
        x2 = y_ref[...]
        y_ref[...] = x2 * lax.rsqrt(jnp.mean(x2 * x2, axis=-1, keepdims=True) + EPS) * g_ref[...]


def _ffn(hn, x1, w_up, w_down, g):
    t, d = x1.shape
    dff = w_up.shape[1]
    tm = _tile(t, 512)
    tf = _tile(dff, 1024)
    return pl.pallas_call(
        _ffn_kernel,
        grid=(t // tm, dff // tf),
        in_specs=[pl.BlockSpec((tm, d), lambda i, f: (i, 0)),
                  pl.BlockSpec((tm, d), lambda i, f: (i, 0)),
                  pl.BlockSpec((d, tf), lambda i, f: (0, f)),
                  pl.BlockSpec((tf, d), lambda i, f: (f, 0)),
                  pl.BlockSpec((1, d), lambda i, f: (0, 0))],
        out_specs=pl.BlockSpec((tm, d), lambda i, f: (i, 0)),
        out_shape=jax.ShapeDtypeStruct((t, d), f32),
        compiler_params=_params("parallel", "arbitrary"),
        name="ffn",
    )(hn, x1, w_up, w_down, g)


def _layer(x3, w, mstate, fox_past):
    nstr, frames, d = x3.shape
    t = nstr * frames
    x = x3.reshape(t, d)
    xn, gates_t = _norm_gates(x, w["norm_mix"], w["w_in_t"], *w["gate_rows"])
    w_proj_t = w["w_proj_t"]
    k32, k16 = _matmul_heads_t(xn, w_proj_t, 0, "proj_k")
    v32, v16 = _matmul_heads_t(xn, w_proj_t, F_WIDTH, "proj_v")
    z = _matmul_t(xn, w_proj_t, 2 * F_WIDTH, w_proj_t.shape[0] - 2 * F_WIDTH, "proj_rest")

    c0, n0, m0 = mstate
    h_a, c_new, n_slab, m_slab = _mlstm(z, gates_t, w["mlstm_bias"], w["norm_mlstm_h"], c0, n0, m0,
                                        n_streams=nstr)
    n_new = n_slab[..., 0]
    m_new = m_slab[:, :, 0]

    if fox_past is None:
        logf_t, cum_t = _scan_rows(gates_t, w["b_fox_f_col"], row_block=1, rows=F_HEADS, n_streams=nstr,
                                   apply_log_sigmoid=True)
        h_b = _fox_prompt(z, k16, v16, cum_t, q_col0=w["q_col0"], n_streams=nstr)
        logf = logf_t.T.reshape(nstr, frames, F_HEADS)
    else:
        ck, cv, clf = fox_past
        p = ck.shape[1]
        clf_t = jnp.transpose(clf, (0, 2, 1)).reshape(nstr * F_HEADS, p)
        _, cum_c = _scan_rows(clf_t, jnp.zeros((nstr * F_HEADS, 1), f32), row_block=0, rows=nstr * F_HEADS,
                              n_streams=1, apply_log_sigmoid=False)
        h_b, logf4 = _fox_decode(z, k16, v16, gates_t, w["b_fox_f"], ck.reshape(nstr, p * F_HEADS, F_HEAD_DIM),
                                 cv.reshape(nstr, p * F_HEADS, F_HEAD_DIM), cum_c.reshape(nstr * F_HEADS, 1, p),
                                 q_col0=w["q_col0"], n_streams=nstr)
        logf = jnp.transpose(logf4[:, :, 0, :], (0, 2, 1))

    x1, hn = _merge(x, h_a, h_b, z, w["w_branch_a"], w["w_branch_b"], w["w_out"], w["norm_ffn"],
                    gate_col0=w["gate_col0"])
    y = _ffn(hn, x1, w["w_up"], w["w_down"], w["norm_final"])

    k_rows = k32.reshape(nstr, frames, F_HEADS, F_HEAD_DIM)
    v_rows = v32.reshape(nstr, frames, F_HEADS, F_HEAD_DIM)
    return y.reshape(nstr, frames, d), (k_rows, v_rows, logf, c_new, n_new, m_new)


def _regroup_kernel(table_ref, w_ref, o_ref):
    sid = table_ref[pl.program_id(0), 1]
    scale = jnp.where(sid == 1, M_DQK ** -0.5, jnp.where(sid == 2, F_HEAD_DIM ** -0.5 * LOG2E, 1.0))
    o_ref[...] = (w_ref[...] * scale).astype(bf16)


def _regroup_w_in_t(w_t, offs):
    d = w_t.shape[1]
    mq, mk, mv, mo, mi, mf, fq, fk, fv, ff, ga, gb = range(12)
    order = ((fk, 0), (fv, 0), (mq, 0), (mk, 1), (mv, 0), (mo, 0), (ga, 0), (gb, 0), (fq, 2))
    tr = _tile(d, F_WIDTH)
    table = []
    for seg, scale_id in order:
        start, stop = offs[seg], offs[seg + 1]
        assert start % 8 == 0 and (stop - start) % tr == 0
        table += [(r // 8, scale_id) for r in range(start, stop, tr)]
    table = jnp.asarray(table, jnp.int32)
    return pl.pallas_call(
        _regroup_kernel,
        grid_spec=pltpu.PrefetchScalarGridSpec(
            num_scalar_prefetch=1,
            grid=(table.shape[0],),
            in_specs=[pl.BlockSpec((pl.Element(tr), pl.Element(d)),
                                   lambda t, tbl: (pl.multiple_of(tbl[t, 0] * 8, 8), 0))],
            out_specs=pl.BlockSpec((tr, d), lambda t, tbl: (t, 0))),
        out_shape=jax.ShapeDtypeStruct((table.shape[0] * tr, d), bf16),
        compiler_params=_params("parallel"),
        name="regroup_w_in",
    )(table, w_t)


def _prepare_weights(norm_mix, w_in, b_mlstm_i, b_mlstm_f, b_fox_f, norm_mlstm_h, w_branch_a, w_branch_b,
                     w_out, norm_ffn, w_up, w_down, norm_final):
    d = w_in.shape[0]
    sizes = (M_HEADS * M_DQK, M_HEADS * M_DQK, M_WIDTH, M_WIDTH, M_HEADS, M_HEADS,
             F_WIDTH, F_WIDTH, F_WIDTH, F_HEADS, d, d)
    offs = [0]
    for s in sizes:
        offs.append(offs[-1] + s)
    w_t = w_in.T
    return {
        "norm_mix": norm_mix.reshape(1, d),
        "w_in_t": w_t,
        "gate_rows": (offs[4], offs[9]),
        "w_proj_t": _regroup_w_in_t(w_t, offs),
        "gate_col0": 4 * M_WIDTH,
        "q_col0": 4 * M_WIDTH + 2 * d,
        "mlstm_bias": jnp.concatenate([b_mlstm_i, b_mlstm_f]).astype(f32).reshape(2 * M_HEADS, 1),
        "b_fox_f": b_fox_f.astype(f32),
        "b_fox_f_col": b_fox_f.astype(f32).reshape(F_HEADS, 1),
        "norm_mlstm_h": norm_mlstm_h.reshape(1, M_WIDTH),
        "w_branch_a": w_branch_a.astype(bf16),
        "w_branch_b": w_branch_b.astype(bf16),
        "w_out": w_out.astype(bf16),
        "norm_ffn": norm_ffn.reshape(1, d),
        "w_up": w_up.astype(bf16),
        "w_down": w_down.astype(bf16),
        "norm_final": norm_final.reshape(1, d),
    }


def kernel(x_prompt, x_sample, cache_fox_k, cache_fox_v, cache_fox_logf, state_mlstm_c, state_mlstm_n, state_mlstm_m, norm_mix, w_in, b_mlstm_i, b_mlstm_f, b_fox_f, norm_mlstm_h, w_branch_a, w_branch_b, w_out, norm_ffn, w_up, w_down, norm_final):
    depth = w_in.shape[0]
    assert depth == 1, "the final norm is fused into the layer's FFN kernel"
    w = _prepare_weights(norm_mix[0], w_in[0], b_mlstm_i[0], b_mlstm_f[0], b_fox_f[0], norm_mlstm_h[0],
                         w_branch_a[0], w_branch_b[0], w_out[0], norm_ffn[0], w_up[0], w_down[0], norm_final)
    bp = x_prompt.shape[0]
    fresh = (jnp.zeros((bp, M_HEADS, M_DQK, M_DV), f32), jnp.zeros((bp, M_HEADS, M_DQK), f32),
             jnp.zeros((bp, M_HEADS), f32))
    y_p, st_p = _layer(x_prompt, w, fresh, None)
    y_s, st_s = _layer(x_sample, w, (state_mlstm_c[0], state_mlstm_n[0], state_mlstm_m[0]),
                       (cache_fox_k[0], cache_fox_v[0], cache_fox_logf[0]))
    return (y_p, y_s) + tuple(a[None] for a in st_p) + tuple(a[None] for a in st_s)
```

```python
import functools

import jax
import jax.numpy as jnp
from jax import lax
from jax.experimental import pallas as pl
from jax.experimental.pallas import tpu as pltpu

M_HEADS = 4
M_DQK = 256
M_DV = 256
M_WIDTH = M_HEADS * M_DV
F_HEADS = 8
F_HEAD_DIM = 128
F_WIDTH = F_HEADS * F_HEAD_DIM
EPS = 1e-6
N_GATE_ROWS = 2 * M_HEADS + F_HEADS
LANES = 128
assert F_HEAD_DIM == LANES
MLSTM_BLOCK = 256
ATTN_BLOCK = 512
ATTN_TRIP = 4
LOG2E = 1.4426950408889634
VMEM_LIMIT_BYTES = 56 * 1024 * 1024

f32 = jnp.float32
bf16 = jnp.bfloat16

_NT = (((1,), (1,)), ((), ()))
_TN = (((0,), (0,)), ((), ()))


def _tile(n, pref):
    t = min(n, pref)
    while n % t:
        t //= 2
    return t


def _params(*sem):
    return pltpu.CompilerParams(dimension_semantics=sem, vmem_limit_bytes=VMEM_LIMIT_BYTES)


def _log_sigmoid(z):
    return jnp.minimum(z, 0.0) - jnp.log1p(jnp.exp(-jnp.abs(z)))


def _cumsum_lanes(x):
    rows, n = x.shape
    pad = -rows % 16
    if rows == 1:
        x16 = jnp.broadcast_to(x, (16, n))
    elif pad == 0:
        x16 = x
    else:
        x16 = jnp.concatenate([x, jnp.zeros((pad, n), f32)], axis=0)
    r = lax.broadcasted_iota(jnp.int32, (n, n), 0)
    c = lax.broadcasted_iota(jnp.int32, (n, n), 1)
    u = jnp.where(r <= c, 1.0, 0.0).astype(bf16)
    hi = x16.astype(bf16)
    rem = x16 - hi.astype(f32)
    mid = rem.astype(bf16)
    lo = (rem - mid.astype(f32)).astype(bf16)
    out = (jnp.dot(hi, u, preferred_element_type=f32) + jnp.dot(mid, u, preferred_element_type=f32)
           + jnp.dot(lo, u, preferred_element_type=f32))
    return out[:rows]


def _norm_kernel(x_ref, g_ref, wm_ref, wf_ref, xn_ref, gt_ref):
    x = x_ref[...]
    y = x * lax.rsqrt(jnp.mean(x * x, axis=-1, keepdims=True) + EPS) * g_ref[...]
    xn = y.astype(bf16)
    xn_ref[...] = xn
    wg = jnp.concatenate([wm_ref[...], wf_ref[...]], axis=0).astype(bf16)
    gt_ref[...] = lax.dot_general(wg, xn, _NT, preferred_element_type=f32)


def _norm_gates(x, g, w_t, mlstm_gate_row, fox_gate_row):
    t, d = x.shape
    tm = _tile(t, 512)
    assert 2 * M_HEADS == F_HEADS and mlstm_gate_row % F_HEADS == 0 and fox_gate_row % F_HEADS == 0
    return pl.pallas_call(
        _norm_kernel,
        grid=(t // tm,),
        in_specs=[pl.BlockSpec((tm, d), lambda i: (i, 0)),
                  pl.BlockSpec((1, d), lambda i: (0, 0)),
                  pl.BlockSpec((F_HEADS, d), lambda i: (mlstm_gate_row // F_HEADS, 0)),
                  pl.BlockSpec((F_HEADS, d), lambda i: (fox_gate_row // F_HEADS, 0))],
        out_specs=[pl.BlockSpec((tm, d), lambda i: (i, 0)),
                   pl.BlockSpec((N_GATE_ROWS, tm), lambda i: (0, i))],
        out_shape=[jax.ShapeDtypeStruct((t, d), bf16), jax.ShapeDtypeStruct((N_GATE_ROWS, t), f32)],
        compiler_params=_params("parallel"),
        name="norm_gates",
    )(x, g, w_t, w_t)


def _mm_kernel(a_ref, w_ref, o_ref):
    o_ref[...] = lax.dot_general(a_ref[...], w_ref[...], _NT, preferred_element_type=f32).astype(o_ref.dtype)


def _matmul_t(a, w_t, row0, n, name):
    t, k = a.shape
    tm = _tile(t, 2048)
    tn = _tile(n, 1024)
    assert row0 % tn == 0
    return pl.pallas_call(
        _mm_kernel,
        grid=(t // tm, n // tn),
        in_specs=[pl.BlockSpec((tm, k), lambda i, j: (i, 0)),
                  pl.BlockSpec((tn, k), lambda i, j: (row0 // tn + j, 0))],
        out_specs=pl.BlockSpec((tm, tn), lambda i, j: (i, j)),
        out_shape=jax.ShapeDtypeStruct((t, n), bf16),
        compiler_params=_params("parallel", "arbitrary"),
        name=name,
    )(a, w_t)


def _mm_heads_kernel(a_ref, w_ref, o16_ref, *rows_ref):
    r = lax.dot_general(a_ref[...], w_ref[...], _NT, preferred_element_type=f32)
    tm = a_ref.shape[0]
    for h in range(F_HEADS):
        head = r[:, h * F_HEAD_DIM:(h + 1) * F_HEAD_DIM]
        o16_ref[h] = head.astype(bf16)
        for ref in rows_ref:
            ref[pl.ds(h, tm, stride=F_HEADS), :] = head


def _matmul_heads_t(a, w_t, row0, name, *, state_rows):
    t, k = a.shape
    assert row0 % F_WIDTH == 0
    tm = _tile(t, 1024)
    out_specs = [pl.BlockSpec((F_HEADS, tm, F_HEAD_DIM), lambda i: (0, i, 0))]
    out_shape = [jax.ShapeDtypeStruct((F_HEADS, t, F_HEAD_DIM), bf16)]
    if state_rows:
        out_specs.append(pl.BlockSpec((tm * F_HEADS, F_HEAD_DIM), lambda i: (i, 0)))
        out_shape.append(jax.ShapeDtypeStruct((t * F_HEADS, F_HEAD_DIM), f32))
    return pl.pallas_call(
        _mm_heads_kernel,
        grid=(t // tm,),
        in_specs=[pl.BlockSpec((tm, k), lambda i: (i, 0)),
                  pl.BlockSpec((F_WIDTH, k), lambda i: (row0 // F_WIDTH, 0))],
        out_specs=out_specs,
        out_shape=out_shape,
        compiler_params=_params("parallel"),
        name=name,
    )(a, w_t)


def _scan_kernel(x_ref, bias_ref, logf_ref, cum_ref, carry_ref, *, apply_log_sigmoid):
    @pl.when(pl.program_id(1) == 0)
    def _():
        carry_ref[...] = jnp.zeros_like(carry_ref)

    x = x_ref[...]
    if apply_log_sigmoid:
        x = _log_sigmoid(x + bias_ref[...])
    logf_ref[...] = x
    cum = _cumsum_lanes(x) + carry_ref[:, :1]
    cum_ref[...] = cum * LOG2E
    carry_ref[...] = jnp.broadcast_to(cum[:, -1:], carry_ref.shape)


def _scan_rows(x, bias, *, row_block, rows, n_streams, apply_log_sigmoid):
    total = x.shape[1]
    s = total // n_streams
    tb = _tile(s, 512)
    nb = s // tb
    spec = pl.BlockSpec((rows, tb), lambda b, j: (0, b * nb + j))
    return pl.pallas_call(
        functools.partial(_scan_kernel, apply_log_sigmoid=apply_log_sigmoid),
        grid=(n_streams, nb),
        in_specs=[pl.BlockSpec((rows, tb), lambda b, j: (row_block, b * nb + j)),
                  pl.BlockSpec((rows, 1), lambda b, j: (0, 0))],
        out_specs=[spec, spec],
        out_shape=[jax.ShapeDtypeStruct((rows, total), f32)] * 2,
        scratch_shapes=[pltpu.VMEM((rows, LANES), f32)],
        compiler_params=_params("arbitrary", "arbitrary"),
        name="logf_scan",
    )(x, bias)


def _mlstm_kernel(bias_ref, q_ref, k_ref, v_ref, o_ref, gates_ref, nh_ref, c0_ref, n0_ref, m0_ref,
                  h_ref, c_out_ref, n_out_ref, m_out_ref, caug_ref, m_ref):
    blk = pl.program_id(1)
    n_blk = pl.num_programs(1)
    L = q_ref.shape[0]

    @pl.when(blk == 0)
    def _():
        lane = lax.broadcasted_iota(jnp.int32, (M_HEADS, M_DQK, LANES), 2)
        caug_ref[:, :, :M_DV] = c0_ref[...]
        caug_ref[:, :, M_DV:] = jnp.where(lane == 0, n0_ref[...], 0.0)
        m_ref[...] = m0_ref[...]

    gates = gates_ref[...]
    ig_all = gates[:M_HEADS] + bias_ref[:M_HEADS]
    lf_all = _log_sigmoid(gates[M_HEADS:2 * M_HEADS] + bias_ref[M_HEADS:])
    a_all = ig_all - _cumsum_lanes(lf_all)

    t_idx = lax.broadcasted_iota(jnp.int32, (L, L), 0)
    s_idx = lax.broadcasted_iota(jnp.int32, (L, L), 1)
    causal = s_idx <= t_idx
    diag = s_idx == t_idx
    ones_col = jnp.where(lax.broadcasted_iota(jnp.int32, (L, LANES), 1) == 0, 1.0, 0.0).astype(bf16)

    for h in range(M_HEADS):
        cols = slice(h * M_DV, (h + 1) * M_DV)
        m0 = m_ref[h:h + 1, :1]
        ig = ig_all[h:h + 1]
        lf = lf_all[h:h + 1]
        a_row = a_all[h:h + 1]
        a_mat = jnp.where(causal, a_row, -jnp.inf)
        g_col = jnp.maximum(m0, jnp.max(a_mat, axis=1, keepdims=True))
        b_col = jnp.sum(jnp.where(causal, lf, 0.0), axis=1, keepdims=True)
        w_intra = jnp.exp(a_mat - g_col)
        w_inter = jnp.exp(m0 - g_col)

        q = q_ref[:, cols]
        k = k_ref[:, cols]
        v_aug = jnp.concatenate([v_ref[:, cols], ones_col], axis=1)
        c_aug = caug_ref[h]

        s = lax.dot_general(q, k, _NT, preferred_element_type=f32)
        sw = (s * w_intra).astype(bf16)
        num_aug = (jnp.dot(sw, v_aug, preferred_element_type=f32)
                   + w_inter * jnp.dot(q, c_aug.astype(bf16), preferred_element_type=f32))
        num = num_aug[:, :M_DV]
        den = num_aug[:, M_DV:M_DV + 1]
        den = jnp.maximum(jnp.abs(den), jnp.exp(-(b_col + g_col)))
        hh = num / den
        hh = hh * lax.rsqrt(jnp.mean(hh * hh, axis=-1, keepdims=True) + EPS)
        hh = hh * nh_ref[:, cols] * jax.nn.sigmoid(o_ref[:, cols].astype(f32))
        h_ref[:, cols] = hh.astype(h_ref.dtype)

        g_end = jnp.maximum(m0, jnp.max(a_row, axis=1, keepdims=True))
        b_end = jnp.sum(lf, axis=1, keepdims=True)
        ig_col = jnp.sum(jnp.where(diag, ig, 0.0), axis=1, keepdims=True)
        w_tok = jnp.exp(ig_col - b_col - g_end)
        w_state = jnp.exp(m0 - g_end)
        kw = (k.astype(f32) * w_tok).astype(bf16)
        caug_ref[h] = w_state * c_aug + lax.dot_general(kw, v_aug, _TN, preferred_element_type=f32)
        m_ref[h:h + 1, :] = jnp.broadcast_to(b_end + g_end, (1, LANES))

    @pl.when(blk == n_blk - 1)
    def _():
        c_out_ref[...] = caug_ref[:, :, :M_DV]
        n_out_ref[...] = caug_ref[:, :, M_DV:]
        m_out_ref[...] = m_ref[...]


def _mlstm(z, gates_t, gate_bias, norm_h, c0, n0, m0, *, n_streams):
    t = z.shape[0]
    s = t // n_streams
    L = _tile(s, MLSTM_BLOCK)
    nb = s // L
    gates3 = gates_t.reshape(N_GATE_ROWS, n_streams * nb, L).transpose(1, 0, 2)
    n0c = n0.reshape(n_streams, M_HEADS, M_DQK, 1)
    m0b = jnp.broadcast_to(m0.reshape(n_streams, M_HEADS, 1), (n_streams, M_HEADS, LANES))

    def zcols(group):
        return pl.BlockSpec((L, M_WIDTH), lambda b, c: (b * nb + c, group))

    def state(*tail):
        return pl.BlockSpec((None, M_HEADS) + tail, lambda b, c: (b, 0) + (0,) * len(tail))

    return pl.pallas_call(
        _mlstm_kernel,
        grid=(n_streams, nb),
        in_specs=[pl.BlockSpec((2 * M_HEADS, 1), lambda b, c: (0, 0)),
                  zcols(0), zcols(1), zcols(2), zcols(3),
                  pl.BlockSpec((None, N_GATE_ROWS, L), lambda b, c: (b * nb + c, 0, 0)),
                  pl.BlockSpec((1, M_WIDTH), lambda b, c: (0, 0)),
                  state(M_DQK, M_DV), state(M_DQK, 1), state(LANES)],
        out_specs=[pl.BlockSpec((L, M_WIDTH), lambda b, c: (b * nb + c, 0)),
                   state(M_DQK, M_DV), state(M_DQK, LANES), state(LANES)],
        out_shape=[jax.ShapeDtypeStruct((t, M_WIDTH), bf16),
                   jax.ShapeDtypeStruct((n_streams, M_HEADS, M_DQK, M_DV), f32),
                   jax.ShapeDtypeStruct((n_streams, M_HEADS, M_DQK, LANES), f32),
                   jax.ShapeDtypeStruct((n_streams, M_HEADS, LANES), f32)],
        scratch_shapes=[pltpu.VMEM((M_HEADS, M_DQK, M_DV + LANES), f32), pltpu.VMEM((M_HEADS, LANES), f32)],
        compiler_params=_params("parallel", "arbitrary"),
        name="mlstm",
    )(gate_bias, z, z, z, z, gates3, norm_h, c0, n0c, m0b)


def _fox_prompt_kernel(q_ref, k_ref, v_ref, cum_ref, o_ref,
                       sa_ref, sb_ref, pa_ref, pb_ref, xa_ref, xb_ref, m_ref, l_ref, acc_ref):
    i = pl.program_id(2)
    tq = o_ref.shape[0]
    tk = sa_ref.shape[1]
    assert tq == tk

    def q_rows(r):
        return q_ref[pl.ds(pl.multiple_of(r * tq, tq), tq), :]

    def k_rows(ref, j):
        return ref[pl.ds(pl.multiple_of(j * tk, tk), tk), :]

    def causal(s, j):
        ahead = (lax.broadcasted_iota(jnp.int32, (tq, tk), 1) - lax.broadcasted_iota(jnp.int32, (tq, tk), 0))
        return jnp.where(ahead <= i * tq - j * tk, s, -jnp.inf)

    def scores(j, s_ref, x_ref, masked, q_row=i):
        s = lax.dot_general(q_rows(q_row), k_rows(k_ref, j), _NT, preferred_element_type=f32) - cum_ref[j]
        if masked:
            s = causal(s, j)
        s_ref[...] = s
        x_ref[...] = jnp.broadcast_to(jnp.max(s, axis=1, keepdims=True), x_ref.shape)

    def values(j, p_ref):
        return jnp.dot(p_ref[...], k_rows(v_ref, j), preferred_element_type=f32)

    def softmax(s_ref, x_ref, p_ref, pv_prev, mask_block=None):
        s = s_ref[...]
        if mask_block is None:
            x = x_ref[...]
        else:
            s = causal(s, mask_block)
            x = jnp.max(s, axis=1, keepdims=True)
        m_old = m_ref[...]
        m_new = jnp.maximum(m_old, x)
        alpha = jnp.exp2(m_old - m_new)
        p = jnp.exp2(s - jnp.concatenate([m_new] * (tk // LANES), axis=1))
        p_ref[...] = p.astype(bf16)
        l_ref[...] = alpha * l_ref[...] + jnp.sum(p, axis=1, keepdims=True)
        acc_ref[...] = alpha * (acc_ref[...] + pv_prev)
        m_ref[...] = m_new

    m_ref[...] = jnp.full(m_ref.shape, -jnp.inf, f32)
    l_ref[...] = jnp.zeros(l_ref.shape, f32)
    acc_ref[...] = jnp.zeros(acc_ref.shape, f32)
    pb_ref[...] = jnp.zeros(pb_ref.shape, bf16)

    @pl.when(i == 0)
    def _():
        scores(0, sa_ref, xa_ref, False)

    def pair(j0):
        scores(j0 + 1, sb_ref, xb_ref, False)
        softmax(sa_ref, xa_ref, pa_ref, values(jnp.maximum(j0 - 1, 0), pb_ref))
        scores(j0 + 2, sa_ref, xa_ref, False)
        softmax(sb_ref, xb_ref, pb_ref, values(j0, pa_ref))

    def long_trip(jj, carry):
        for u in range(0, ATTN_TRIP, 2):
            pair(ATTN_TRIP * jj + u)
        return carry

    def short_trip(jj, carry):
        pair(ATTN_TRIP * (i // ATTN_TRIP) + 2 * jj)
        return carry

    lax.fori_loop(0, i // ATTN_TRIP, long_trip, 0)
    lax.fori_loop(0, (i % ATTN_TRIP) // 2, short_trip, 0)

    def scores_of_next_row():
        scores(0, sa_ref, xa_ref, False, q_row=jnp.minimum(i + 1, pl.num_programs(2) - 1))

    @pl.when(i % 2 == 0)
    def _():
        softmax(sa_ref, xa_ref, pa_ref, values(jnp.maximum(i - 1, 0), pb_ref), mask_block=i)
        scores_of_next_row()
        acc_ref[...] += values(i, pa_ref)

    @pl.when(i % 2 == 1)
    def _():
        scores(i, sb_ref, xb_ref, True)
        softmax(sa_ref, xa_ref, pa_ref, values(jnp.maximum(i - 2, 0), pb_ref))
        scores_of_next_row()
        softmax(sb_ref, xb_ref, pb_ref, values(i - 1, pa_ref))
        acc_ref[...] += values(i, pb_ref)

    o_ref[...] = (acc_ref[...] / l_ref[...]).astype(o_ref.dtype)


def _fox_prompt(q, k, v, cum_t, *, n_streams):
    t = q.shape[1]
    s = t // n_streams
    tq = tk = _tile(s, ATTN_BLOCK)
    nq = nk = s // tk
    cum4 = cum_t.reshape(F_HEADS, n_streams * nk, 1, tk)
    head_rows = pl.BlockSpec((None, s, F_HEAD_DIM), lambda b, h, i: (h, b, 0))
    return pl.pallas_call(
        _fox_prompt_kernel,
        grid=(n_streams, F_HEADS, nq),
        in_specs=[head_rows, head_rows, head_rows,
                  pl.BlockSpec((None, nk, 1, tk), lambda b, h, i: (h, b, 0, 0))],
        out_specs=pl.BlockSpec((tq, F_HEAD_DIM), lambda b, h, i: (b * nq + i, h)),
        out_shape=jax.ShapeDtypeStruct((t, F_WIDTH), bf16),
        scratch_shapes=[pltpu.VMEM((tq, tk), f32)] * 2 + [pltpu.VMEM((tq, tk), bf16)] * 2
        + [pltpu.VMEM((tq, LANES), f32)] * 5,
        compiler_params=_params("parallel", "parallel", "arbitrary"),
        name="fox_prompt",
    )(q, k, v, cum4)


def _fox_decode_kernel(bias_ref, q_ref, kn_ref, vn_ref, kc_ref, vc_ref, cumc_ref, fg_ref, o_ref, logf_ref):
    head = pl.program_id(1)
    L = q_ref.shape[0]
    q = q_ref[...]
    logf = _log_sigmoid(fg_ref[...] + bias_ref[head])
    logf_ref[...] = logf
    cum_new = _cumsum_lanes(logf) * LOG2E
    cum_c = cumc_ref[...]
    cum_c = cum_c - cum_c[:, -1:]

    p_len = kc_ref.shape[0] // F_HEADS
    kc = kc_ref[pl.ds(head, p_len, stride=F_HEADS), :].astype(bf16)
    vc = vc_ref[pl.ds(head, p_len, stride=F_HEADS), :].astype(bf16)
    s_c = lax.dot_general(q, kc, _NT, preferred_element_type=f32) - cum_c
    s_n = lax.dot_general(q, kn_ref[...], _NT, preferred_element_type=f32) - cum_new
    row = lax.broadcasted_iota(jnp.int32, (L, L), 0)
    col = lax.broadcasted_iota(jnp.int32, (L, L), 1)
    s_n = jnp.where(col <= row, s_n, -jnp.inf)
    m = jnp.maximum(jnp.max(s_c, axis=1, keepdims=True), jnp.max(s_n, axis=1, keepdims=True))
    p_c = jnp.exp2(s_c - m)
    p_n = jnp.exp2(s_n - m)
    l = jnp.sum(p_c, axis=1, keepdims=True) + jnp.sum(p_n, axis=1, keepdims=True)
    acc = (jnp.dot(p_c.astype(bf16), vc, preferred_element_type=f32)
           + jnp.dot(p_n.astype(bf16), vn_ref[...], preferred_element_type=f32))
    o_ref[...] = (acc / l).astype(o_ref.dtype)


def _fox_decode(q, k, v, gates_t, bias, cache_k, cache_v, cum_cache, *, n_streams):
    t = q.shape[1]
    L = t // n_streams
    p = cache_k.shape[1] // F_HEADS
    gates4 = gates_t.reshape(N_GATE_ROWS, n_streams, 1, L)
    head_rows = pl.BlockSpec((None, L, F_HEAD_DIM), lambda b, h: (h, b, 0))
    return pl.pallas_call(
        _fox_decode_kernel,
        grid=(n_streams, F_HEADS),
        in_specs=[pl.BlockSpec(memory_space=pltpu.SMEM),
                  head_rows, head_rows, head_rows,
                  pl.BlockSpec((None, p * F_HEADS, F_HEAD_DIM), lambda b, h: (b, 0, 0)),
                  pl.BlockSpec((None, p * F_HEADS, F_HEAD_DIM), lambda b, h: (b, 0, 0)),
                  pl.BlockSpec((None, 1, p), lambda b, h: (b * F_HEADS + h, 0, 0)),
                  pl.BlockSpec((None, None, 1, L), lambda b, h: (2 * M_HEADS + h, b, 0, 0))],
        out_specs=[pl.BlockSpec((L, F_HEAD_DIM), lambda b, h: (b, h)),
                   pl.BlockSpec((None, None, 1, L), lambda b, h: (b, h, 0, 0))],
        out_shape=[jax.ShapeDtypeStruct((t, F_WIDTH), bf16),
                   jax.ShapeDtypeStruct((n_streams, F_HEADS, 1, L), f32)],
        compiler_params=_params("parallel", "parallel"),
        name="fox_decode",
    )(bias, q, k, v, cache_k, cache_v, cum_cache, gates4)


def _merge_kernel(x_ref, ha_ref, hb_ref, ga_ref, gb_ref, wa_ref, wb_ref, wo_ref, g_ref, x1_ref, hn_ref):
    pa = jnp.dot(ha_ref[...], wa_ref[...], preferred_element_type=f32)
    pb = jnp.dot(hb_ref[...], wb_ref[...], preferred_element_type=f32)
    merged = (jax.nn.sigmoid(ga_ref[...].astype(f32)) * pa + jax.nn.sigmoid(gb_ref[...].astype(f32)) * pb)
    x1 = x_ref[...] + jnp.dot(merged.astype(bf16), wo_ref[...], preferred_element_type=f32)
    x1_ref[...] = x1
    hn = x1 * lax.rsqrt(jnp.mean(x1 * x1, axis=-1, keepdims=True) + EPS) * g_ref[...]
    hn_ref[...] = hn.astype(hn_ref.dtype)


def _merge(x, h_a, h_b, z, w_a, w_b, w_o, g, *, gate_col0):
    t, d = x.shape
    tm = _tile(t, 256)
    ga_blk = gate_col0 // d
    row = lambda i: (i, 0)
    fixed = lambda i: (0, 0)
    resident = functools.partial(pl.BlockSpec, index_map=fixed, pipeline_mode=pl.Buffered(1))
    return pl.pallas_call(
        _merge_kernel,
        grid=(t // tm,),
        in_specs=[pl.BlockSpec((tm, d), row),
                  pl.BlockSpec((tm, M_WIDTH), row),
                  pl.BlockSpec((tm, F_WIDTH), row),
                  pl.BlockSpec((tm, d), lambda i: (i, ga_blk)),
                  pl.BlockSpec((tm, d), lambda i: (i, ga_blk + 1)),
                  resident(w_a.shape), resident(w_b.shape), resident(w_o.shape),
                  pl.BlockSpec((1, d), fixed)],
        out_specs=[pl.BlockSpec((tm, d), row), pl.BlockSpec((tm, d), row)],
        out_shape=[jax.ShapeDtypeStruct((t, d), f32), jax.ShapeDtypeStruct((t, d), bf16)],
        compiler_params=_params("parallel"),
        name="merge",
    )(x, h_a, h_b, z, z, w_a, w_b, w_o, g)


def _ffn_kernel(hn_ref, x1_ref, wu_ref, wd_ref, g_ref, y_ref):
    f = pl.program_id(1)

    @pl.when(f == 0)
    def _():
        y_ref[...] = x1_ref[...]

    u = jnp.maximum(jnp.dot(hn_ref[...], wu_ref[...], preferred_element_type=f32), 0.0)
    y_ref[...] += jnp.dot((u * u).astype(bf16), wd_ref[...], preferred_element_type=f32)

    @pl.when(f == pl.num_programs(1) - 1)
    def _():
        x2 = y_ref[...]
        y_ref[...] = x2 * lax.rsqrt(jnp.mean(x2 * x2, axis=-1, keepdims=True) + EPS) * g_ref[...]


def _ffn(hn, x1, w_up, w_down, g):
    t, d = x1.shape
    dff = w_up.shape[1]
    tm = _tile(t, 512)
    tf = _tile(dff, 1024)
    return pl.pallas_call(
        _ffn_kernel,
        grid=(t // tm, dff // tf),
        in_specs=[pl.BlockSpec((tm, d), lambda i, f: (i, 0)),
                  pl.BlockSpec((tm, d), lambda i, f: (i, 0)),
                  pl.BlockSpec((d, tf), lambda i, f: (0, f)),
                  pl.BlockSpec((tf, d), lambda i, f: (f, 0)),
                  pl.BlockSpec((1, d), lambda i, f: (0, 0))],
        out_specs=pl.BlockSpec((tm, d), lambda i, f: (i, 0)),
        out_shape=jax.ShapeDtypeStruct((t, d), f32),
        compiler_params=_params("parallel", "arbitrary"),
        name="ffn",
    )(hn, x1, w_up, w_down, g)


def _layer(x3, w, mstate, fox_past):
    nstr, frames, d = x3.shape
    t = nstr * frames
    x = x3.reshape(t, d)
    xn, gates_t = _norm_gates(x, w["norm_mix"], w["w_in_t"], *w["gate_rows"])
    w_proj_t = w["w_proj_t"]
    n_rest = w_proj_t.shape[0] - 3 * F_WIDTH
    k16, k32 = _matmul_heads_t(xn, w_proj_t, 0, "proj_k", state_rows=True)
    v16, v32 = _matmul_heads_t(xn, w_proj_t, F_WIDTH, "proj_v", state_rows=True)
    z = _matmul_t(xn, w_proj_t, 2 * F_WIDTH, n_rest, "proj_rest")
    (q16,) = _matmul_heads_t(xn, w_proj_t, 2 * F_WIDTH + n_rest, "proj_q", state_rows=False)

    c0, n0, m0 = mstate
    h_a, c_new, n_slab, m_slab = _mlstm(z, gates_t, w["mlstm_bias"], w["norm_mlstm_h"], c0, n0, m0,
                                        n_streams=nstr)
    n_new = n_slab[..., 0]
    m_new = m_slab[:, :, 0]

    if fox_past is None:
        logf_t, cum_t = _scan_rows(gates_t, w["b_fox_f_col"], row_block=1, rows=F_HEADS, n_streams=nstr,
                                   apply_log_sigmoid=True)
        h_b = _fox_prompt(q16, k16, v16, cum_t, n_streams=nstr)
        logf = logf_t.T.reshape(nstr, frames, F_HEADS)
    else:
        ck, cv, clf = fox_past
        p = ck.shape[1]
        clf_t = jnp.transpose(clf, (0, 2, 1)).reshape(nstr * F_HEADS, p)
        _, cum_c = _scan_rows(clf_t, jnp.zeros((nstr * F_HEADS, 1), f32), row_block=0, rows=nstr * F_HEADS,
                              n_streams=1, apply_log_sigmoid=False)
        h_b, logf4 = _fox_decode(q16, k16, v16, gates_t, w["b_fox_f"], ck.reshape(nstr, p * F_HEADS, F_HEAD_DIM),
                                 cv.reshape(nstr, p * F_HEADS, F_HEAD_DIM), cum_c.reshape(nstr * F_HEADS, 1, p),
                                 n_streams=nstr)
        logf = jnp.transpose(logf4[:, :, 0, :], (0, 2, 1))

    x1, hn = _merge(x, h_a, h_b, z, w["w_branch_a"], w["w_branch_b"], w["w_out"], w["norm_ffn"],
                    gate_col0=w["gate_col0"])
    y = _ffn(hn, x1, w["w_up"], w["w_down"], w["norm_final"])

    k_rows = k32.reshape(nstr, frames, F_HEADS, F_HEAD_DIM)
    v_rows = v32.reshape(nstr, frames, F_HEADS, F_HEAD_DIM)
    return y.reshape(nstr, frames, d), (k_rows, v_rows, logf, c_new, n_new, m_new)


def _regroup_kernel(table_ref, w_ref, o_ref):
    sid = table_ref[pl.program_id(0), 1]
    scale = jnp.where(sid == 1, M_DQK ** -0.5, jnp.where(sid == 2, F_HEAD_DIM ** -0.5 * LOG2E, 1.0))
    o_ref[...] = (w_ref[...] * scale).astype(bf16)


def _regroup_w_in_t(w_t, offs):
    d = w_t.shape[1]
    mq, mk, mv, mo, mi, mf, fq, fk, fv, ff, ga, gb = range(12)
    order = ((fk, 0), (fv, 0), (mq, 0), (mk, 1), (mv, 0), (mo, 0), (ga, 0), (gb, 0), (fq, 2))
    tr = _tile(d, F_WIDTH)
    table = []
    for seg, scale_id in order:
        start, stop = offs[seg], offs[seg + 1]
        assert start % 8 == 0 and (stop - start) % tr == 0
        table += [(r // 8, scale_id) for r in range(start, stop, tr)]
    table = jnp.asarray(table, jnp.int32)
    return pl.pallas_call(
        _regroup_kernel,
        grid_spec=pltpu.PrefetchScalarGridSpec(
            num_scalar_prefetch=1,
            grid=(table.shape[0],),
            in_specs=[pl.BlockSpec((pl.Element(tr), pl.Element(d)),
                                   lambda t, tbl: (pl.multiple_of(tbl[t, 0] * 8, 8), 0))],
            out_specs=pl.BlockSpec((tr, d), lambda t, tbl: (t, 0))),
        out_shape=jax.ShapeDtypeStruct((table.shape[0] * tr, d), bf16),
        compiler_params=_params("parallel"),
        name="regroup_w_in",
    )(table, w_t)


def _prepare_weights(norm_mix, w_in, b_mlstm_i, b_mlstm_f, b_fox_f, norm_mlstm_h, w_branch_a, w_branch_b,
                     w_out, norm_ffn, w_up, w_down, norm_final):
    d = w_in.shape[0]
    sizes = (M_HEADS * M_DQK, M_HEADS * M_DQK, M_WIDTH, M_WIDTH, M_HEADS, M_HEADS,
             F_WIDTH, F_WIDTH, F_WIDTH, F_HEADS, d, d)
    offs = [0]
    for s in sizes:
        offs.append(offs[-1] + s)
    w_t = w_in.T
    return {
        "norm_mix": norm_mix.reshape(1, d),
        "w_in_t": w_t,
        "gate_rows": (offs[4], offs[9]),
        "w_proj_t": _regroup_w_in_t(w_t, offs),
        "gate_col0": 4 * M_WIDTH,
        "mlstm_bias": jnp.concatenate([b_mlstm_i, b_mlstm_f]).astype(f32).reshape(2 * M_HEADS, 1),
        "b_fox_f": b_fox_f.astype(f32),
        "b_fox_f_col": b_fox_f.astype(f32).reshape(F_HEADS, 1),
        "norm_mlstm_h": norm_mlstm_h.reshape(1, M_WIDTH),
        "w_branch_a": w_branch_a.astype(bf16),
        "w_branch_b": w_branch_b.astype(bf16),
        "w_out": w_out.astype(bf16),
        "norm_ffn": norm_ffn.reshape(1, d),
        "w_up": w_up.astype(bf16),
        "w_down": w_down.astype(bf16),
        "norm_final": norm_final.reshape(1, d),
    }


def kernel(x_prompt, x_sample, cache_fox_k, cache_fox_v, cache_fox_logf, state_mlstm_c, state_mlstm_n, state_mlstm_m, norm_mix, w_in, b_mlstm_i, b_mlstm_f, b_fox_f, norm_mlstm_h, w_branch_a, w_branch_b, w_out, norm_ffn, w_up, w_down, norm_final):
    depth = w_in.shape[0]
    assert depth == 1, "the final norm is fused into the layer's FFN kernel"
    w = _prepare_weights(norm_mix[0], w_in[0], b_mlstm_i[0], b_mlstm_f[0], b_fox_f[0], norm_mlstm_h[0],
                         w_branch_a[0], w_branch_b[0], w_out[0], norm_ffn[0], w_up[0], w_down[0], norm_final)
    bp = x_prompt.shape[0]
    fresh = (jnp.zeros((bp, M_HEADS, M_DQK, M_DV), f32), jnp.zeros((bp, M_HEADS, M_DQK), f32),
             jnp.zeros((bp, M_HEADS), f32))
    y_p, st_p = _layer(x_prompt, w, fresh, None)
    y_s, st_s = _layer(x_sample, w, (state_mlstm_c[0], state_mlstm_n[0], state_mlstm_m[0]),
                       (cache_fox_k[0], cache_fox_v[0], cache_fox_logf[0]))
    return (y_p, y_s) + tuple(a[None] for a in st_p) + tuple(a[None] for a in st_s)
```

```python
import functools

import jax
import jax.numpy as jnp
from jax import lax
from jax.experimental import pallas as pl
from jax.experimental.pallas import tpu as pltpu

M_HEADS = 4
M_DQK = 256
M_DV = 256
M_WIDTH = M_HEADS * M_DV
F_HEADS = 8
F_HEAD_DIM = 128
F_WIDTH = F_HEADS * F_HEAD_DIM
EPS = 1e-6
N_GATE_ROWS = 2 * M_HEADS + F_HEADS
LANES = 128
assert F_HEAD_DIM == LANES
MLSTM_BLOCK = 256
ATTN_BLOCK = 512
ATTN_TRIP = 4
ATTN_ROWS = 2
LOG2E = 1.4426950408889634
VMEM_LIMIT_BYTES = 56 * 1024 * 1024

f32 = jnp.float32
bf16 = jnp.bfloat16

_NT = (((1,), (1,)), ((), ()))
_TN = (((0,), (0,)), ((), ()))


def _tile(n, pref):
    t = min(n, pref)
    while n % t:
        t //= 2
    return t


def _params(*sem):
    return pltpu.CompilerParams(dimension_semantics=sem, vmem_limit_bytes=VMEM_LIMIT_BYTES)


def _log_sigmoid(z):
    return jnp.minimum(z, 0.0) - jnp.log1p(jnp.exp(-jnp.abs(z)))


def _cumsum_lanes(x):
    rows, n = x.shape
    pad = -rows % 16
    if rows == 1:
        x16 = jnp.broadcast_to(x, (16, n))
    elif pad == 0:
        x16 = x
    else:
        x16 = jnp.concatenate([x, jnp.zeros((pad, n), f32)], axis=0)
    r = lax.broadcasted_iota(jnp.int32, (n, n), 0)
    c = lax.broadcasted_iota(jnp.int32, (n, n), 1)
    u = jnp.where(r <= c, 1.0, 0.0).astype(bf16)
    hi = x16.astype(bf16)
    rem = x16 - hi.astype(f32)
    mid = rem.astype(bf16)
    lo = (rem - mid.astype(f32)).astype(bf16)
    out = (jnp.dot(hi, u, preferred_element_type=f32) + jnp.dot(mid, u, preferred_element_type=f32)
           + jnp.dot(lo, u, preferred_element_type=f32))
    return out[:rows]


def _norm_kernel(x_ref, g_ref, wm_ref, wf_ref, xn_ref, gt_ref):
    x = x_ref[...]
    y = x * lax.rsqrt(jnp.mean(x * x, axis=-1, keepdims=True) + EPS) * g_ref[...]
    xn = y.astype(bf16)
    xn_ref[...] = xn
    wg = jnp.concatenate([wm_ref[...], wf_ref[...]], axis=0).astype(bf16)
    gt_ref[...] = lax.dot_general(wg, xn, _NT, preferred_element_type=f32)


def _norm_gates(x, g, w_t, mlstm_gate_row, fox_gate_row):
    t, d = x.shape
    tm = _tile(t, 512)
    assert 2 * M_HEADS == F_HEADS and mlstm_gate_row % F_HEADS == 0 and fox_gate_row % F_HEADS == 0
    return pl.pallas_call(
        _norm_kernel,
        grid=(t // tm,),
        in_specs=[pl.BlockSpec((tm, d), lambda i: (i, 0)),
                  pl.BlockSpec((1, d), lambda i: (0, 0)),
                  pl.BlockSpec((F_HEADS, d), lambda i: (mlstm_gate_row // F_HEADS, 0)),
                  pl.BlockSpec((F_HEADS, d), lambda i: (fox_gate_row // F_HEADS, 0))],
        out_specs=[pl.BlockSpec((tm, d), lambda i: (i, 0)),
                   pl.BlockSpec((N_GATE_ROWS, tm), lambda i: (0, i))],
        out_shape=[jax.ShapeDtypeStruct((t, d), bf16), jax.ShapeDtypeStruct((N_GATE_ROWS, t), f32)],
        compiler_params=_params("parallel"),
        name="norm_gates",
    )(x, g, w_t, w_t)


def _mm_kernel(a_ref, w_ref, o_ref):
    o_ref[...] = lax.dot_general(a_ref[...], w_ref[...], _NT, preferred_element_type=f32).astype(o_ref.dtype)


def _matmul_t(a, w_t, row0, n, name):
    t, k = a.shape
    tm = _tile(t, 2048)
    tn = _tile(n, 1024)
    assert row0 % tn == 0
    return pl.pallas_call(
        _mm_kernel,
        grid=(t // tm, n // tn),
        in_specs=[pl.BlockSpec((tm, k), lambda i, j: (i, 0)),
                  pl.BlockSpec((tn, k), lambda i, j: (row0 // tn + j, 0))],
        out_specs=pl.BlockSpec((tm, tn), lambda i, j: (i, j)),
        out_shape=jax.ShapeDtypeStruct((t, n), bf16),
        compiler_params=_params("parallel", "arbitrary"),
        name=name,
    )(a, w_t)


def _mm_heads_kernel(a_ref, w_ref, o16_ref, *rows_ref):
    r = lax.dot_general(a_ref[...], w_ref[...], _NT, preferred_element_type=f32)
    tm = a_ref.shape[0]
    for h in range(F_HEADS):
        head = r[:, h * F_HEAD_DIM:(h + 1) * F_HEAD_DIM]
        o16_ref[h] = head.astype(bf16)
        for ref in rows_ref:
            ref[pl.ds(h, tm, stride=F_HEADS), :] = head


def _matmul_heads_t(a, w_t, row0, name, *, state_rows):
    t, k = a.shape
    assert row0 % F_WIDTH == 0
    tm = _tile(t, 1024)
    out_specs = [pl.BlockSpec((F_HEADS, tm, F_HEAD_DIM), lambda i: (0, i, 0))]
    out_shape = [jax.ShapeDtypeStruct((F_HEADS, t, F_HEAD_DIM), bf16)]
    if state_rows:
        out_specs.append(pl.BlockSpec((tm * F_HEADS, F_HEAD_DIM), lambda i: (i, 0)))
        out_shape.append(jax.ShapeDtypeStruct((t * F_HEADS, F_HEAD_DIM), f32))
    return pl.pallas_call(
        _mm_heads_kernel,
        grid=(t // tm,),
        in_specs=[pl.BlockSpec((tm, k), lambda i: (i, 0)),
                  pl.BlockSpec((F_WIDTH, k), lambda i: (row0 // F_WIDTH, 0))],
        out_specs=out_specs,
        out_shape=out_shape,
        compiler_params=_params("parallel"),
        name=name,
    )(a, w_t)


def _scan_kernel(x_ref, bias_ref, logf_ref, cum_ref, carry_ref, *, apply_log_sigmoid):
    @pl.when(pl.program_id(1) == 0)
    def _():
        carry_ref[...] = jnp.zeros_like(carry_ref)

    x = x_ref[...]
    if apply_log_sigmoid:
        x = _log_sigmoid(x + bias_ref[...])
    logf_ref[...] = x
    cum = _cumsum_lanes(x) + carry_ref[:, :1]
    cum_ref[...] = cum * LOG2E
    carry_ref[...] = jnp.broadcast_to(cum[:, -1:], carry_ref.shape)


def _scan_rows(x, bias, *, row_block, rows, n_streams, apply_log_sigmoid):
    total = x.shape[1]
    s = total // n_streams
    tb = _tile(s, 512)
    nb = s // tb
    spec = pl.BlockSpec((rows, tb), lambda b, j: (0, b * nb + j))
    return pl.pallas_call(
        functools.partial(_scan_kernel, apply_log_sigmoid=apply_log_sigmoid),
        grid=(n_streams, nb),
        in_specs=[pl.BlockSpec((rows, tb), lambda b, j: (row_block, b * nb + j)),
                  pl.BlockSpec((rows, 1), lambda b, j: (0, 0))],
        out_specs=[spec, spec],
        out_shape=[jax.ShapeDtypeStruct((rows, total), f32)] * 2,
        scratch_shapes=[pltpu.VMEM((rows, LANES), f32)],
        compiler_params=_params("arbitrary", "arbitrary"),
        name="logf_scan",
    )(x, bias)


def _mlstm_kernel(bias_ref, q_ref, k_ref, v_ref, o_ref, gates_ref, nh_ref, c0_ref, n0_ref, m0_ref,
                  h_ref, c_out_ref, n_out_ref, m_out_ref, caug_ref, m_ref):
    blk = pl.program_id(1)
    n_blk = pl.num_programs(1)
    L = q_ref.shape[0]

    @pl.when(blk == 0)
    def _():
        lane = lax.broadcasted_iota(jnp.int32, (M_HEADS, M_DQK, LANES), 2)
        caug_ref[:, :, :M_DV] = c0_ref[...]
        caug_ref[:, :, M_DV:] = jnp.where(lane == 0, n0_ref[...], 0.0)
        m_ref[...] = m0_ref[...]

    gates = gates_ref[...]
    ig_all = gates[:M_HEADS] + bias_ref[:M_HEADS]
    lf_all = _log_sigmoid(gates[M_HEADS:2 * M_HEADS] + bias_ref[M_HEADS:])
    a_all = ig_all - _cumsum_lanes(lf_all)

    t_idx = lax.broadcasted_iota(jnp.int32, (L, L), 0)
    s_idx = lax.broadcasted_iota(jnp.int32, (L, L), 1)
    causal = s_idx <= t_idx
    diag = s_idx == t_idx
    ones_col = jnp.where(lax.broadcasted_iota(jnp.int32, (L, LANES), 1) == 0, 1.0, 0.0).astype(bf16)

    for h in range(M_HEADS):
        cols = slice(h * M_DV, (h + 1) * M_DV)
        m0 = m_ref[h:h + 1, :1]
        ig = ig_all[h:h + 1]
        lf = lf_all[h:h + 1]
        a_row = a_all[h:h + 1]
        a_mat = jnp.where(causal, a_row, -jnp.inf)
        g_col = jnp.maximum(m0, jnp.max(a_mat, axis=1, keepdims=True))
        b_col = jnp.sum(jnp.where(causal, lf, 0.0), axis=1, keepdims=True)
        w_intra = jnp.exp(a_mat - g_col)
        w_inter = jnp.exp(m0 - g_col)

        q = q_ref[:, cols]
        k = k_ref[:, cols]
        v_aug = jnp.concatenate([v_ref[:, cols], ones_col], axis=1)
        c_aug = caug_ref[h]

        s = lax.dot_general(q, k, _NT, preferred_element_type=f32)
        sw = (s * w_intra).astype(bf16)
        num_aug = (jnp.dot(sw, v_aug, preferred_element_type=f32)
                   + w_inter * jnp.dot(q, c_aug.astype(bf16), preferred_element_type=f32))
        num = num_aug[:, :M_DV]
        den = num_aug[:, M_DV:M_DV + 1]
        den = jnp.maximum(jnp.abs(den), jnp.exp(-(b_col + g_col)))
        hh = num / den
        hh = hh * lax.rsqrt(jnp.mean(hh * hh, axis=-1, keepdims=True) + EPS)
        hh = hh * nh_ref[:, cols] * jax.nn.sigmoid(o_ref[:, cols].astype(f32))
        h_ref[:, cols] = hh.astype(h_ref.dtype)

        g_end = jnp.maximum(m0, jnp.max(a_row, axis=1, keepdims=True))
        b_end = jnp.sum(lf, axis=1, keepdims=True)
        ig_col = jnp.sum(jnp.where(diag, ig, 0.0), axis=1, keepdims=True)
        w_tok = jnp.exp(ig_col - b_col - g_end)
        w_state = jnp.exp(m0 - g_end)
        kw = (k.astype(f32) * w_tok).astype(bf16)
        caug_ref[h] = w_state * c_aug + lax.dot_general(kw, v_aug, _TN, preferred_element_type=f32)
        m_ref[h:h + 1, :] = jnp.broadcast_to(b_end + g_end, (1, LANES))

    @pl.when(blk == n_blk - 1)
    def _():
        c_out_ref[...] = caug_ref[:, :, :M_DV]
        n_out_ref[...] = caug_ref[:, :, M_DV:]
        m_out_ref[...] = m_ref[...]


def _mlstm(z, gates_t, gate_bias, norm_h, c0, n0, m0, *, n_streams):
    t = z.shape[0]
    s = t // n_streams
    L = _tile(s, MLSTM_BLOCK)
    nb = s // L
    gates3 = gates_t.reshape(N_GATE_ROWS, n_streams * nb, L).transpose(1, 0, 2)
    n0c = n0.reshape(n_streams, M_HEADS, M_DQK, 1)
    m0b = jnp.broadcast_to(m0.reshape(n_streams, M_HEADS, 1), (n_streams, M_HEADS, LANES))

    def zcols(group):
        return pl.BlockSpec((L, M_WIDTH), lambda b, c: (b * nb + c, group))

    def state(*tail):
        return pl.BlockSpec((None, M_HEADS) + tail, lambda b, c: (b, 0) + (0,) * len(tail))

    return pl.pallas_call(
        _mlstm_kernel,
        grid=(n_streams, nb),
        in_specs=[pl.BlockSpec((2 * M_HEADS, 1), lambda b, c: (0, 0)),
                  zcols(0), zcols(1), zcols(2), zcols(3),
                  pl.BlockSpec((None, N_GATE_ROWS, L), lambda b, c: (b * nb + c, 0, 0)),
                  pl.BlockSpec((1, M_WIDTH), lambda b, c: (0, 0)),
                  state(M_DQK, M_DV), state(M_DQK, 1), state(LANES)],
        out_specs=[pl.BlockSpec((L, M_WIDTH), lambda b, c: (b * nb + c, 0)),
                   state(M_DQK, M_DV), state(M_DQK, LANES), state(LANES)],
        out_shape=[jax.ShapeDtypeStruct((t, M_WIDTH), bf16),
                   jax.ShapeDtypeStruct((n_streams, M_HEADS, M_DQK, M_DV), f32),
                   jax.ShapeDtypeStruct((n_streams, M_HEADS, M_DQK, LANES), f32),
                   jax.ShapeDtypeStruct((n_streams, M_HEADS, LANES), f32)],
        scratch_shapes=[pltpu.VMEM((M_HEADS, M_DQK, M_DV + LANES), f32), pltpu.VMEM((M_HEADS, LANES), f32)],
        compiler_params=_params("parallel", "arbitrary"),
        name="mlstm",
    )(gate_bias, z, z, z, z, gates3, norm_h, c0, n0c, m0b)


def _fox_prompt_kernel(q_ref, k_ref, v_ref, cum_ref, o_ref, *scratch):
    tq = scratch[0].shape[0]
    n_blocks = q_ref.shape[0] // tq
    for r in range(ATTN_ROWS):
        _fox_query_block(pl.program_id(2) * ATTN_ROWS + r, n_blocks, q_ref, k_ref, v_ref, cum_ref,
                         o_ref.at[r * tq:(r + 1) * tq], *scratch)


def _fox_query_block(i, n_blocks, q_ref, k_ref, v_ref, cum_ref, o_ref,
                     sa_ref, sb_ref, pa_ref, pb_ref, xa_ref, xb_ref, m_ref, l_ref, acc_ref):
    tq = o_ref.shape[0]
    tk = sa_ref.shape[1]
    assert tq == tk

    def q_rows(r):
        return q_ref[pl.ds(pl.multiple_of(r * tq, tq), tq), :]

    def k_rows(ref, j):
        return ref[pl.ds(pl.multiple_of(j * tk, tk), tk), :]

    def causal(s, j):
        ahead = (lax.broadcasted_iota(jnp.int32, (tq, tk), 1) - lax.broadcasted_iota(jnp.int32, (tq, tk), 0))
        return jnp.where(ahead <= i * tq - j * tk, s, -jnp.inf)

    def scores(j, s_ref, x_ref, masked, q_row=i):
        s = lax.dot_general(q_rows(q_row), k_rows(k_ref, j), _NT, preferred_element_type=f32) - cum_ref[j]
        if masked:
            s = causal(s, j)
        s_ref[...] = s
        x_ref[...] = jnp.broadcast_to(jnp.max(s, axis=1, keepdims=True), x_ref.shape)

    def values(j, p_ref):
        return jnp.dot(p_ref[...], k_rows(v_ref, j), preferred_element_type=f32)

    def softmax(s_ref, x_ref, p_ref, pv_prev, mask_block=None):
        s = s_ref[...]
        if mask_block is None:
            x = x_ref[...]
        else:
            s = causal(s, mask_block)
            x = jnp.max(s, axis=1, keepdims=True)
        m_old = m_ref[...]
        m_new = jnp.maximum(m_old, x)
        alpha = jnp.exp2(m_old - m_new)
        p = jnp.exp2(s - jnp.concatenate([m_new] * (tk // LANES), axis=1))
        p_ref[...] = p.astype(bf16)
        l_ref[...] = alpha * l_ref[...] + jnp.sum(p, axis=1, keepdims=True)
        acc_ref[...] = alpha * (acc_ref[...] + pv_prev)
        m_ref[...] = m_new

    m_ref[...] = jnp.full(m_ref.shape, -jnp.inf, f32)
    l_ref[...] = jnp.zeros(l_ref.shape, f32)
    acc_ref[...] = jnp.zeros(acc_ref.shape, f32)
    pb_ref[...] = jnp.zeros(pb_ref.shape, bf16)

    @pl.when(i == 0)
    def _():
        scores(0, sa_ref, xa_ref, False)

    def pair(j0):
        scores(j0 + 1, sb_ref, xb_ref, False)
        softmax(sa_ref, xa_ref, pa_ref, values(jnp.maximum(j0 - 1, 0), pb_ref))
        scores(j0 + 2, sa_ref, xa_ref, False)
        softmax(sb_ref, xb_ref, pb_ref, values(j0, pa_ref))

    def long_trip(jj, carry):
        for u in range(0, ATTN_TRIP, 2):
            pair(ATTN_TRIP * jj + u)
        return carry

    def short_trip(jj, carry):
        pair(ATTN_TRIP * (i // ATTN_TRIP) + 2 * jj)
        return carry

    lax.fori_loop(0, i // ATTN_TRIP, long_trip, 0)
    lax.fori_loop(0, (i % ATTN_TRIP) // 2, short_trip, 0)

    def scores_of_next_row():
        scores(0, sa_ref, xa_ref, False, q_row=jnp.minimum(i + 1, n_blocks - 1))

    @pl.when(i % 2 == 0)
    def _():
        softmax(sa_ref, xa_ref, pa_ref, values(jnp.maximum(i - 1, 0), pb_ref), mask_block=i)
        scores_of_next_row()
        acc_ref[...] += values(i, pa_ref)

    @pl.when(i % 2 == 1)
    def _():
        scores(i, sb_ref, xb_ref, True)
        softmax(sa_ref, xa_ref, pa_ref, values(jnp.maximum(i - 2, 0), pb_ref))
        scores_of_next_row()
        softmax(sb_ref, xb_ref, pb_ref, values(i - 1, pa_ref))
        acc_ref[...] += values(i, pb_ref)

    o_ref[...] = (acc_ref[...] / l_ref[...]).astype(o_ref.dtype)


def _fox_prompt(q, k, v, cum_t, *, n_streams):
    t = q.shape[1]
    s = t // n_streams
    tq = tk = _tile(s, ATTN_BLOCK)
    nk = s // tk
    assert nk % ATTN_ROWS == 0
    steps = nk // ATTN_ROWS
    cum4 = cum_t.reshape(F_HEADS, n_streams * nk, 1, tk)
    head_rows = pl.BlockSpec((None, s, F_HEAD_DIM), lambda b, h, i: (h, b, 0))
    return pl.pallas_call(
        _fox_prompt_kernel,
        grid=(n_streams, F_HEADS, steps),
        in_specs=[head_rows, head_rows, head_rows,
                  pl.BlockSpec((None, nk, 1, tk), lambda b, h, i: (h, b, 0, 0))],
        out_specs=pl.BlockSpec((ATTN_ROWS * tq, F_HEAD_DIM), lambda b, h, i: (b * steps + i, h)),
        out_shape=jax.ShapeDtypeStruct((t, F_WIDTH), bf16),
        scratch_shapes=[pltpu.VMEM((tq, tk), f32)] * 2 + [pltpu.VMEM((tq, tk), bf16)] * 2
        + [pltpu.VMEM((tq, LANES), f32)] * 5,
        compiler_params=_params("parallel", "parallel", "arbitrary"),
        name="fox_prompt",
    )(q, k, v, cum4)


def _fox_decode_kernel(bias_ref, q_ref, kn_ref, vn_ref, kc_ref, vc_ref, cumc_ref, fg_ref, o_ref, logf_ref):
    head = pl.program_id(1)
    L = q_ref.shape[0]
    q = q_ref[...]
    logf = _log_sigmoid(fg_ref[...] + bias_ref[head])
    logf_ref[...] = logf
    cum_new = _cumsum_lanes(logf) * LOG2E
    cum_c = cumc_ref[...]
    cum_c = cum_c - cum_c[:, -1:]

    p_len = kc_ref.shape[0] // F_HEADS
    kc = kc_ref[pl.ds(head, p_len, stride=F_HEADS), :].astype(bf16)
    vc = vc_ref[pl.ds(head, p_len, stride=F_HEADS), :].astype(bf16)
    s_c = lax.dot_general(q, kc, _NT, preferred_element_type=f32) - cum_c
    s_n = lax.dot_general(q, kn_ref[...], _NT, preferred_element_type=f32) - cum_new
    row = lax.broadcasted_iota(jnp.int32, (L, L), 0)
    col = lax.broadcasted_iota(jnp.int32, (L, L), 1)
    s_n = jnp.where(col <= row, s_n, -jnp.inf)
    m = jnp.maximum(jnp.max(s_c, axis=1, keepdims=True), jnp.max(s_n, axis=1, keepdims=True))
    p_c = jnp.exp2(s_c - m)
    p_n = jnp.exp2(s_n - m)
    l = jnp.sum(p_c, axis=1, keepdims=True) + jnp.sum(p_n, axis=1, keepdims=True)
    acc = (jnp.dot(p_c.astype(bf16), vc, preferred_element_type=f32)
           + jnp.dot(p_n.astype(bf16), vn_ref[...], preferred_element_type=f32))
    o_ref[...] = (acc / l).astype(o_ref.dtype)


def _fox_decode(q, k, v, gates_t, bias, cache_k, cache_v, cum_cache, *, n_streams):
    t = q.shape[1]
    L = t // n_streams
    p = cache_k.shape[1] // F_HEADS
    gates4 = gates_t.reshape(N_GATE_ROWS, n_streams, 1, L)
    head_rows = pl.BlockSpec((None, L, F_HEAD_DIM), lambda b, h: (h, b, 0))
    return pl.pallas_call(
        _fox_decode_kernel,
        grid=(n_streams, F_HEADS),
        in_specs=[pl.BlockSpec(memory_space=pltpu.SMEM),
                  head_rows, head_rows, head_rows,
                  pl.BlockSpec((None, p * F_HEADS, F_HEAD_DIM), lambda b, h: (b, 0, 0)),
                  pl.BlockSpec((None, p * F_HEADS, F_HEAD_DIM), lambda b, h: (b, 0, 0)),
                  pl.BlockSpec((None, 1, p), lambda b, h: (b * F_HEADS + h, 0, 0)),
                  pl.BlockSpec((None, None, 1, L), lambda b, h: (2 * M_HEADS + h, b, 0, 0))],
        out_specs=[pl.BlockSpec((L, F_HEAD_DIM), lambda b, h: (b, h)),
                   pl.BlockSpec((None, None, 1, L), lambda b, h: (b, h, 0, 0))],
        out_shape=[jax.ShapeDtypeStruct((t, F_WIDTH), bf16),
                   jax.ShapeDtypeStruct((n_streams, F_HEADS, 1, L), f32)],
        compiler_params=_params("parallel", "parallel"),
        name="fox_decode",
    )(bias, q, k, v, cache_k, cache_v, cum_cache, gates4)


def _merge_kernel(x_ref, ha_ref, hb_ref, ga_ref, gb_ref, wa_ref, wb_ref, wo_ref, g_ref, x1_ref, hn_ref):
    pa = jnp.dot(ha_ref[...], wa_ref[...], preferred_element_type=f32)
    pb = jnp.dot(hb_ref[...], wb_ref[...], preferred_element_type=f32)
    merged = (jax.nn.sigmoid(ga_ref[...].astype(f32)) * pa + jax.nn.sigmoid(gb_ref[...].astype(f32)) * pb)
    x1 = x_ref[...] + jnp.dot(merged.astype(bf16), wo_ref[...], preferred_element_type=f32)
    x1_ref[...] = x1
    hn = x1 * lax.rsqrt(jnp.mean(x1 * x1, axis=-1, keepdims=True) + EPS) * g_ref[...]
    hn_ref[...] = hn.astype(hn_ref.dtype)


def _merge(x, h_a, h_b, z, w_a, w_b, w_o, g, *, gate_col0):
    t, d = x.shape
    tm = _tile(t, 256)
    ga_blk = gate_col0 // d
    row = lambda i: (i, 0)
    fixed = lambda i: (0, 0)
    resident = functools.partial(pl.BlockSpec, index_map=fixed, pipeline_mode=pl.Buffered(1))
    return pl.pallas_call(
        _merge_kernel,
        grid=(t // tm,),
        in_specs=[pl.BlockSpec((tm, d), row),
                  pl.BlockSpec((tm, M_WIDTH), row),
                  pl.BlockSpec((tm, F_WIDTH), row),
                  pl.BlockSpec((tm, d), lambda i: (i, ga_blk)),
                  pl.BlockSpec((tm, d), lambda i: (i, ga_blk + 1)),
                  resident(w_a.shape), resident(w_b.shape), resident(w_o.shape),
                  pl.BlockSpec((1, d), fixed)],
        out_specs=[pl.BlockSpec((tm, d), row), pl.BlockSpec((tm, d), row)],
        out_shape=[jax.ShapeDtypeStruct((t, d), f32), jax.ShapeDtypeStruct((t, d), bf16)],
        compiler_params=_params("parallel"),
        name="merge",
    )(x, h_a, h_b, z, z, w_a, w_b, w_o, g)


def _ffn_kernel(hn_ref, x1_ref, wu_ref, wd_ref, g_ref, y_ref):
    f = pl.program_id(1)

    @pl.when(f == 0)
    def _():
        y_ref[...] = x1_ref[...]

    u = jnp.maximum(jnp.dot(hn_ref[...], wu_ref[...], preferred_element_type=f32), 0.0)
    y_ref[...] += jnp.dot((u * u).astype(bf16), wd_ref[...], preferred_element_type=f32)

    @pl.when(f == pl.num_programs(1) - 1)
    def _():
        x2 = y_ref[...]
        y_ref[...] = x2 * lax.rsqrt(jnp.mean(x2 * x2, axis=-1, keepdims=True) + EPS) * g_ref[...]


def _ffn(hn, x1, w_up, w_down, g):
    t, d = x1.shape
    dff = w_up.shape[1]
    tm = _tile(t, 1024)
    tf = _tile(dff, 512)
    return pl.pallas_call(
        _ffn_kernel,
        grid=(t // tm, dff // tf),
        in_specs=[pl.BlockSpec((tm, d), lambda i, f: (i, 0)),
                  pl.BlockSpec((tm, d), lambda i, f: (i, 0)),
                  pl.BlockSpec((d, tf), lambda i, f: (0, f)),
                  pl.BlockSpec((tf, d), lambda i, f: (f, 0)),
                  pl.BlockSpec((1, d), lambda i, f: (0, 0))],
        out_specs=pl.BlockSpec((tm, d), lambda i, f: (i, 0)),
        out_shape=jax.ShapeDtypeStruct((t, d), f32),
        compiler_params=_params("parallel", "arbitrary"),
        name="ffn",
    )(hn, x1, w_up, w_down, g)


def _layer(x3, w, mstate, fox_past):
    nstr, frames, d = x3.shape
    t = nstr * frames
    x = x3.reshape(t, d)
    xn, gates_t = _norm_gates(x, w["norm_mix"], w["w_in_t"], *w["gate_rows"])
    w_proj_t = w["w_proj_t"]
    n_rest = w_proj_t.shape[0] - 3 * F_WIDTH
    k16, k32 = _matmul_heads_t(xn, w_proj_t, 0, "proj_k", state_rows=True)
    v16, v32 = _matmul_heads_t(xn, w_proj_t, F_WIDTH, "proj_v", state_rows=True)
    z = _matmul_t(xn, w_proj_t, 2 * F_WIDTH, n_rest, "proj_rest")
    (q16,) = _matmul_heads_t(xn, w_proj_t, 2 * F_WIDTH + n_rest, "proj_q", state_rows=False)

    c0, n0, m0 = mstate
    h_a, c_new, n_slab, m_slab = _mlstm(z, gates_t, w["mlstm_bias"], w["norm_mlstm_h"], c0, n0, m0,
                                        n_streams=nstr)
    n_new = n_slab[..., 0]
    m_new = m_slab[:, :, 0]

    if fox_past is None:
        logf_t, cum_t = _scan_rows(gates_t, w["b_fox_f_col"], row_block=1, rows=F_HEADS, n_streams=nstr,
                                   apply_log_sigmoid=True)
        h_b = _fox_prompt(q16, k16, v16, cum_t, n_streams=nstr)
        logf = logf_t.T.reshape(nstr, frames, F_HEADS)
    else:
        ck, cv, clf = fox_past
        p = ck.shape[1]
        clf_t = jnp.transpose(clf, (0, 2, 1)).reshape(nstr * F_HEADS, p)
        _, cum_c = _scan_rows(clf_t, jnp.zeros((nstr * F_HEADS, 1), f32), row_block=0, rows=nstr * F_HEADS,
                              n_streams=1, apply_log_sigmoid=False)
        h_b, logf4 = _fox_decode(q16, k16, v16, gates_t, w["b_fox_f"], ck.reshape(nstr, p * F_HEADS, F_HEAD_DIM),
                                 cv.reshape(nstr, p * F_HEADS, F_HEAD_DIM), cum_c.reshape(nstr * F_HEADS, 1, p),
                                 n_streams=nstr)
        logf = jnp.transpose(logf4[:, :, 0, :], (0, 2, 1))

    x1, hn = _merge(x, h_a, h_b, z, w["w_branch_a"], w["w_branch_b"], w["w_out"], w["norm_ffn"],
                    gate_col0=w["gate_col0"])
    y = _ffn(hn, x1, w["w_up"], w["w_down"], w["norm_final"])

    k_rows = k32.reshape(nstr, frames, F_HEADS, F_HEAD_DIM)
    v_rows = v32.reshape(nstr, frames, F_HEADS, F_HEAD_DIM)
    return y.reshape(nstr, frames, d), (k_rows, v_rows, logf, c_new, n_new, m_new)


def _regroup_kernel(table_ref, w_ref, o_ref):
    sid = table_ref[pl.program_id(0), 1]
    scale = jnp.where(sid == 1, M_DQK ** -0.5, jnp.where(sid == 2, F_HEAD_DIM ** -0.5 * LOG2E, 1.0))
    o_ref[...] = (w_ref[...] * scale).astype(bf16)


def _regroup_w_in_t(w_t, offs):
    d = w_t.shape[1]
    mq, mk, mv, mo, mi, mf, fq, fk, fv, ff, ga, gb = range(12)
    order = ((fk, 0), (fv, 0), (mq, 0), (mk, 1), (mv, 0), (mo, 0), (ga, 0), (gb, 0), (fq, 2))
    tr = _tile(d, F_WIDTH)
    table = []
    for seg, scale_id in order:
        start, stop = offs[seg], offs[seg + 1]
        assert start % 8 == 0 and (stop - start) % tr == 0
        table += [(r // 8, scale_id) for r in range(start, stop, tr)]
    table = jnp.asarray(table, jnp.int32)
    return pl.pallas_call(
        _regroup_kernel,
        grid_spec=pltpu.PrefetchScalarGridSpec(
            num_scalar_prefetch=1,
            grid=(table.shape[0],),
            in_specs=[pl.BlockSpec((pl.Element(tr), pl.Element(d)),
                                   lambda t, tbl: (pl.multiple_of(tbl[t, 0] * 8, 8), 0))],
            out_specs=pl.BlockSpec((tr, d), lambda t, tbl: (t, 0))),
        out_shape=jax.ShapeDtypeStruct((table.shape[0] * tr, d), bf16),
        compiler_params=_params("parallel"),
        name="regroup_w_in",
    )(table, w_t)


def _prepare_weights(norm_mix, w_in, b_mlstm_i, b_mlstm_f, b_fox_f, norm_mlstm_h, w_branch_a, w_branch_b,
                     w_out, norm_ffn, w_up, w_down, norm_final):
    d = w_in.shape[0]
    sizes = (M_HEADS * M_DQK, M_HEADS * M_DQK, M_WIDTH, M_WIDTH, M_HEADS, M_HEADS,
             F_WIDTH, F_WIDTH, F_WIDTH, F_HEADS, d, d)
    offs = [0]
    for s in sizes:
        offs.append(offs[-1] + s)
    w_t = w_in.T
    return {
        "norm_mix": norm_mix.reshape(1, d),
        "w_in_t": w_t,
        "gate_rows": (offs[4], offs[9]),
        "w_proj_t": _regroup_w_in_t(w_t, offs),
        "gate_col0": 4 * M_WIDTH,
        "mlstm_bias": jnp.concatenate([b_mlstm_i, b_mlstm_f]).astype(f32).reshape(2 * M_HEADS, 1),
        "b_fox_f": b_fox_f.astype(f32),
        "b_fox_f_col": b_fox_f.astype(f32).reshape(F_HEADS, 1),
        "norm_mlstm_h": norm_mlstm_h.reshape(1, M_WIDTH),
        "w_branch_a": w_branch_a.astype(bf16),
        "w_branch_b": w_branch_b.astype(bf16),
        "w_out": w_out.astype(bf16),
        "norm_ffn": norm_ffn.reshape(1, d),
        "w_up": w_up.astype(bf16),
        "w_down": w_down.astype(bf16),
        "norm_final": norm_final.reshape(1, d),
    }


def kernel(x_prompt, x_sample, cache_fox_k, cache_fox_v, cache_fox_logf, state_mlstm_c, state_mlstm_n, state_mlstm_m, norm_mix, w_in, b_mlstm_i, b_mlstm_f, b_fox_f, norm_mlstm_h, w_branch_a, w_branch_b, w_out, norm_ffn, w_up, w_down, norm_final):
    depth = w_in.shape[0]
    assert depth == 1, "the final norm is fused into the layer's FFN kernel"
    w = _prepare_weights(norm_mix[0], w_in[0], b_mlstm_i[0], b_mlstm_f[0], b_fox_f[0], norm_mlstm_h[0],
                         w_branch_a[0], w_branch_b[0], w_out[0], norm_ffn[0], w_up[0], w_down[0], norm_final)
    bp = x_prompt.shape[0]
    fresh = (jnp.zeros((bp, M_HEADS, M_DQK, M_DV), f32), jnp.zeros((bp, M_HEADS, M_DQK), f32),
             jnp.zeros((bp, M_HEADS), f32))
    y_p, st_p = _layer(x_prompt, w, fresh, None)
    y_s, st_s = _layer(x_sample, w, (state_mlstm_c[0], state_mlstm_n[0], state_mlstm_m[0]),
                       (cache_fox_k[0], cache_fox_v[0], cache_fox_logf[0]))
    return (y_p, y_s) + tuple(a[None] for a in st_p) + tuple(a[None] for a in st_s)
```

```python
import functools

import jax
import jax.numpy as jnp
from jax import lax
from jax.experimental import pallas as pl
from jax.experimental.pallas import tpu as pltpu

M_HEADS = 4
M_DQK = 256
M_DV = 256
M_WIDTH = M_HEADS * M_DV
F_HEADS = 8
F_HEAD_DIM = 128
F_WIDTH = F_HEADS * F_HEAD_DIM
EPS = 1e-6
N_GATE_ROWS = 2 * M_HEADS + F_HEADS
LANES = 128
assert F_HEAD_DIM == LANES
MLSTM_BLOCK = 256
ATTN_BLOCK = 512
ATTN_TRIP = 4
LOG2E = 1.4426950408889634
VMEM_LIMIT_BYTES = 56 * 1024 * 1024

f32 = jnp.float32
bf16 = jnp.bfloat16

_NT = (((1,), (1,)), ((), ()))
_TN = (((0,), (0,)), ((), ()))


def _tile(n, pref):
    t = min(n, pref)
    while n % t:
        t //= 2
    return t


def _params(*sem):
    return pltpu.CompilerParams(dimension_semantics=sem, vmem_limit_bytes=VMEM_LIMIT_BYTES)


def _log_sigmoid(z):
    return jnp.minimum(z, 0.0) - jnp.log1p(jnp.exp(-jnp.abs(z)))


def _cumsum_lanes(x):
    rows, n = x.shape
    pad = -rows % 16
    if rows == 1:
        x16 = jnp.broadcast_to(x, (16, n))
    elif pad == 0:
        x16 = x
    else:
        x16 = jnp.concatenate([x, jnp.zeros((pad, n), f32)], axis=0)
    r = lax.broadcasted_iota(jnp.int32, (n, n), 0)
    c = lax.broadcasted_iota(jnp.int32, (n, n), 1)
    u = jnp.where(r <= c, 1.0, 0.0).astype(bf16)
    hi = x16.astype(bf16)
    rem = x16 - hi.astype(f32)
    mid = rem.astype(bf16)
    lo = (rem - mid.astype(f32)).astype(bf16)
    out = (jnp.dot(hi, u, preferred_element_type=f32) + jnp.dot(mid, u, preferred_element_type=f32)
           + jnp.dot(lo, u, preferred_element_type=f32))
    return out[:rows]


def _norm_kernel(x_ref, g_ref, wm_ref, wf_ref, xn_ref, gt_ref):
    x = x_ref[...]
    y = x * lax.rsqrt(jnp.mean(x * x, axis=-1, keepdims=True) + EPS) * g_ref[...]
    xn = y.astype(bf16)
    xn_ref[...] = xn
    wg = jnp.concatenate([wm_ref[...], wf_ref[...]], axis=0).astype(bf16)
    gt_ref[...] = lax.dot_general(wg, xn, _NT, preferred_element_type=f32)


def _norm_gates(x, g, w_t, mlstm_gate_row, fox_gate_row):
    t, d = x.shape
    tm = _tile(t, 512)
    assert 2 * M_HEADS == F_HEADS and mlstm_gate_row % F_HEADS == 0 and fox_gate_row % F_HEADS == 0
    return pl.pallas_call(
        _norm_kernel,
        grid=(t // tm,),
        in_specs=[pl.BlockSpec((tm, d), lambda i: (i, 0)),
                  pl.BlockSpec((1, d), lambda i: (0, 0)),
                  pl.BlockSpec((F_HEADS, d), lambda i: (mlstm_gate_row // F_HEADS, 0)),
                  pl.BlockSpec((F_HEADS, d), lambda i: (fox_gate_row // F_HEADS, 0))],
        out_specs=[pl.BlockSpec((tm, d), lambda i: (i, 0)),
                   pl.BlockSpec((N_GATE_ROWS, tm), lambda i: (0, i))],
        out_shape=[jax.ShapeDtypeStruct((t, d), bf16), jax.ShapeDtypeStruct((N_GATE_ROWS, t), f32)],
        compiler_params=_params("parallel"),
        name="norm_gates",
    )(x, g, w_t, w_t)


def _mm_kernel(a_ref, w_ref, o_ref):
    o_ref[...] = lax.dot_general(a_ref[...], w_ref[...], _NT, preferred_element_type=f32).astype(o_ref.dtype)


def _matmul_t(a, w_t, row0, n, name):
    t, k = a.shape
    tm = _tile(t, 2048)
    tn = _tile(n, 1024)
    assert row0 % tn == 0
    return pl.pallas_call(
        _mm_kernel,
        grid=(t // tm, n // tn),
        in_specs=[pl.BlockSpec((tm, k), lambda i, j: (i, 0)),
                  pl.BlockSpec((tn, k), lambda i, j: (row0 // tn + j, 0))],
        out_specs=pl.BlockSpec((tm, tn), lambda i, j: (i, j)),
        out_shape=jax.ShapeDtypeStruct((t, n), bf16),
        compiler_params=_params("parallel", "arbitrary"),
        name=name,
    )(a, w_t)


def _mm_heads_kernel(a_ref, w_ref, o16_ref, *rows_ref):
    r = lax.dot_general(a_ref[...], w_ref[...], _NT, preferred_element_type=f32)
    tm = a_ref.shape[0]
    for h in range(F_HEADS):
        head = r[:, h * F_HEAD_DIM:(h + 1) * F_HEAD_DIM]
        o16_ref[h] = head.astype(bf16)
        for ref in rows_ref:
            ref[pl.ds(h, tm, stride=F_HEADS), :] = head


def _matmul_heads_t(a, w_t, row0, name, *, state_rows):
    t, k = a.shape
    assert row0 % F_WIDTH == 0
    tm = _tile(t, 1024)
    out_specs = [pl.BlockSpec((F_HEADS, tm, F_HEAD_DIM), lambda i: (0, i, 0))]
    out_shape = [jax.ShapeDtypeStruct((F_HEADS, t, F_HEAD_DIM), bf16)]
    if state_rows:
        out_specs.append(pl.BlockSpec((tm * F_HEADS, F_HEAD_DIM), lambda i: (i, 0)))
        out_shape.append(jax.ShapeDtypeStruct((t * F_HEADS, F_HEAD_DIM), f32))
    return pl.pallas_call(
        _mm_heads_kernel,
        grid=(t // tm,),
        in_specs=[pl.BlockSpec((tm, k), lambda i: (i, 0)),
                  pl.BlockSpec((F_WIDTH, k), lambda i: (row0 // F_WIDTH, 0))],
        out_specs=out_specs,
        out_shape=out_shape,
        compiler_params=_params("parallel"),
        name=name,
    )(a, w_t)


def _scan_kernel(x_ref, bias_ref, logf_ref, cum_ref, carry_ref, *, apply_log_sigmoid):
    @pl.when(pl.program_id(1) == 0)
    def _():
        carry_ref[...] = jnp.zeros_like(carry_ref)

    x = x_ref[...]
    if apply_log_sigmoid:
        x = _log_sigmoid(x + bias_ref[...])
    logf_ref[...] = x
    cum = _cumsum_lanes(x) + carry_ref[:, :1]
    cum_ref[...] = cum * LOG2E
    carry_ref[...] = jnp.broadcast_to(cum[:, -1:], carry_ref.shape)


def _scan_rows(x, bias, *, row_block, rows, n_streams, apply_log_sigmoid):
    total = x.shape[1]
    s = total // n_streams
    tb = _tile(s, 512)
    nb = s // tb
    spec = pl.BlockSpec((rows, tb), lambda b, j: (0, b * nb + j))
    return pl.pallas_call(
        functools.partial(_scan_kernel, apply_log_sigmoid=apply_log_sigmoid),
        grid=(n_streams, nb),
        in_specs=[pl.BlockSpec((rows, tb), lambda b, j: (row_block, b * nb + j)),
                  pl.BlockSpec((rows, 1), lambda b, j: (0, 0))],
        out_specs=[spec, spec],
        out_shape=[jax.ShapeDtypeStruct((rows, total), f32)] * 2,
        scratch_shapes=[pltpu.VMEM((rows, LANES), f32)],
        compiler_params=_params("arbitrary", "arbitrary"),
        name="logf_scan",
    )(x, bias)


def _mlstm_kernel(bias_ref, q_ref, k_ref, v_ref, o_ref, gates_ref, nh_ref, c0_ref, n0_ref, m0_ref,
                  h_ref, c_out_ref, n_out_ref, m_out_ref, caug_ref, m_ref):
    blk = pl.program_id(1)
    n_blk = pl.num_programs(1)
    L = q_ref.shape[0]

    @pl.when(blk == 0)
    def _():
        lane = lax.broadcasted_iota(jnp.int32, (M_HEADS, M_DQK, LANES), 2)
        caug_ref[:, :, :M_DV] = c0_ref[...]
        caug_ref[:, :, M_DV:] = jnp.where(lane == 0, n0_ref[...], 0.0)
        m_ref[...] = m0_ref[...]

    gates = gates_ref[...]
    ig_all = gates[:M_HEADS] + bias_ref[:M_HEADS]
    lf_all = _log_sigmoid(gates[M_HEADS:2 * M_HEADS] + bias_ref[M_HEADS:])
    a_all = ig_all - _cumsum_lanes(lf_all)

    t_idx = lax.broadcasted_iota(jnp.int32, (L, L), 0)
    s_idx = lax.broadcasted_iota(jnp.int32, (L, L), 1)
    causal = s_idx <= t_idx
    diag = s_idx == t_idx
    ones_col = jnp.where(lax.broadcasted_iota(jnp.int32, (L, LANES), 1) == 0, 1.0, 0.0).astype(bf16)

    for h in range(M_HEADS):
        cols = slice(h * M_DV, (h + 1) * M_DV)
        m0 = m_ref[h:h + 1, :1]
        ig = ig_all[h:h + 1]
        lf = lf_all[h:h + 1]
        a_row = a_all[h:h + 1]
        a_mat = jnp.where(causal, a_row, -jnp.inf)
        g_col = jnp.maximum(m0, jnp.max(a_mat, axis=1, keepdims=True))
        b_col = jnp.sum(jnp.where(causal, lf, 0.0), axis=1, keepdims=True)
        w_intra = jnp.exp(a_mat - g_col)
        w_inter = jnp.exp(m0 - g_col)

        q = q_ref[:, cols]
        k = k_ref[:, cols]
        v_aug = jnp.concatenate([v_ref[:, cols], ones_col], axis=1)
        c_aug = caug_ref[h]

        s = lax.dot_general(q, k, _NT, preferred_element_type=f32)
        sw = (s * w_intra).astype(bf16)
        num_aug = (jnp.dot(sw, v_aug, preferred_element_type=f32)
                   + w_inter * jnp.dot(q, c_aug.astype(bf16), preferred_element_type=f32))
        num = num_aug[:, :M_DV]
        den = num_aug[:, M_DV:M_DV + 1]
        den = jnp.maximum(jnp.abs(den), jnp.exp(-(b_col + g_col)))
        hh = num / den
        hh = hh * lax.rsqrt(jnp.mean(hh * hh, axis=-1, keepdims=True) + EPS)
        hh = hh * nh_ref[:, cols] * jax.nn.sigmoid(o_ref[:, cols].astype(f32))
        h_ref[:, cols] = hh.astype(h_ref.dtype)

        g_end = jnp.maximum(m0, jnp.max(a_row, axis=1, keepdims=True))
        b_end = jnp.sum(lf, axis=1, keepdims=True)
        ig_col = jnp.sum(jnp.where(diag, ig, 0.0), axis=1, keepdims=True)
        w_tok = jnp.exp(ig_col - b_col - g_end)
        w_state = jnp.exp(m0 - g_end)
        kw = (k.astype(f32) * w_tok).astype(bf16)
        caug_ref[h] = w_state * c_aug + lax.dot_general(kw, v_aug, _TN, preferred_element_type=f32)
        m_ref[h:h + 1, :] = jnp.broadcast_to(b_end + g_end, (1, LANES))

    @pl.when(blk == n_blk - 1)
    def _():
        c_out_ref[...] = caug_ref[:, :, :M_DV]
        n_out_ref[...] = caug_ref[:, :, M_DV:]
        m_out_ref[...] = m_ref[...]


def _mlstm(z, gates_t, gate_bias, norm_h, c0, n0, m0, *, n_streams):
    t = z.shape[0]
    s = t // n_streams
    L = _tile(s, MLSTM_BLOCK)
    nb = s // L
    gates3 = gates_t.reshape(N_GATE_ROWS, n_streams * nb, L).transpose(1, 0, 2)
    n0c = n0.reshape(n_streams, M_HEADS, M_DQK, 1)
    m0b = jnp.broadcast_to(m0.reshape(n_streams, M_HEADS, 1), (n_streams, M_HEADS, LANES))

    def zcols(group):
        return pl.BlockSpec((L, M_WIDTH), lambda b, c: (b * nb + c, group))

    def state(*tail):
        return pl.BlockSpec((None, M_HEADS) + tail, lambda b, c: (b, 0) + (0,) * len(tail))

    return pl.pallas_call(
        _mlstm_kernel,
        grid=(n_streams, nb),
        in_specs=[pl.BlockSpec((2 * M_HEADS, 1), lambda b, c: (0, 0)),
                  zcols(0), zcols(1), zcols(2), zcols(3),
                  pl.BlockSpec((None, N_GATE_ROWS, L), lambda b, c: (b * nb + c, 0, 0)),
                  pl.BlockSpec((1, M_WIDTH), lambda b, c: (0, 0)),
                  state(M_DQK, M_DV), state(M_DQK, 1), state(LANES)],
        out_specs=[pl.BlockSpec((L, M_WIDTH), lambda b, c: (b * nb + c, 0)),
                   state(M_DQK, M_DV), state(M_DQK, LANES), state(LANES)],
        out_shape=[jax.ShapeDtypeStruct((t, M_WIDTH), bf16),
                   jax.ShapeDtypeStruct((n_streams, M_HEADS, M_DQK, M_DV), f32),
                   jax.ShapeDtypeStruct((n_streams, M_HEADS, M_DQK, LANES), f32),
                   jax.ShapeDtypeStruct((n_streams, M_HEADS, LANES), f32)],
        scratch_shapes=[pltpu.VMEM((M_HEADS, M_DQK, M_DV + LANES), f32), pltpu.VMEM((M_HEADS, LANES), f32)],
        compiler_params=_params("parallel", "arbitrary"),
        name="mlstm",
    )(gate_bias, z, z, z, z, gates3, norm_h, c0, n0c, m0b)


def _fox_prompt_kernel(q_ref, k_ref, v_ref, cum_ref, o_ref,
                       sa_ref, sb_ref, pa_ref, pb_ref, xa_ref, xb_ref, m_ref, l_ref, acc_ref):
    i = pl.program_id(2)
    tq = o_ref.shape[0]
    tk = sa_ref.shape[1]
    assert tq == tk

    def q_rows(r):
        return q_ref[pl.ds(pl.multiple_of(r * tq, tq), tq), :]

    def k_rows(ref, j):
        return ref[pl.ds(pl.multiple_of(j * tk, tk), tk), :]

    def causal(s, j):
        ahead = (lax.broadcasted_iota(jnp.int32, (tq, tk), 1) - lax.broadcasted_iota(jnp.int32, (tq, tk), 0))
        return jnp.where(ahead <= i * tq - j * tk, s, -jnp.inf)

    def scores(j, s_ref, x_ref, masked, q_row=i):
        s = lax.dot_general(q_rows(q_row), k_rows(k_ref, j), _NT, preferred_element_type=f32) - cum_ref[j]
        if masked:
            s = causal(s, j)
        s_ref[...] = s
        x_ref[...] = jnp.broadcast_to(jnp.max(s, axis=1, keepdims=True), x_ref.shape)

    def values(j, p_ref):
        return jnp.dot(p_ref[...], k_rows(v_ref, j), preferred_element_type=f32)

    def softmax(s_ref, x_ref, p_ref, pv_prev, mask_block=None):
        s = s_ref[...]
        if mask_block is None:
            x = x_ref[...]
        else:
            s = causal(s, mask_block)
            x = jnp.max(s, axis=1, keepdims=True)
        m_old = m_ref[...]
        m_new = jnp.maximum(m_old, x)
        alpha = jnp.exp2(m_old - m_new)
        p = jnp.exp2(s - jnp.concatenate([m_new] * (tk // LANES), axis=1))
        p_ref[...] = p.astype(bf16)
        l_ref[...] = alpha * l_ref[...] + sum(p[:, c:c + LANES] for c in range(0, tk, LANES))
        acc_ref[...] = alpha * (acc_ref[...] + pv_prev)
        m_ref[...] = m_new

    m_ref[...] = jnp.full(m_ref.shape, -jnp.inf, f32)
    l_ref[...] = jnp.zeros(l_ref.shape, f32)
    acc_ref[...] = jnp.zeros(acc_ref.shape, f32)
    pb_ref[...] = jnp.zeros(pb_ref.shape, bf16)

    @pl.when(i == 0)
    def _():
        scores(0, sa_ref, xa_ref, False)

    def pair(j0):
        scores(j0 + 1, sb_ref, xb_ref, False)
        softmax(sa_ref, xa_ref, pa_ref, values(jnp.maximum(j0 - 1, 0), pb_ref))
        scores(j0 + 2, sa_ref, xa_ref, False)
        softmax(sb_ref, xb_ref, pb_ref, values(j0, pa_ref))

    def long_trip(jj, carry):
        for u in range(0, ATTN_TRIP, 2):
            pair(ATTN_TRIP * jj + u)
        return carry

    def short_trip(jj, carry):
        pair(ATTN_TRIP * (i // ATTN_TRIP) + 2 * jj)
        return carry

    lax.fori_loop(0, i // ATTN_TRIP, long_trip, 0)
    lax.fori_loop(0, (i % ATTN_TRIP) // 2, short_trip, 0)

    def scores_of_next_row():
        scores(0, sa_ref, xa_ref, False, q_row=jnp.minimum(i + 1, pl.num_programs(2) - 1))

    @pl.when(i % 2 == 0)
    def _():
        softmax(sa_ref, xa_ref, pa_ref, values(jnp.maximum(i - 1, 0), pb_ref), mask_block=i)
        scores_of_next_row()
        acc_ref[...] += values(i, pa_ref)

    @pl.when(i % 2 == 1)
    def _():
        scores(i, sb_ref, xb_ref, True)
        softmax(sa_ref, xa_ref, pa_ref, values(jnp.maximum(i - 2, 0), pb_ref))
        scores_of_next_row()
        softmax(sb_ref, xb_ref, pb_ref, values(i - 1, pa_ref))
        acc_ref[...] += values(i, pb_ref)

    o_ref[...] = (acc_ref[...] / jnp.sum(l_ref[...], axis=1, keepdims=True)).astype(o_ref.dtype)


def _fox_prompt(q, k, v, cum_t, *, n_streams):
    t = q.shape[1]
    s = t // n_streams
    tq = tk = _tile(s, ATTN_BLOCK)
    nq = nk = s // tk
    cum4 = cum_t.reshape(F_HEADS, n_streams * nk, 1, tk)
    head_rows = pl.BlockSpec((None, s, F_HEAD_DIM), lambda b, h, i: (h, b, 0))
    return pl.pallas_call(
        _fox_prompt_kernel,
        grid=(n_streams, F_HEADS, nq),
        in_specs=[head_rows, head_rows, head_rows,
                  pl.BlockSpec((None, nk, 1, tk), lambda b, h, i: (h, b, 0, 0))],
        out_specs=pl.BlockSpec((tq, F_HEAD_DIM), lambda b, h, i: (b * nq + i, h)),
        out_shape=jax.ShapeDtypeStruct((t, F_WIDTH), bf16),
        scratch_shapes=[pltpu.VMEM((tq, tk), f32)] * 2 + [pltpu.VMEM((tq, tk), bf16)] * 2
        + [pltpu.VMEM((tq, LANES), f32)] * 5,
        compiler_params=_params("parallel", "parallel", "arbitrary"),
        name="fox_prompt",
    )(q, k, v, cum4)


def _fox_decode_kernel(bias_ref, q_ref, kn_ref, vn_ref, kc_ref, vc_ref, cumc_ref, fg_ref, o_ref, logf_ref):
    head = pl.program_id(1)
    L = q_ref.shape[0]
    q = q_ref[...]
    logf = _log_sigmoid(fg_ref[...] + bias_ref[head])
    logf_ref[...] = logf
    cum_new = _cumsum_lanes(logf) * LOG2E
    cum_c = cumc_ref[...]
    cum_c = cum_c - cum_c[:, -1:]

    p_len = kc_ref.shape[0] // F_HEADS
    kc = kc_ref[pl.ds(head, p_len, stride=F_HEADS), :].astype(bf16)
    vc = vc_ref[pl.ds(head, p_len, stride=F_HEADS), :].astype(bf16)
    s_c = lax.dot_general(q, kc, _NT, preferred_element_type=f32) - cum_c
    s_n = lax.dot_general(q, kn_ref[...], _NT, preferred_element_type=f32) - cum_new
    row = lax.broadcasted_iota(jnp.int32, (L, L), 0)
    col = lax.broadcasted_iota(jnp.int32, (L, L), 1)
    s_n = jnp.where(col <= row, s_n, -jnp.inf)
    m = jnp.maximum(jnp.max(s_c, axis=1, keepdims=True), jnp.max(s_n, axis=1, keepdims=True))
    p_c = jnp.exp2(s_c - m)
    p_n = jnp.exp2(s_n - m)
    l = jnp.sum(p_c, axis=1, keepdims=True) + jnp.sum(p_n, axis=1, keepdims=True)
    acc = (jnp.dot(p_c.astype(bf16), vc, preferred_element_type=f32)
           + jnp.dot(p_n.astype(bf16), vn_ref[...], preferred_element_type=f32))
    o_ref[...] = (acc / l).astype(o_ref.dtype)


def _fox_decode(q, k, v, gates_t, bias, cache_k, cache_v, cum_cache, *, n_streams):
    t = q.shape[1]
    L = t // n_streams
    p = cache_k.shape[1] // F_HEADS
    gates4 = gates_t.reshape(N_GATE_ROWS, n_streams, 1, L)
    head_rows = pl.BlockSpec((None, L, F_HEAD_DIM), lambda b, h: (h, b, 0))
    return pl.pallas_call(
        _fox_decode_kernel,
        grid=(n_streams, F_HEADS),
        in_specs=[pl.BlockSpec(memory_space=pltpu.SMEM),
                  head_rows, head_rows, head_rows,
                  pl.BlockSpec((None, p * F_HEADS, F_HEAD_DIM), lambda b, h: (b, 0, 0)),
                  pl.BlockSpec((None, p * F_HEADS, F_HEAD_DIM), lambda b, h: (b, 0, 0)),
                  pl.BlockSpec((None, 1, p), lambda b, h: (b * F_HEADS + h, 0, 0)),
                  pl.BlockSpec((None, None, 1, L), lambda b, h: (2 * M_HEADS + h, b, 0, 0))],
        out_specs=[pl.BlockSpec((L, F_HEAD_DIM), lambda b, h: (b, h)),
                   pl.BlockSpec((None, None, 1, L), lambda b, h: (b, h, 0, 0))],
        out_shape=[jax.ShapeDtypeStruct((t, F_WIDTH), bf16),
                   jax.ShapeDtypeStruct((n_streams, F_HEADS, 1, L), f32)],
        compiler_params=_params("parallel", "parallel"),
        name="fox_decode",
    )(bias, q, k, v, cache_k, cache_v, cum_cache, gates4)


def _merge_kernel(x_ref, ha_ref, hb_ref, ga_ref, gb_ref, wa_ref, wb_ref, wo_ref, g_ref, x1_ref, hn_ref):
    pa = jnp.dot(ha_ref[...], wa_ref[...], preferred_element_type=f32)
    pb = jnp.dot(hb_ref[...], wb_ref[...], preferred_element_type=f32)
    merged = (jax.nn.sigmoid(ga_ref[...].astype(f32)) * pa + jax.nn.sigmoid(gb_ref[...].astype(f32)) * pb)
    x1 = x_ref[...] + jnp.dot(merged.astype(bf16), wo_ref[...], preferred_element_type=f32)
    x1_ref[...] = x1
    hn = x1 * lax.rsqrt(jnp.mean(x1 * x1, axis=-1, keepdims=True) + EPS) * g_ref[...]
    hn_ref[...] = hn.astype(hn_ref.dtype)


def _merge(x, h_a, h_b, z, w_a, w_b, w_o, g, *, gate_col0):
    t, d = x.shape
    tm = _tile(t, 256)
    ga_blk = gate_col0 // d
    row = lambda i: (i, 0)
    fixed = lambda i: (0, 0)
    resident = functools.partial(pl.BlockSpec, index_map=fixed, pipeline_mode=pl.Buffered(1))
    return pl.pallas_call(
        _merge_kernel,
        grid=(t // tm,),
        in_specs=[pl.BlockSpec((tm, d), row),
                  pl.BlockSpec((tm, M_WIDTH), row),
                  pl.BlockSpec((tm, F_WIDTH), row),
                  pl.BlockSpec((tm, d), lambda i: (i, ga_blk)),
                  pl.BlockSpec((tm, d), lambda i: (i, ga_blk + 1)),
                  resident(w_a.shape), resident(w_b.shape), resident(w_o.shape),
                  pl.BlockSpec((1, d), fixed)],
        out_specs=[pl.BlockSpec((tm, d), row), pl.BlockSpec((tm, d), row)],
        out_shape=[jax.ShapeDtypeStruct((t, d), f32), jax.ShapeDtypeStruct((t, d), bf16)],
        compiler_params=_params("parallel"),
        name="merge",
    )(x, h_a, h_b, z, z, w_a, w_b, w_o, g)


def _ffn_kernel(hn_ref, x1_ref, wu_ref, wd_ref, g_ref, y_ref):
    f = pl.program_id(1)

    @pl.when(f == 0)
    def _():
        y_ref[...] = x1_ref[...]

    u = jnp.maximum(jnp.dot(hn_ref[...], wu_ref[...], preferred_element_type=f32), 0.0)
    y_ref[...] += jnp.dot((u * u).astype(bf16), wd_ref[...], preferred_element_type=f32)

    @pl.when(f == pl.num_programs(1) - 1)
    def _():
        x2 = y_ref[...]
        y_ref[...] = x2 * lax.rsqrt(jnp.mean(x2 * x2, axis=-1, keepdims=True) + EPS) * g_ref[...]


def _ffn(hn, x1, w_up, w_down, g):
    t, d = x1.shape
    dff = w_up.shape[1]
    tm = _tile(t, 512)
    tf = _tile(dff, 1024)
    return pl.pallas_call(
        _ffn_kernel,
        grid=(t // tm, dff // tf),
        in_specs=[pl.BlockSpec((tm, d), lambda i, f: (i, 0)),
                  pl.BlockSpec((tm, d), lambda i, f: (i, 0)),
                  pl.BlockSpec((d, tf), lambda i, f: (0, f)),
                  pl.BlockSpec((tf, d), lambda i, f: (f, 0)),
                  pl.BlockSpec((1, d), lambda i, f: (0, 0))],
        out_specs=pl.BlockSpec((tm, d), lambda i, f: (i, 0)),
        out_shape=jax.ShapeDtypeStruct((t, d), f32),
        compiler_params=_params("parallel", "arbitrary"),
        name="ffn",
    )(hn, x1, w_up, w_down, g)


def _layer(x3, w, mstate, fox_past):
    nstr, frames, d = x3.shape
    t = nstr * frames
    x = x3.reshape(t, d)
    xn, gates_t = _norm_gates(x, w["norm_mix"], w["w_in_t"], *w["gate_rows"])
    w_proj_t = w["w_proj_t"]
    n_rest = w_proj_t.shape[0] - 3 * F_WIDTH
    k16, k32 = _matmul_heads_t(xn, w_proj_t, 0, "proj_k", state_rows=True)
    v16, v32 = _matmul_heads_t(xn, w_proj_t, F_WIDTH, "proj_v", state_rows=True)
    z = _matmul_t(xn, w_proj_t, 2 * F_WIDTH, n_rest, "proj_rest")
    (q16,) = _matmul_heads_t(xn, w_proj_t, 2 * F_WIDTH + n_rest, "proj_q", state_rows=False)

    c0, n0, m0 = mstate
    h_a, c_new, n_slab, m_slab = _mlstm(z, gates_t, w["mlstm_bias"], w["norm_mlstm_h"], c0, n0, m0,
                                        n_streams=nstr)
    n_new = n_slab[..., 0]
    m_new = m_slab[:, :, 0]

    if fox_past is None:
        logf_t, cum_t = _scan_rows(gates_t, w["b_fox_f_col"], row_block=1, rows=F_HEADS, n_streams=nstr,
                                   apply_log_sigmoid=True)
        h_b = _fox_prompt(q16, k16, v16, cum_t, n_streams=nstr)
        logf = logf_t.T.reshape(nstr, frames, F_HEADS)
    else:
        ck, cv, clf = fox_past
        p = ck.shape[1]
        clf_t = jnp.transpose(clf, (0, 2, 1)).reshape(nstr * F_HEADS, p)
        _, cum_c = _scan_rows(clf_t, jnp.zeros((nstr * F_HEADS, 1), f32), row_block=0, rows=nstr * F_HEADS,
                              n_streams=1, apply_log_sigmoid=False)
        h_b, logf4 = _fox_decode(q16, k16, v16, gates_t, w["b_fox_f"], ck.reshape(nstr, p * F_HEADS, F_HEAD_DIM),
                                 cv.reshape(nstr, p * F_HEADS, F_HEAD_DIM), cum_c.reshape(nstr * F_HEADS, 1, p),
                                 n_streams=nstr)
        logf = jnp.transpose(logf4[:, :, 0, :], (0, 2, 1))

    x1, hn = _merge(x, h_a, h_b, z, w["w_branch_a"], w["w_branch_b"], w["w_out"], w["norm_ffn"],
                    gate_col0=w["gate_col0"])
    y = _ffn(hn, x1, w["w_up"], w["w_down"], w["norm_final"])

    k_rows = k32.reshape(nstr, frames, F_HEADS, F_HEAD_DIM)
    v_rows = v32.reshape(nstr, frames, F_HEADS, F_HEAD_DIM)
    return y.reshape(nstr, frames, d), (k_rows, v_rows, logf, c_new, n_new, m_new)


def _regroup_kernel(table_ref, w_ref, o_ref):
    sid = table_ref[pl.program_id(0), 1]
    scale = jnp.where(sid == 1, M_DQK ** -0.5, jnp.where(sid == 2, F_HEAD_DIM ** -0.5 * LOG2E, 1.0))
    o_ref[...] = (w_ref[...] * scale).astype(bf16)


def _regroup_w_in_t(w_t, offs):
    d = w_t.shape[1]
    mq, mk, mv, mo, mi, mf, fq, fk, fv, ff, ga, gb = range(12)
    order = ((fk, 0), (fv, 0), (mq, 0), (mk, 1), (mv, 0), (mo, 0), (ga, 0), (gb, 0), (fq, 2))
    tr = _tile(d, F_WIDTH)
    table = []
    for seg, scale_id in order:
        start, stop = offs[seg], offs[seg + 1]
        assert start % 8 == 0 and (stop - start) % tr == 0
        table += [(r // 8, scale_id) for r in range(start, stop, tr)]
    table = jnp.asarray(table, jnp.int32)
    return pl.pallas_call(
        _regroup_kernel,
        grid_spec=pltpu.PrefetchScalarGridSpec(
            num_scalar_prefetch=1,
            grid=(table.shape[0],),
            in_specs=[pl.BlockSpec((pl.Element(tr), pl.Element(d)),
                                   lambda t, tbl: (pl.multiple_of(tbl[t, 0] * 8, 8), 0))],
            out_specs=pl.BlockSpec((tr, d), lambda t, tbl: (t, 0))),
        out_shape=jax.ShapeDtypeStruct((table.shape[0] * tr, d), bf16),
        compiler_params=_params("parallel"),
        name="regroup_w_in",
    )(table, w_t)


def _prepare_weights(norm_mix, w_in, b_mlstm_i, b_mlstm_f, b_fox_f, norm_mlstm_h, w_branch_a, w_branch_b,
                     w_out, norm_ffn, w_up, w_down, norm_final):
    d = w_in.shape[0]
    sizes = (M_HEADS * M_DQK, M_HEADS * M_DQK, M_WIDTH, M_WIDTH, M_HEADS, M_HEADS,
             F_WIDTH, F_WIDTH, F_WIDTH, F_HEADS, d, d)
    offs = [0]
    for s in sizes:
        offs.append(offs[-1] + s)
    w_t = w_in.T
    return {
        "norm_mix": norm_mix.reshape(1, d),
        "w_in_t": w_t,
        "gate_rows": (offs[4], offs[9]),
        "w_proj_t": _regroup_w_in_t(w_t, offs),
        "gate_col0": 4 * M_WIDTH,
        "mlstm_bias": jnp.concatenate([b_mlstm_i, b_mlstm_f]).astype(f32).reshape(2 * M_HEADS, 1),
        "b_fox_f": b_fox_f.astype(f32),
        "b_fox_f_col": b_fox_f.astype(f32).reshape(F_HEADS, 1),
        "norm_mlstm_h": norm_mlstm_h.reshape(1, M_WIDTH),
        "w_branch_a": w_branch_a.astype(bf16),
        "w_branch_b": w_branch_b.astype(bf16),
        "w_out": w_out.astype(bf16),
        "norm_ffn": norm_ffn.reshape(1, d),
        "w_up": w_up.astype(bf16),
        "w_down": w_down.astype(bf16),
        "norm_final": norm_final.reshape(1, d),
    }


def kernel(x_prompt, x_sample, cache_fox_k, cache_fox_v, cache_fox_logf, state_mlstm_c, state_mlstm_n, state_mlstm_m, norm_mix, w_in, b_mlstm_i, b_mlstm_f, b_fox_f, norm_mlstm_h, w_branch_a, w_branch_b, w_out, norm_ffn, w_up, w_down, norm_final):
    depth = w_in.shape[0]
    assert depth == 1, "the final norm is fused into the layer's FFN kernel"
    w = _prepare_weights(norm_mix[0], w_in[0], b_mlstm_i[0], b_mlstm_f[0], b_fox_f[0], norm_mlstm_h[0],
                         w_branch_a[0], w_branch_b[0], w_out[0], norm_ffn[0], w_up[0], w_down[0], norm_final)
    bp = x_prompt.shape[0]
    fresh = (jnp.zeros((bp, M_HEADS, M_DQK, M_DV), f32), jnp.zeros((bp, M_HEADS, M_DQK), f32),
             jnp.zeros((bp, M_HEADS), f32))
    y_p, st_p = _layer(x_prompt, w, fresh, None)
    y_s, st_s = _layer(x_sample, w, (state_mlstm_c[0], state_mlstm_n[0], state_mlstm_m[0]),
                       (cache_fox_k[0], cache_fox_v[0], cache_fox_logf[0]))
    return (y_p, y_s) + tuple(a[None] for a in st_p) + tuple(a[None] for a in st_s)
```

```python
import functools

import jax
import jax.numpy as jnp
from jax import lax
from jax.experimental import pallas as pl
from jax.experimental.pallas import tpu as pltpu

M_HEADS = 4
M_DQK = 256
M_DV = 256
M_WIDTH = M_HEADS * M_DV
F_HEADS = 8
F_HEAD_DIM = 128
F_WIDTH = F_HEADS * F_HEAD_DIM
EPS = 1e-6
N_GATE_ROWS = 2 * M_HEADS + F_HEADS
LANES = 128
assert F_HEAD_DIM == LANES
MLSTM_BLOCK = 256
ATTN_BLOCK = 512
ATTN_TRIP = 6
LOG2E = 1.4426950408889634
VMEM_LIMIT_BYTES = 56 * 1024 * 1024

f32 = jnp.float32
bf16 = jnp.bfloat16

_NT = (((1,), (1,)), ((), ()))
_TN = (((0,), (0,)), ((), ()))


def _tile(n, pref):
    t = min(n, pref)
    while n % t:
        t //= 2
    return t


def _params(*sem):
    return pltpu.CompilerParams(dimension_semantics=sem, vmem_limit_bytes=VMEM_LIMIT_BYTES)


def _log_sigmoid(z):
    return jnp.minimum(z, 0.0) - jnp.log1p(jnp.exp(-jnp.abs(z)))


def _cumsum_lanes(x):
    rows, n = x.shape
    pad = -rows % 16
    if rows == 1:
        x16 = jnp.broadcast_to(x, (16, n))
    elif pad == 0:
        x16 = x
    else:
        x16 = jnp.concatenate([x, jnp.zeros((pad, n), f32)], axis=0)
    r = lax.broadcasted_iota(jnp.int32, (n, n), 0)
    c = lax.broadcasted_iota(jnp.int32, (n, n), 1)
    u = jnp.where(r <= c, 1.0, 0.0).astype(bf16)
    hi = x16.astype(bf16)
    rem = x16 - hi.astype(f32)
    mid = rem.astype(bf16)
    lo = (rem - mid.astype(f32)).astype(bf16)
    out = (jnp.dot(hi, u, preferred_element_type=f32) + jnp.dot(mid, u, preferred_element_type=f32)
           + jnp.dot(lo, u, preferred_element_type=f32))
    return out[:rows]


def _norm_kernel(x_ref, g_ref, wm_ref, wf_ref, xn_ref, gt_ref):
    x = x_ref[...]
    y = x * lax.rsqrt(jnp.mean(x * x, axis=-1, keepdims=True) + EPS) * g_ref[...]
    xn = y.astype(bf16)
    xn_ref[...] = xn
    wg = jnp.concatenate([wm_ref[...], wf_ref[...]], axis=0).astype(bf16)
    gt_ref[...] = lax.dot_general(wg, xn, _NT, preferred_element_type=f32)


def _norm_gates(x, g, w_t, mlstm_gate_row, fox_gate_row):
    t, d = x.shape
    tm = _tile(t, 512)
    assert 2 * M_HEADS == F_HEADS and mlstm_gate_row % F_HEADS == 0 and fox_gate_row % F_HEADS == 0
    return pl.pallas_call(
        _norm_kernel,
        grid=(t // tm,),
        in_specs=[pl.BlockSpec((tm, d), lambda i: (i, 0)),
                  pl.BlockSpec((1, d), lambda i: (0, 0)),
                  pl.BlockSpec((F_HEADS, d), lambda i: (mlstm_gate_row // F_HEADS, 0)),
                  pl.BlockSpec((F_HEADS, d), lambda i: (fox_gate_row // F_HEADS, 0))],
        out_specs=[pl.BlockSpec((tm, d), lambda i: (i, 0)),
                   pl.BlockSpec((N_GATE_ROWS, tm), lambda i: (0, i))],
        out_shape=[jax.ShapeDtypeStruct((t, d), bf16), jax.ShapeDtypeStruct((N_GATE_ROWS, t), f32)],
        compiler_params=_params("parallel"),
        name="norm_gates",
    )(x, g, w_t, w_t)


def _mm_kernel(a_ref, w_ref, o_ref):
    o_ref[...] = lax.dot_general(a_ref[...], w_ref[...], _NT, preferred_element_type=f32).astype(o_ref.dtype)


def _matmul_t(a, w_t, row0, n, name):
    t, k = a.shape
    tm = _tile(t, 2048)
    tn = _tile(n, 1024)
    assert row0 % tn == 0
    return pl.pallas_call(
        _mm_kernel,
        grid=(t // tm, n // tn),
        in_specs=[pl.BlockSpec((tm, k), lambda i, j: (i, 0)),
                  pl.BlockSpec((tn, k), lambda i, j: (row0 // tn + j, 0))],
        out_specs=pl.BlockSpec((tm, tn), lambda i, j: (i, j)),
        out_shape=jax.ShapeDtypeStruct((t, n), bf16),
        compiler_params=_params("parallel", "arbitrary"),
        name=name,
    )(a, w_t)


def _mm_heads_kernel(a_ref, w_ref, o16_ref, *rows_ref):
    r = lax.dot_general(a_ref[...], w_ref[...], _NT, preferred_element_type=f32)
    tm = a_ref.shape[0]
    for h in range(F_HEADS):
        head = r[:, h * F_HEAD_DIM:(h + 1) * F_HEAD_DIM]
        o16_ref[h] = head.astype(bf16)
        for ref in rows_ref:
            ref[pl.ds(h, tm, stride=F_HEADS), :] = head


def _matmul_heads_t(a, w_t, row0, name, *, state_rows):
    t, k = a.shape
    assert row0 % F_WIDTH == 0
    tm = _tile(t, 1024)
    out_specs = [pl.BlockSpec((F_HEADS, tm, F_HEAD_DIM), lambda i: (0, i, 0))]
    out_shape = [jax.ShapeDtypeStruct((F_HEADS, t, F_HEAD_DIM), bf16)]
    if state_rows:
        out_specs.append(pl.BlockSpec((tm * F_HEADS, F_HEAD_DIM), lambda i: (i, 0)))
        out_shape.append(jax.ShapeDtypeStruct((t * F_HEADS, F_HEAD_DIM), f32))
    return pl.pallas_call(
        _mm_heads_kernel,
        grid=(t // tm,),
        in_specs=[pl.BlockSpec((tm, k), lambda i: (i, 0)),
                  pl.BlockSpec((F_WIDTH, k), lambda i: (row0 // F_WIDTH, 0))],
        out_specs=out_specs,
        out_shape=out_shape,
        compiler_params=_params("parallel"),
        name=name,
    )(a, w_t)


def _scan_kernel(x_ref, bias_ref, logf_ref, cum_ref, carry_ref, *, apply_log_sigmoid):
    @pl.when(pl.program_id(1) == 0)
    def _():
        carry_ref[...] = jnp.zeros_like(carry_ref)

    x = x_ref[...]
    if apply_log_sigmoid:
        x = _log_sigmoid(x + bias_ref[...])
    logf_ref[...] = x
    cum = _cumsum_lanes(x) + carry_ref[:, :1]
    cum_ref[...] = cum * LOG2E
    carry_ref[...] = jnp.broadcast_to(cum[:, -1:], carry_ref.shape)


def _scan_rows(x, bias, *, row_block, rows, n_streams, apply_log_sigmoid):
    total = x.shape[1]
    s = total // n_streams
    tb = _tile(s, 512)
    nb = s // tb
    spec = pl.BlockSpec((rows, tb), lambda b, j: (0, b * nb + j))
    return pl.pallas_call(
        functools.partial(_scan_kernel, apply_log_sigmoid=apply_log_sigmoid),
        grid=(n_streams, nb),
        in_specs=[pl.BlockSpec((rows, tb), lambda b, j: (row_block, b * nb + j)),
                  pl.BlockSpec((rows, 1), lambda b, j: (0, 0))],
        out_specs=[spec, spec],
        out_shape=[jax.ShapeDtypeStruct((rows, total), f32)] * 2,
        scratch_shapes=[pltpu.VMEM((rows, LANES), f32)],
        compiler_params=_params("arbitrary", "arbitrary"),
        name="logf_scan",
    )(x, bias)


def _mlstm_kernel(bias_ref, q_ref, k_ref, v_ref, o_ref, gates_ref, nh_ref, c0_ref, n0_ref, m0_ref,
                  h_ref, c_out_ref, n_out_ref, m_out_ref, caug_ref, m_ref):
    blk = pl.program_id(1)
    n_blk = pl.num_programs(1)
    L = q_ref.shape[0]

    @pl.when(blk == 0)
    def _():
        lane = lax.broadcasted_iota(jnp.int32, (M_HEADS, M_DQK, LANES), 2)
        caug_ref[:, :, :M_DV] = c0_ref[...]
        caug_ref[:, :, M_DV:] = jnp.where(lane == 0, n0_ref[...], 0.0)
        m_ref[...] = m0_ref[...]

    gates = gates_ref[...]
    ig_all = gates[:M_HEADS] + bias_ref[:M_HEADS]
    lf_all = _log_sigmoid(gates[M_HEADS:2 * M_HEADS] + bias_ref[M_HEADS:])
    a_all = ig_all - _cumsum_lanes(lf_all)

    t_idx = lax.broadcasted_iota(jnp.int32, (L, L), 0)
    s_idx = lax.broadcasted_iota(jnp.int32, (L, L), 1)
    causal = s_idx <= t_idx
    diag = s_idx == t_idx
    ones_col = jnp.where(lax.broadcasted_iota(jnp.int32, (L, LANES), 1) == 0, 1.0, 0.0).astype(bf16)

    for h in range(M_HEADS):
        cols = slice(h * M_DV, (h + 1) * M_DV)
        m0 = m_ref[h:h + 1, :1]
        ig = ig_all[h:h + 1]
        lf = lf_all[h:h + 1]
        a_row = a_all[h:h + 1]
        a_mat = jnp.where(causal, a_row, -jnp.inf)
        g_col = jnp.maximum(m0, jnp.max(a_mat, axis=1, keepdims=True))
        b_col = jnp.sum(jnp.where(causal, lf, 0.0), axis=1, keepdims=True)
        w_intra = jnp.exp(a_mat - g_col)
        w_inter = jnp.exp(m0 - g_col)

        q = q_ref[:, cols]
        k = k_ref[:, cols]
        v_aug = jnp.concatenate([v_ref[:, cols], ones_col], axis=1)
        c_aug = caug_ref[h]

        s = lax.dot_general(q, k, _NT, preferred_element_type=f32)
        sw = (s * w_intra).astype(bf16)
        num_aug = (jnp.dot(sw, v_aug, preferred_element_type=f32)
                   + w_inter * jnp.dot(q, c_aug.astype(bf16), preferred_element_type=f32))
        num = num_aug[:, :M_DV]
        den = num_aug[:, M_DV:M_DV + 1]
        den = jnp.maximum(jnp.abs(den), jnp.exp(-(b_col + g_col)))
        hh = num / den
        hh = hh * lax.rsqrt(jnp.mean(hh * hh, axis=-1, keepdims=True) + EPS)
        hh = hh * nh_ref[:, cols] * jax.nn.sigmoid(o_ref[:, cols].astype(f32))
        h_ref[:, cols] = hh.astype(h_ref.dtype)

        g_end = jnp.maximum(m0, jnp.max(a_row, axis=1, keepdims=True))
        b_end = jnp.sum(lf, axis=1, keepdims=True)
        ig_col = jnp.sum(jnp.where(diag, ig, 0.0), axis=1, keepdims=True)
        w_tok = jnp.exp(ig_col - b_col - g_end)
        w_state = jnp.exp(m0 - g_end)
        kw = (k.astype(f32) * w_tok).astype(bf16)
        caug_ref[h] = w_state * c_aug + lax.dot_general(kw, v_aug, _TN, preferred_element_type=f32)
        m_ref[h:h + 1, :] = jnp.broadcast_to(b_end + g_end, (1, LANES))

    @pl.when(blk == n_blk - 1)
    def _():
        c_out_ref[...] = caug_ref[:, :, :M_DV]
        n_out_ref[...] = caug_ref[:, :, M_DV:]
        m_out_ref[...] = m_ref[...]


def _mlstm(z, gates_t, gate_bias, norm_h, c0, n0, m0, *, n_streams):
    t = z.shape[0]
    s = t // n_streams
    L = _tile(s, MLSTM_BLOCK)
    nb = s // L
    gates3 = gates_t.reshape(N_GATE_ROWS, n_streams * nb, L).transpose(1, 0, 2)
    n0c = n0.reshape(n_streams, M_HEADS, M_DQK, 1)
    m0b = jnp.broadcast_to(m0.reshape(n_streams, M_HEADS, 1), (n_streams, M_HEADS, LANES))

    def zcols(group):
        return pl.BlockSpec((L, M_WIDTH), lambda b, c: (b * nb + c, group))

    def state(*tail):
        return pl.BlockSpec((None, M_HEADS) + tail, lambda b, c: (b, 0) + (0,) * len(tail))

    return pl.pallas_call(
        _mlstm_kernel,
        grid=(n_streams, nb),
        in_specs=[pl.BlockSpec((2 * M_HEADS, 1), lambda b, c: (0, 0)),
                  zcols(0), zcols(1), zcols(2), zcols(3),
                  pl.BlockSpec((None, N_GATE_ROWS, L), lambda b, c: (b * nb + c, 0, 0)),
                  pl.BlockSpec((1, M_WIDTH), lambda b, c: (0, 0)),
                  state(M_DQK, M_DV), state(M_DQK, 1), state(LANES)],
        out_specs=[pl.BlockSpec((L, M_WIDTH), lambda b, c: (b * nb + c, 0)),
                   state(M_DQK, M_DV), state(M_DQK, LANES), state(LANES)],
        out_shape=[jax.ShapeDtypeStruct((t, M_WIDTH), bf16),
                   jax.ShapeDtypeStruct((n_streams, M_HEADS, M_DQK, M_DV), f32),
                   jax.ShapeDtypeStruct((n_streams, M_HEADS, M_DQK, LANES), f32),
                   jax.ShapeDtypeStruct((n_streams, M_HEADS, LANES), f32)],
        scratch_shapes=[pltpu.VMEM((M_HEADS, M_DQK, M_DV + LANES), f32), pltpu.VMEM((M_HEADS, LANES), f32)],
        compiler_params=_params("parallel", "arbitrary"),
        name="mlstm",
    )(gate_bias, z, z, z, z, gates3, norm_h, c0, n0c, m0b)


def _fox_prompt_kernel(q_ref, k_ref, v_ref, cum_ref, o_ref,
                       sa_ref, sb_ref, pa_ref, pb_ref, xa_ref, xb_ref, m_ref, l_ref, acc_ref):
    i = pl.program_id(2)
    tq = o_ref.shape[0]
    tk = sa_ref.shape[1]
    assert tq == tk

    def q_rows(r):
        return q_ref[pl.ds(pl.multiple_of(r * tq, tq), tq), :]

    def k_rows(ref, j):
        return ref[pl.ds(pl.multiple_of(j * tk, tk), tk), :]

    def causal(s, j):
        ahead = (lax.broadcasted_iota(jnp.int32, (tq, tk), 1) - lax.broadcasted_iota(jnp.int32, (tq, tk), 0))
        return jnp.where(ahead <= i * tq - j * tk, s, -jnp.inf)

    def scores(j, s_ref, x_ref, masked, q_row=i):
        s = lax.dot_general(q_rows(q_row), k_rows(k_ref, j), _NT, preferred_element_type=f32) - cum_ref[j]
        if masked:
            s = causal(s, j)
        s_ref[...] = s
        x_ref[...] = jnp.broadcast_to(jnp.max(s, axis=1, keepdims=True), x_ref.shape)

    def values(j, p_ref):
        return jnp.dot(p_ref[...], k_rows(v_ref, j), preferred_element_type=f32)

    def softmax(s_ref, x_ref, p_ref, pv_prev, mask_block=None):
        s = s_ref[...]
        if mask_block is None:
            x = x_ref[...]
        else:
            s = causal(s, mask_block)
            x = jnp.max(s, axis=1, keepdims=True)
        m_old = m_ref[...]
        m_new = jnp.maximum(m_old, x)
        alpha = jnp.exp2(m_old - m_new)
        p = jnp.exp2(s - jnp.concatenate([m_new] * (tk // LANES), axis=1))
        p_ref[...] = p.astype(bf16)
        l_ref[...] = alpha * l_ref[...] + sum(p[:, c:c + LANES] for c in range(0, tk, LANES))
        acc_ref[...] = alpha * (acc_ref[...] + pv_prev)
        m_ref[...] = m_new

    m_ref[...] = jnp.full(m_ref.shape, -jnp.inf, f32)
    l_ref[...] = jnp.zeros(l_ref.shape, f32)
    acc_ref[...] = jnp.zeros(acc_ref.shape, f32)
    pb_ref[...] = jnp.zeros(pb_ref.shape, bf16)

    @pl.when(i == 0)
    def _():
        scores(0, sa_ref, xa_ref, False)

    def pair(j0):
        scores(j0 + 1, sb_ref, xb_ref, False)
        softmax(sa_ref, xa_ref, pa_ref, values(jnp.maximum(j0 - 1, 0), pb_ref))
        scores(j0 + 2, sa_ref, xa_ref, False)
        softmax(sb_ref, xb_ref, pb_ref, values(j0, pa_ref))

    def long_trip(jj, carry):
        for u in range(0, ATTN_TRIP, 2):
            pair(ATTN_TRIP * jj + u)
        return carry

    def short_trip(jj, carry):
        pair(ATTN_TRIP * (i // ATTN_TRIP) + 2 * jj)
        return carry

    lax.fori_loop(0, i // ATTN_TRIP, long_trip, 0)
    lax.fori_loop(0, (i % ATTN_TRIP) // 2, short_trip, 0)

    def scores_of_next_row():
        scores(0, sa_ref, xa_ref, False, q_row=jnp.minimum(i + 1, pl.num_programs(2) - 1))

    @pl.when(i % 2 == 0)
    def _():
        softmax(sa_ref, xa_ref, pa_ref, values(jnp.maximum(i - 1, 0), pb_ref), mask_block=i)
        scores_of_next_row()
        acc_ref[...] += values(i, pa_ref)

    @pl.when(i % 2 == 1)
    def _():
        scores(i, sb_ref, xb_ref, True)
        softmax(sa_ref, xa_ref, pa_ref, values(jnp.maximum(i - 2, 0), pb_ref))
        scores_of_next_row()
        softmax(sb_ref, xb_ref, pb_ref, values(i - 1, pa_ref))
        acc_ref[...] += values(i, pb_ref)

    o_ref[...] = (acc_ref[...] / jnp.sum(l_ref[...], axis=1, keepdims=True)).astype(o_ref.dtype)


def _fox_prompt(q, k, v, cum_t, *, n_streams):
    t = q.shape[1]
    s = t // n_streams
    tq = tk = _tile(s, ATTN_BLOCK)
    nq = nk = s // tk
    cum4 = cum_t.reshape(F_HEADS, n_streams * nk, 1, tk)
    head_rows = pl.BlockSpec((None, s, F_HEAD_DIM), lambda b, h, i: (h, b, 0))
    return pl.pallas_call(
        _fox_prompt_kernel,
        grid=(n_streams, F_HEADS, nq),
        in_specs=[head_rows, head_rows, head_rows,
                  pl.BlockSpec((None, nk, 1, tk), lambda b, h, i: (h, b, 0, 0))],
        out_specs=pl.BlockSpec((tq, F_HEAD_DIM), lambda b, h, i: (b * nq + i, h)),
        out_shape=jax.ShapeDtypeStruct((t, F_WIDTH), bf16),
        scratch_shapes=[pltpu.VMEM((tq, tk), f32)] * 2 + [pltpu.VMEM((tq, tk), bf16)] * 2
        + [pltpu.VMEM((tq, LANES), f32)] * 5,
        compiler_params=_params("parallel", "parallel", "arbitrary"),
        name="fox_prompt",
    )(q, k, v, cum4)


def _fox_decode_kernel(bias_ref, q_ref, kn_ref, vn_ref, kc_ref, vc_ref, cumc_ref, fg_ref, o_ref, logf_ref):
    head = pl.program_id(1)
    L = q_ref.shape[0]
    q = q_ref[...]
    logf = _log_sigmoid(fg_ref[...] + bias_ref[head])
    logf_ref[...] = logf
    cum_new = _cumsum_lanes(logf) * LOG2E
    cum_c = cumc_ref[...]
    cum_c = cum_c - cum_c[:, -1:]

    p_len = kc_ref.shape[0] // F_HEADS
    kc = kc_ref[pl.ds(head, p_len, stride=F_HEADS), :].astype(bf16)
    vc = vc_ref[pl.ds(head, p_len, stride=F_HEADS), :].astype(bf16)
    s_c = lax.dot_general(q, kc, _NT, preferred_element_type=f32) - cum_c
    s_n = lax.dot_general(q, kn_ref[...], _NT, preferred_element_type=f32) - cum_new
    row = lax.broadcasted_iota(jnp.int32, (L, L), 0)
    col = lax.broadcasted_iota(jnp.int32, (L, L), 1)
    s_n = jnp.where(col <= row, s_n, -jnp.inf)
    m = jnp.maximum(jnp.max(s_c, axis=1, keepdims=True), jnp.max(s_n, axis=1, keepdims=True))
    p_c = jnp.exp2(s_c - m)
    p_n = jnp.exp2(s_n - m)
    l = jnp.sum(p_c, axis=1, keepdims=True) + jnp.sum(p_n, axis=1, keepdims=True)
    acc = (jnp.dot(p_c.astype(bf16), vc, preferred_element_type=f32)
           + jnp.dot(p_n.astype(bf16), vn_ref[...], preferred_element_type=f32))
    o_ref[...] = (acc / l).astype(o_ref.dtype)


def _fox_decode(q, k, v, gates_t, bias, cache_k, cache_v, cum_cache, *, n_streams):
    t = q.shape[1]
    L = t // n_streams
    p = cache_k.shape[1] // F_HEADS
    gates4 = gates_t.reshape(N_GATE_ROWS, n_streams, 1, L)
    head_rows = pl.BlockSpec((None, L, F_HEAD_DIM), lambda b, h: (h, b, 0))
    return pl.pallas_call(
        _fox_decode_kernel,
        grid=(n_streams, F_HEADS),
        in_specs=[pl.BlockSpec(memory_space=pltpu.SMEM),
                  head_rows, head_rows, head_rows,
                  pl.BlockSpec((None, p * F_HEADS, F_HEAD_DIM), lambda b, h: (b, 0, 0)),
                  pl.BlockSpec((None, p * F_HEADS, F_HEAD_DIM), lambda b, h: (b, 0, 0)),
                  pl.BlockSpec((None, 1, p), lambda b, h: (b * F_HEADS + h, 0, 0)),
                  pl.BlockSpec((None, None, 1, L), lambda b, h: (2 * M_HEADS + h, b, 0, 0))],
        out_specs=[pl.BlockSpec((L, F_HEAD_DIM), lambda b, h: (b, h)),
                   pl.BlockSpec((None, None, 1, L), lambda b, h: (b, h, 0, 0))],
        out_shape=[jax.ShapeDtypeStruct((t, F_WIDTH), bf16),
                   jax.ShapeDtypeStruct((n_streams, F_HEADS, 1, L), f32)],
        compiler_params=_params("parallel", "parallel"),
        name="fox_decode",
    )(bias, q, k, v, cache_k, cache_v, cum_cache, gates4)


def _merge_kernel(x_ref, ha_ref, hb_ref, ga_ref, gb_ref, wa_ref, wb_ref, wo_ref, g_ref, x1_ref, hn_ref):
    pa = jnp.dot(ha_ref[...], wa_ref[...], preferred_element_type=f32)
    pb = jnp.dot(hb_ref[...], wb_ref[...], preferred_element_type=f32)
    merged = (jax.nn.sigmoid(ga_ref[...].astype(f32)) * pa + jax.nn.sigmoid(gb_ref[...].astype(f32)) * pb)
    x1 = x_ref[...] + jnp.dot(merged.astype(bf16), wo_ref[...], preferred_element_type=f32)
    x1_ref[...] = x1
    hn = x1 * lax.rsqrt(jnp.mean(x1 * x1, axis=-1, keepdims=True) + EPS) * g_ref[...]
    hn_ref[...] = hn.astype(hn_ref.dtype)


def _merge(x, h_a, h_b, z, w_a, w_b, w_o, g, *, gate_col0):
    t, d = x.shape
    tm = _tile(t, 256)
    ga_blk = gate_col0 // d
    row = lambda i: (i, 0)
    fixed = lambda i: (0, 0)
    resident = functools.partial(pl.BlockSpec, index_map=fixed, pipeline_mode=pl.Buffered(1))
    return pl.pallas_call(
        _merge_kernel,
        grid=(t // tm,),
        in_specs=[pl.BlockSpec((tm, d), row),
                  pl.BlockSpec((tm, M_WIDTH), row),
                  pl.BlockSpec((tm, F_WIDTH), row),
                  pl.BlockSpec((tm, d), lambda i: (i, ga_blk)),
                  pl.BlockSpec((tm, d), lambda i: (i, ga_blk + 1)),
                  resident(w_a.shape), resident(w_b.shape), resident(w_o.shape),
                  pl.BlockSpec((1, d), fixed)],
        out_specs=[pl.BlockSpec((tm, d), row), pl.BlockSpec((tm, d), row)],
        out_shape=[jax.ShapeDtypeStruct((t, d), f32), jax.ShapeDtypeStruct((t, d), bf16)],
        compiler_params=_params("parallel"),
        name="merge",
    )(x, h_a, h_b, z, z, w_a, w_b, w_o, g)


def _ffn_kernel(hn_ref, x1_ref, wu_ref, wd_ref, g_ref, y_ref):
    f = pl.program_id(1)

    @pl.when(f == 0)
    def _():
        y_ref[...] = x1_ref[...]

    u = jnp.maximum(jnp.dot(hn_ref[...], wu_ref[...], preferred_element_type=f32), 0.0)
    y_ref[...] += jnp.dot((u * u).astype(bf16), wd_ref[...], preferred_element_type=f32)

    @pl.when(f == pl.num_programs(1) - 1)
    def _():
        x2 = y_ref[...]
        y_ref[...] = x2 * lax.rsqrt(jnp.mean(x2 * x2, axis=-1, keepdims=True) + EPS) * g_ref[...]


def _ffn(hn, x1, w_up, w_down, g):
    t, d = x1.shape
    dff = w_up.shape[1]
    tm = _tile(t, 512)
    tf = _tile(dff, 1024)
    return pl.pallas_call(
        _ffn_kernel,
        grid=(t // tm, dff // tf),
        in_specs=[pl.BlockSpec((tm, d), lambda i, f: (i, 0)),
                  pl.BlockSpec((tm, d), lambda i, f: (i, 0)),
                  pl.BlockSpec((d, tf), lambda i, f: (0, f)),
                  pl.BlockSpec((tf, d), lambda i, f: (f, 0)),
                  pl.BlockSpec((1, d), lambda i, f: (0, 0))],
        out_specs=pl.BlockSpec((tm, d), lambda i, f: (i, 0)),
        out_shape=jax.ShapeDtypeStruct((t, d), f32),
        compiler_params=_params("parallel", "arbitrary"),
        name="ffn",
    )(hn, x1, w_up, w_down, g)


def _layer(x3, w, mstate, fox_past):
    nstr, frames, d = x3.shape
    t = nstr * frames
    x = x3.reshape(t, d)
    xn, gates_t = _norm_gates(x, w["norm_mix"], w["w_in_t"], *w["gate_rows"])
    w_proj_t = w["w_proj_t"]
    n_rest = w_proj_t.shape[0] - 3 * F_WIDTH
    k16, k32 = _matmul_heads_t(xn, w_proj_t, 0, "proj_k", state_rows=True)
    v16, v32 = _matmul_heads_t(xn, w_proj_t, F_WIDTH, "proj_v", state_rows=True)
    z = _matmul_t(xn, w_proj_t, 2 * F_WIDTH, n_rest, "proj_rest")
    (q16,) = _matmul_heads_t(xn, w_proj_t, 2 * F_WIDTH + n_rest, "proj_q", state_rows=False)

    c0, n0, m0 = mstate
    h_a, c_new, n_slab, m_slab = _mlstm(z, gates_t, w["mlstm_bias"], w["norm_mlstm_h"], c0, n0, m0,
                                        n_streams=nstr)
    n_new = n_slab[..., 0]
    m_new = m_slab[:, :, 0]

    if fox_past is None:
        logf_t, cum_t = _scan_rows(gates_t, w["b_fox_f_col"], row_block=1, rows=F_HEADS, n_streams=nstr,
                                   apply_log_sigmoid=True)
        h_b = _fox_prompt(q16, k16, v16, cum_t, n_streams=nstr)
        logf = logf_t.T.reshape(nstr, frames, F_HEADS)
    else:
        ck, cv, clf = fox_past
        p = ck.shape[1]
        clf_t = jnp.transpose(clf, (0, 2, 1)).reshape(nstr * F_HEADS, p)
        _, cum_c = _scan_rows(clf_t, jnp.zeros((nstr * F_HEADS, 1), f32), row_block=0, rows=nstr * F_HEADS,
                              n_streams=1, apply_log_sigmoid=False)
        h_b, logf4 = _fox_decode(q16, k16, v16, gates_t, w["b_fox_f"], ck.reshape(nstr, p * F_HEADS, F_HEAD_DIM),
                                 cv.reshape(nstr, p * F_HEADS, F_HEAD_DIM), cum_c.reshape(nstr * F_HEADS, 1, p),
                                 n_streams=nstr)
        logf = jnp.transpose(logf4[:, :, 0, :], (0, 2, 1))

    x1, hn = _merge(x, h_a, h_b, z, w["w_branch_a"], w["w_branch_b"], w["w_out"], w["norm_ffn"],
                    gate_col0=w["gate_col0"])
    y = _ffn(hn, x1, w["w_up"], w["w_down"], w["norm_final"])

    k_rows = k32.reshape(nstr, frames, F_HEADS, F_HEAD_DIM)
    v_rows = v32.reshape(nstr, frames, F_HEADS, F_HEAD_DIM)
    return y.reshape(nstr, frames, d), (k_rows, v_rows, logf, c_new, n_new, m_new)


def _regroup_kernel(table_ref, w_ref, o_ref):
    sid = table_ref[pl.program_id(0), 1]
    scale = jnp.where(sid == 1, M_DQK ** -0.5, jnp.where(sid == 2, F_HEAD_DIM ** -0.5 * LOG2E, 1.0))
    o_ref[...] = (w_ref[...] * scale).astype(bf16)


def _regroup_w_in_t(w_t, offs):
    d = w_t.shape[1]
    mq, mk, mv, mo, mi, mf, fq, fk, fv, ff, ga, gb = range(12)
    order = ((fk, 0), (fv, 0), (mq, 0), (mk, 1), (mv, 0), (mo, 0), (ga, 0), (gb, 0), (fq, 2))
    tr = _tile(d, F_WIDTH)
    table = []
    for seg, scale_id in order:
        start, stop = offs[seg], offs[seg + 1]
        assert start % 8 == 0 and (stop - start) % tr == 0
        table += [(r // 8, scale_id) for r in range(start, stop, tr)]
    table = jnp.asarray(table, jnp.int32)
    return pl.pallas_call(
        _regroup_kernel,
        grid_spec=pltpu.PrefetchScalarGridSpec(
            num_scalar_prefetch=1,
            grid=(table.shape[0],),
            in_specs=[pl.BlockSpec((pl.Element(tr), pl.Element(d)),
                                   lambda t, tbl: (pl.multiple_of(tbl[t, 0] * 8, 8), 0))],
            out_specs=pl.BlockSpec((tr, d), lambda t, tbl: (t, 0))),
        out_shape=jax.ShapeDtypeStruct((table.shape[0] * tr, d), bf16),
        compiler_params=_params("parallel"),
        name="regroup_w_in",
    )(table, w_t)


def _prepare_weights(norm_mix, w_in, b_mlstm_i, b_mlstm_f, b_fox_f, norm_mlstm_h, w_branch_a, w_branch_b,
                     w_out, norm_ffn, w_up, w_down, norm_final):
    d = w_in.shape[0]
    sizes = (M_HEADS * M_DQK, M_HEADS * M_DQK, M_WIDTH, M_WIDTH, M_HEADS, M_HEADS,
             F_WIDTH, F_WIDTH, F_WIDTH, F_HEADS, d, d)
    offs = [0]
    for s in sizes:
        offs.append(offs[-1] + s)
    w_t = w_in.T
    return {
        "norm_mix": norm_mix.reshape(1, d),
        "w_in_t": w_t,
        "gate_rows": (offs[4], offs[9]),
        "w_proj_t": _regroup_w_in_t(w_t, offs),
        "gate_col0": 4 * M_WIDTH,
        "mlstm_bias": jnp.concatenate([b_mlstm_i, b_mlstm_f]).astype(f32).reshape(2 * M_HEADS, 1),
        "b_fox_f": b_fox_f.astype(f32),
        "b_fox_f_col": b_fox_f.astype(f32).reshape(F_HEADS, 1),
        "norm_mlstm_h": norm_mlstm_h.reshape(1, M_WIDTH),
        "w_branch_a": w_branch_a.astype(bf16),
        "w_branch_b": w_branch_b.astype(bf16),
        "w_out": w_out.astype(bf16),
        "norm_ffn": norm_ffn.reshape(1, d),
        "w_up": w_up.astype(bf16),
        "w_down": w_down.astype(bf16),
        "norm_final": norm_final.reshape(1, d),
    }


def kernel(x_prompt, x_sample, cache_fox_k, cache_fox_v, cache_fox_logf, state_mlstm_c, state_mlstm_n, state_mlstm_m, norm_mix, w_in, b_mlstm_i, b_mlstm_f, b_fox_f, norm_mlstm_h, w_branch_a, w_branch_b, w_out, norm_ffn, w_up, w_down, norm_final):
    depth = w_in.shape[0]
    assert depth == 1, "the final norm is fused into the layer's FFN kernel"
    w = _prepare_weights(norm_mix[0], w_in[0], b_mlstm_i[0], b_mlstm_f[0], b_fox_f[0], norm_mlstm_h[0],
                         w_branch_a[0], w_branch_b[0], w_out[0], norm_ffn[0], w_up[0], w_down[0], norm_final)
    bp = x_prompt.shape[0]
    fresh = (jnp.zeros((bp, M_HEADS, M_DQK, M_DV), f32), jnp.zeros((bp, M_HEADS, M_DQK), f32),
             jnp.zeros((bp, M_HEADS), f32))
    y_p, st_p = _layer(x_prompt, w, fresh, None)
    y_s, st_s = _layer(x_sample, w, (state_mlstm_c[0], state_mlstm_n[0], state_mlstm_m[0]),
                       (cache_fox_k[0], cache_fox_v[0], cache_fox_logf[0]))
    return (y_p, y_s) + tuple(a[None] for a in st_p) + tuple(a[None] for a in st_s)
```

```python
import functools

import jax
import jax.numpy as jnp
from jax import lax
from jax.experimental import pallas as pl
from jax.experimental.pallas import tpu as pltpu

M_HEADS = 4
M_DQK = 256
M_DV = 256
M_WIDTH = M_HEADS * M_DV
F_HEADS = 8
F_HEAD_DIM = 128
F_WIDTH = F_HEADS * F_HEAD_DIM
EPS = 1e-6
N_GATE_ROWS = 2 * M_HEADS + F_HEADS
LANES = 128
assert F_HEAD_DIM == LANES
MLSTM_BLOCK = 256
ATTN_BLOCK = 512
ATTN_TRIP = 6
LOG2E = 1.4426950408889634
VMEM_LIMIT_BYTES = 56 * 1024 * 1024

f32 = jnp.float32
bf16 = jnp.bfloat16

_NT = (((1,), (1,)), ((), ()))
_TN = (((0,), (0,)), ((), ()))


def _tile(n, pref):
    t = min(n, pref)
    while n % t:
        t //= 2
    return t


def _params(*sem):
    return pltpu.CompilerParams(dimension_semantics=sem, vmem_limit_bytes=VMEM_LIMIT_BYTES)


def _log_sigmoid(z):
    return jnp.minimum(z, 0.0) - jnp.log1p(jnp.exp(-jnp.abs(z)))


def _cumsum_lanes(x):
    rows, n = x.shape
    pad = -rows % 16
    if rows == 1:
        x16 = jnp.broadcast_to(x, (16, n))
    elif pad == 0:
        x16 = x
    else:
        x16 = jnp.concatenate([x, jnp.zeros((pad, n), f32)], axis=0)
    r = lax.broadcasted_iota(jnp.int32, (n, n), 0)
    c = lax.broadcasted_iota(jnp.int32, (n, n), 1)
    u = jnp.where(r <= c, 1.0, 0.0).astype(bf16)
    hi = x16.astype(bf16)
    rem = x16 - hi.astype(f32)
    mid = rem.astype(bf16)
    lo = (rem - mid.astype(f32)).astype(bf16)
    out = (jnp.dot(hi, u, preferred_element_type=f32) + jnp.dot(mid, u, preferred_element_type=f32)
           + jnp.dot(lo, u, preferred_element_type=f32))
    return out[:rows]


def _norm_kernel(x_ref, g_ref, wm_ref, wf_ref, xn_ref, gt_ref):
    x = x_ref[...]
    y = x * lax.rsqrt(jnp.mean(x * x, axis=-1, keepdims=True) + EPS) * g_ref[...]
    xn = y.astype(bf16)
    xn_ref[...] = xn
    wg = jnp.concatenate([wm_ref[...], wf_ref[...]], axis=0).astype(bf16)
    gt_ref[...] = lax.dot_general(wg, xn, _NT, preferred_element_type=f32)


def _norm_gates(x, g, w_t, mlstm_gate_row, fox_gate_row):
    t, d = x.shape
    tm = _tile(t, 512)
    assert 2 * M_HEADS == F_HEADS and mlstm_gate_row % F_HEADS == 0 and fox_gate_row % F_HEADS == 0
    return pl.pallas_call(
        _norm_kernel,
        grid=(t // tm,),
        in_specs=[pl.BlockSpec((tm, d), lambda i: (i, 0)),
                  pl.BlockSpec((1, d), lambda i: (0, 0)),
                  pl.BlockSpec((F_HEADS, d), lambda i: (mlstm_gate_row // F_HEADS, 0)),
                  pl.BlockSpec((F_HEADS, d), lambda i: (fox_gate_row // F_HEADS, 0))],
        out_specs=[pl.BlockSpec((tm, d), lambda i: (i, 0)),
                   pl.BlockSpec((N_GATE_ROWS, tm), lambda i: (0, i))],
        out_shape=[jax.ShapeDtypeStruct((t, d), bf16), jax.ShapeDtypeStruct((N_GATE_ROWS, t), f32)],
        compiler_params=_params("parallel"),
        name="norm_gates",
    )(x, g, w_t, w_t)


def _mm_kernel(a_ref, w_ref, o_ref):
    o_ref[...] = lax.dot_general(a_ref[...], w_ref[...], _NT, preferred_element_type=f32).astype(o_ref.dtype)


def _matmul_t(a, w_t, row0, n, name):
    t, k = a.shape
    tm = _tile(t, 2048)
    tn = _tile(n, 1024)
    assert row0 % tn == 0
    return pl.pallas_call(
        _mm_kernel,
        grid=(t // tm, n // tn),
        in_specs=[pl.BlockSpec((tm, k), lambda i, j: (i, 0)),
                  pl.BlockSpec((tn, k), lambda i, j: (row0 // tn + j, 0))],
        out_specs=pl.BlockSpec((tm, tn), lambda i, j: (i, j)),
        out_shape=jax.ShapeDtypeStruct((t, n), bf16),
        compiler_params=_params("parallel", "arbitrary"),
        name=name,
    )(a, w_t)


def _mm_heads_kernel(a_ref, w_ref, o16_ref, *rows_ref):
    r = lax.dot_general(a_ref[...], w_ref[...], _NT, preferred_element_type=f32)
    tm = a_ref.shape[0]
    for h in range(F_HEADS):
        head = r[:, h * F_HEAD_DIM:(h + 1) * F_HEAD_DIM]
        o16_ref[h] = head.astype(bf16)
        for ref in rows_ref:
            ref[pl.ds(h, tm, stride=F_HEADS), :] = head


def _matmul_heads_t(a, w_t, row0, name, *, state_rows):
    t, k = a.shape
    assert row0 % F_WIDTH == 0
    tm = _tile(t, 1024)
    out_specs = [pl.BlockSpec((F_HEADS, tm, F_HEAD_DIM), lambda i: (0, i, 0))]
    out_shape = [jax.ShapeDtypeStruct((F_HEADS, t, F_HEAD_DIM), bf16)]
    if state_rows:
        out_specs.append(pl.BlockSpec((tm * F_HEADS, F_HEAD_DIM), lambda i: (i, 0)))
        out_shape.append(jax.ShapeDtypeStruct((t * F_HEADS, F_HEAD_DIM), f32))
    return pl.pallas_call(
        _mm_heads_kernel,
        grid=(t // tm,),
        in_specs=[pl.BlockSpec((tm, k), lambda i: (i, 0)),
                  pl.BlockSpec((F_WIDTH, k), lambda i: (row0 // F_WIDTH, 0))],
        out_specs=out_specs,
        out_shape=out_shape,
        compiler_params=_params("parallel"),
        name=name,
    )(a, w_t)


def _scan_kernel(x_ref, bias_ref, logf_ref, cum_ref, carry_ref, *, apply_log_sigmoid):
    @pl.when(pl.program_id(1) == 0)
    def _():
        carry_ref[...] = jnp.zeros_like(carry_ref)

    x = x_ref[...]
    if apply_log_sigmoid:
        x = _log_sigmoid(x + bias_ref[...])
    logf_ref[...] = x
    cum = _cumsum_lanes(x) + carry_ref[:, :1]
    cum_ref[...] = cum * LOG2E
    carry_ref[...] = jnp.broadcast_to(cum[:, -1:], carry_ref.shape)


def _scan_rows(x, bias, *, row_block, rows, n_streams, apply_log_sigmoid):
    total = x.shape[1]
    s = total // n_streams
    tb = _tile(s, 512)
    nb = s // tb
    spec = pl.BlockSpec((rows, tb), lambda b, j: (0, b * nb + j))
    return pl.pallas_call(
        functools.partial(_scan_kernel, apply_log_sigmoid=apply_log_sigmoid),
        grid=(n_streams, nb),
        in_specs=[pl.BlockSpec((rows, tb), lambda b, j: (row_block, b * nb + j)),
                  pl.BlockSpec((rows, 1), lambda b, j: (0, 0))],
        out_specs=[spec, spec],
        out_shape=[jax.ShapeDtypeStruct((rows, total), f32)] * 2,
        scratch_shapes=[pltpu.VMEM((rows, LANES), f32)],
        compiler_params=_params("arbitrary", "arbitrary"),
        name="logf_scan",
    )(x, bias)


def _mlstm_kernel(bias_ref, q_ref, k_ref, v_ref, o_ref, gates_ref, nh_ref, c0_ref, n0_ref, m0_ref,
                  h_ref, c_out_ref, n_out_ref, m_out_ref, caug_ref, m_ref):
    blk = pl.program_id(1)
    n_blk = pl.num_programs(1)
    L = q_ref.shape[0]

    @pl.when(blk == 0)
    def _():
        lane = lax.broadcasted_iota(jnp.int32, (M_HEADS, M_DQK, LANES), 2)
        caug_ref[:, :, :M_DV] = c0_ref[...]
        caug_ref[:, :, M_DV:] = jnp.where(lane == 0, n0_ref[...], 0.0)
        m_ref[...] = m0_ref[...]

    gates = gates_ref[...]
    ig_all = gates[:M_HEADS] + bias_ref[:M_HEADS]
    lf_all = _log_sigmoid(gates[M_HEADS:2 * M_HEADS] + bias_ref[M_HEADS:])
    a_all = ig_all - _cumsum_lanes(lf_all)

    t_idx = lax.broadcasted_iota(jnp.int32, (L, L), 0)
    s_idx = lax.broadcasted_iota(jnp.int32, (L, L), 1)
    causal = s_idx <= t_idx
    diag = s_idx == t_idx
    ones_col = jnp.where(lax.broadcasted_iota(jnp.int32, (L, LANES), 1) == 0, 1.0, 0.0).astype(bf16)

    for h in range(M_HEADS):
        cols = slice(h * M_DV, (h + 1) * M_DV)
        m0 = m_ref[h:h + 1, :1]
        ig = ig_all[h:h + 1]
        lf = lf_all[h:h + 1]
        a_row = a_all[h:h + 1]
        a_mat = jnp.where(causal, a_row, -jnp.inf)
        g_col = jnp.maximum(m0, jnp.max(a_mat, axis=1, keepdims=True))
        b_col = jnp.sum(jnp.where(causal, lf, 0.0), axis=1, keepdims=True)
        w_intra = jnp.exp(a_mat - g_col)
        w_inter = jnp.exp(m0 - g_col)

        q = q_ref[:, cols]
        k = k_ref[:, cols]
        v_aug = jnp.concatenate([v_ref[:, cols], ones_col], axis=1)
        c_aug = caug_ref[h]

        s = lax.dot_general(q, k, _NT, preferred_element_type=f32)
        sw = (s * w_intra).astype(bf16)
        num_aug = (jnp.dot(sw, v_aug, preferred_element_type=f32)
                   + w_inter * jnp.dot(q, c_aug.astype(bf16), preferred_element_type=f32))
        num = num_aug[:, :M_DV]
        den = num_aug[:, M_DV:M_DV + 1]
        den = jnp.maximum(jnp.abs(den), jnp.exp(-(b_col + g_col)))
        hh = num / den
        hh = hh * lax.rsqrt(jnp.mean(hh * hh, axis=-1, keepdims=True) + EPS)
        hh = hh * nh_ref[:, cols] * jax.nn.sigmoid(o_ref[:, cols].astype(f32))
        h_ref[:, cols] = hh.astype(h_ref.dtype)

        g_end = jnp.maximum(m0, jnp.max(a_row, axis=1, keepdims=True))
        b_end = jnp.sum(lf, axis=1, keepdims=True)
        ig_col = jnp.sum(jnp.where(diag, ig, 0.0), axis=1, keepdims=True)
        w_tok = jnp.exp(ig_col - b_col - g_end)
        w_state = jnp.exp(m0 - g_end)
        kw = (k.astype(f32) * w_tok).astype(bf16)
        caug_ref[h] = w_state * c_aug + lax.dot_general(kw, v_aug, _TN, preferred_element_type=f32)
        m_ref[h:h + 1, :] = jnp.broadcast_to(b_end + g_end, (1, LANES))

    @pl.when(blk == n_blk - 1)
    def _():
        c_out_ref[...] = caug_ref[:, :, :M_DV]
        n_out_ref[...] = caug_ref[:, :, M_DV:]
        m_out_ref[...] = m_ref[...]


def _mlstm(z, gates_t, gate_bias, norm_h, c0, n0, m0, *, n_streams):
    t = z.shape[0]
    s = t // n_streams
    L = _tile(s, MLSTM_BLOCK)
    nb = s // L
    gates3 = gates_t.reshape(N_GATE_ROWS, n_streams * nb, L).transpose(1, 0, 2)
    n0c = n0.reshape(n_streams, M_HEADS, M_DQK, 1)
    m0b = jnp.broadcast_to(m0.reshape(n_streams, M_HEADS, 1), (n_streams, M_HEADS, LANES))

    def zcols(group):
        return pl.BlockSpec((L, M_WIDTH), lambda b, c: (b * nb + c, group))

    def state(*tail):
        return pl.BlockSpec((None, M_HEADS) + tail, lambda b, c: (b, 0) + (0,) * len(tail))

    return pl.pallas_call(
        _mlstm_kernel,
        grid=(n_streams, nb),
        in_specs=[pl.BlockSpec((2 * M_HEADS, 1), lambda b, c: (0, 0)),
                  zcols(0), zcols(1), zcols(2), zcols(3),
                  pl.BlockSpec((None, N_GATE_ROWS, L), lambda b, c: (b * nb + c, 0, 0)),
                  pl.BlockSpec((1, M_WIDTH), lambda b, c: (0, 0)),
                  state(M_DQK, M_DV), state(M_DQK, 1), state(LANES)],
        out_specs=[pl.BlockSpec((L, M_WIDTH), lambda b, c: (b * nb + c, 0)),
                   state(M_DQK, M_DV), state(M_DQK, LANES), state(LANES)],
        out_shape=[jax.ShapeDtypeStruct((t, M_WIDTH), bf16),
                   jax.ShapeDtypeStruct((n_streams, M_HEADS, M_DQK, M_DV), f32),
                   jax.ShapeDtypeStruct((n_streams, M_HEADS, M_DQK, LANES), f32),
                   jax.ShapeDtypeStruct((n_streams, M_HEADS, LANES), f32)],
        scratch_shapes=[pltpu.VMEM((M_HEADS, M_DQK, M_DV + LANES), f32), pltpu.VMEM((M_HEADS, LANES), f32)],
        compiler_params=_params("parallel", "arbitrary"),
        name="mlstm",
    )(gate_bias, z, z, z, z, gates3, norm_h, c0, n0c, m0b)


def _fox_prompt_kernel(q_ref, k_ref, v_ref, cum_ref, o_ref,
                       sa_ref, sb_ref, pa_ref, pb_ref, xa_ref, xb_ref, m_ref, l_ref, acc_ref):
    i = pl.program_id(2)
    tq = o_ref.shape[0]
    tk = sa_ref.shape[1]
    assert tq == tk

    def q_rows(r):
        return q_ref[pl.ds(pl.multiple_of(r * tq, tq), tq), :]

    def k_rows(ref, j):
        return ref[pl.ds(pl.multiple_of(j * tk, tk), tk), :]

    def causal(s, j):
        ahead = (lax.broadcasted_iota(jnp.int32, (tq, tk), 1) - lax.broadcasted_iota(jnp.int32, (tq, tk), 0))
        return jnp.where(ahead <= i * tq - j * tk, s, -jnp.inf)

    def scores(j, s_ref, x_ref, masked, q_row=i):
        s = lax.dot_general(q_rows(q_row), k_rows(k_ref, j), _NT, preferred_element_type=f32) - cum_ref[j]
        if masked:
            s = causal(s, j)
        s_ref[...] = s
        x_ref[...] = jnp.broadcast_to(jnp.max(s, axis=1, keepdims=True), x_ref.shape)

    def values(j, p_ref):
        return jnp.dot(p_ref[...], k_rows(v_ref, j), preferred_element_type=f32)

    def softmax(s_ref, x_ref, p_ref, pv_prev, mask_block=None):
        s = s_ref[...]
        if mask_block is None:
            x = x_ref[...]
        else:
            s = causal(s, mask_block)
            x = jnp.max(s, axis=1, keepdims=True)
        m_old = m_ref[...]
        m_new = jnp.maximum(m_old, x)
        alpha = jnp.exp2(m_old - m_new)
        p = jnp.exp2((s - jnp.concatenate([m_new] * (tk // LANES), axis=1)).astype(bf16))
        p_ref[...] = p
        l_ref[...] = alpha * l_ref[...] + sum(p[:, c:c + LANES] for c in range(0, tk, LANES)).astype(f32)
        acc_ref[...] = alpha * (acc_ref[...] + pv_prev)
        m_ref[...] = m_new

    m_ref[...] = jnp.full(m_ref.shape, -jnp.inf, f32)
    l_ref[...] = jnp.zeros(l_ref.shape, f32)
    acc_ref[...] = jnp.zeros(acc_ref.shape, f32)
    pb_ref[...] = jnp.zeros(pb_ref.shape, bf16)

    @pl.when(i == 0)
    def _():
        scores(0, sa_ref, xa_ref, False)

    def pair(j0):
        scores(j0 + 1, sb_ref, xb_ref, False)
        softmax(sa_ref, xa_ref, pa_ref, values(jnp.maximum(j0 - 1, 0), pb_ref))
        scores(j0 + 2, sa_ref, xa_ref, False)
        softmax(sb_ref, xb_ref, pb_ref, values(j0, pa_ref))

    def long_trip(jj, carry):
        for u in range(0, ATTN_TRIP, 2):
            pair(ATTN_TRIP * jj + u)
        return carry

    def short_trip(jj, carry):
        pair(ATTN_TRIP * (i // ATTN_TRIP) + 2 * jj)
        return carry

    lax.fori_loop(0, i // ATTN_TRIP, long_trip, 0)
    lax.fori_loop(0, (i % ATTN_TRIP) // 2, short_trip, 0)

    def scores_of_next_row():
        scores(0, sa_ref, xa_ref, False, q_row=jnp.minimum(i + 1, pl.num_programs(2) - 1))

    @pl.when(i % 2 == 0)
    def _():
        softmax(sa_ref, xa_ref, pa_ref, values(jnp.maximum(i - 1, 0), pb_ref), mask_block=i)
        scores_of_next_row()
        acc_ref[...] += values(i, pa_ref)

    @pl.when(i % 2 == 1)
    def _():
        scores(i, sb_ref, xb_ref, True)
        softmax(sa_ref, xa_ref, pa_ref, values(jnp.maximum(i - 2, 0), pb_ref))
        scores_of_next_row()
        softmax(sb_ref, xb_ref, pb_ref, values(i - 1, pa_ref))
        acc_ref[...] += values(i, pb_ref)

    o_ref[...] = (acc_ref[...] / jnp.sum(l_ref[...], axis=1, keepdims=True)).astype(o_ref.dtype)


def _fox_prompt(q, k, v, cum_t, *, n_streams):
    t = q.shape[1]
    s = t // n_streams
    tq = tk = _tile(s, ATTN_BLOCK)
    nq = nk = s // tk
    cum4 = cum_t.reshape(F_HEADS, n_streams * nk, 1, tk)
    head_rows = pl.BlockSpec((None, s, F_HEAD_DIM), lambda b, h, i: (h, b, 0))
    return pl.pallas_call(
        _fox_prompt_kernel,
        grid=(n_streams, F_HEADS, nq),
        in_specs=[head_rows, head_rows, head_rows,
                  pl.BlockSpec((None, nk, 1, tk), lambda b, h, i: (h, b, 0, 0))],
        out_specs=pl.BlockSpec((tq, F_HEAD_DIM), lambda b, h, i: (b * nq + i, h)),
        out_shape=jax.ShapeDtypeStruct((t, F_WIDTH), bf16),
        scratch_shapes=[pltpu.VMEM((tq, tk), f32)] * 2 + [pltpu.VMEM((tq, tk), bf16)] * 2
        + [pltpu.VMEM((tq, LANES), f32)] * 5,
        compiler_params=_params("parallel", "parallel", "arbitrary"),
        name="fox_prompt",
    )(q, k, v, cum4)


def _fox_decode_kernel(bias_ref, q_ref, kn_ref, vn_ref, kc_ref, vc_ref, cumc_ref, fg_ref, o_ref, logf_ref):
    head = pl.program_id(1)
    L = q_ref.shape[0]
    q = q_ref[...]
    logf = _log_sigmoid(fg_ref[...] + bias_ref[head])
    logf_ref[...] = logf
    cum_new = _cumsum_lanes(logf) * LOG2E
    cum_c = cumc_ref[...]
    cum_c = cum_c - cum_c[:, -1:]

    p_len = kc_ref.shape[0] // F_HEADS
    kc = kc_ref[pl.ds(head, p_len, stride=F_HEADS), :].astype(bf16)
    vc = vc_ref[pl.ds(head, p_len, stride=F_HEADS), :].astype(bf16)
    s_c = lax.dot_general(q, kc, _NT, preferred_element_type=f32) - cum_c
    s_n = lax.dot_general(q, kn_ref[...], _NT, preferred_element_type=f32) - cum_new
    row = lax.broadcasted_iota(jnp.int32, (L, L), 0)
    col = lax.broadcasted_iota(jnp.int32, (L, L), 1)
    s_n = jnp.where(col <= row, s_n, -jnp.inf)
    m = jnp.maximum(jnp.max(s_c, axis=1, keepdims=True), jnp.max(s_n, axis=1, keepdims=True))
    p_c = jnp.exp2(s_c - m)
    p_n = jnp.exp2(s_n - m)
    l = jnp.sum(p_c, axis=1, keepdims=True) + jnp.sum(p_n, axis=1, keepdims=True)
    acc = (jnp.dot(p_c.astype(bf16), vc, preferred_element_type=f32)
           + jnp.dot(p_n.astype(bf16), vn_ref[...], preferred_element_type=f32))
    o_ref[...] = (acc / l).astype(o_ref.dtype)


def _fox_decode(q, k, v, gates_t, bias, cache_k, cache_v, cum_cache, *, n_streams):
    t = q.shape[1]
    L = t // n_streams
    p = cache_k.shape[1] // F_HEADS
    gates4 = gates_t.reshape(N_GATE_ROWS, n_streams, 1, L)
    head_rows = pl.BlockSpec((None, L, F_HEAD_DIM), lambda b, h: (h, b, 0))
    return pl.pallas_call(
        _fox_decode_kernel,
        grid=(n_streams, F_HEADS),
        in_specs=[pl.BlockSpec(memory_space=pltpu.SMEM),
                  head_rows, head_rows, head_rows,
                  pl.BlockSpec((None, p * F_HEADS, F_HEAD_DIM), lambda b, h: (b, 0, 0)),
                  pl.BlockSpec((None, p * F_HEADS, F_HEAD_DIM), lambda b, h: (b, 0, 0)),
                  pl.BlockSpec((None, 1, p), lambda b, h: (b * F_HEADS + h, 0, 0)),
                  pl.BlockSpec((None, None, 1, L), lambda b, h: (2 * M_HEADS + h, b, 0, 0))],
        out_specs=[pl.BlockSpec((L, F_HEAD_DIM), lambda b, h: (b, h)),
                   pl.BlockSpec((None, None, 1, L), lambda b, h: (b, h, 0, 0))],
        out_shape=[jax.ShapeDtypeStruct((t, F_WIDTH), bf16),
                   jax.ShapeDtypeStruct((n_streams, F_HEADS, 1, L), f32)],
        compiler_params=_params("parallel", "parallel"),
        name="fox_decode",
    )(bias, q, k, v, cache_k, cache_v, cum_cache, gates4)


def _merge_kernel(x_ref, ha_ref, hb_ref, ga_ref, gb_ref, wa_ref, wb_ref, wo_ref, g_ref, x1_ref, hn_ref):
    pa = jnp.dot(ha_ref[...], wa_ref[...], preferred_element_type=f32)
    pb = jnp.dot(hb_ref[...], wb_ref[...], preferred_element_type=f32)
    merged = (jax.nn.sigmoid(ga_ref[...].astype(f32)) * pa + jax.nn.sigmoid(gb_ref[...].astype(f32)) * pb)
    x1 = x_ref[...] + jnp.dot(merged.astype(bf16), wo_ref[...], preferred_element_type=f32)
    x1_ref[...] = x1
    hn = x1 * lax.rsqrt(jnp.mean(x1 * x1, axis=-1, keepdims=True) + EPS) * g_ref[...]
    hn_ref[...] = hn.astype(hn_ref.dtype)


def _merge(x, h_a, h_b, z, w_a, w_b, w_o, g, *, gate_col0):
    t, d = x.shape
    tm = _tile(t, 256)
    ga_blk = gate_col0 // d
    row = lambda i: (i, 0)
    fixed = lambda i: (0, 0)
    resident = functools.partial(pl.BlockSpec, index_map=fixed, pipeline_mode=pl.Buffered(1))
    return pl.pallas_call(
        _merge_kernel,
        grid=(t // tm,),
        in_specs=[pl.BlockSpec((tm, d), row),
                  pl.BlockSpec((tm, M_WIDTH), row),
                  pl.BlockSpec((tm, F_WIDTH), row),
                  pl.BlockSpec((tm, d), lambda i: (i, ga_blk)),
                  pl.BlockSpec((tm, d), lambda i: (i, ga_blk + 1)),
                  resident(w_a.shape), resident(w_b.shape), resident(w_o.shape),
                  pl.BlockSpec((1, d), fixed)],
        out_specs=[pl.BlockSpec((tm, d), row), pl.BlockSpec((tm, d), row)],
        out_shape=[jax.ShapeDtypeStruct((t, d), f32), jax.ShapeDtypeStruct((t, d), bf16)],
        compiler_params=_params("parallel"),
        name="merge",
    )(x, h_a, h_b, z, z, w_a, w_b, w_o, g)


def _ffn_kernel(hn_ref, x1_ref, wu_ref, wd_ref, g_ref, y_ref):
    f = pl.program_id(1)

    @pl.when(f == 0)
    def _():
        y_ref[...] = x1_ref[...]

    u = jnp.maximum(jnp.dot(hn_ref[...], wu_ref[...], preferred_element_type=f32), 0.0)
    y_ref[...] += jnp.dot((u * u).astype(bf16), wd_ref[...], preferred_element_type=f32)

    @pl.when(f == pl.num_programs(1) - 1)
    def _():
        x2 = y_ref[...]
        y_ref[...] = x2 * lax.rsqrt(jnp.mean(x2 * x2, axis=-1, keepdims=True) + EPS) * g_ref[...]


def _ffn(hn, x1, w_up, w_down, g):
    t, d = x1.shape
    dff = w_up.shape[1]
    tm = _tile(t, 512)
    tf = _tile(dff, 1024)
    return pl.pallas_call(
        _ffn_kernel,
        grid=(t // tm, dff // tf),
        in_specs=[pl.BlockSpec((tm, d), lambda i, f: (i, 0)),
                  pl.BlockSpec((tm, d), lambda i, f: (i, 0)),
                  pl.BlockSpec((d, tf), lambda i, f: (0, f)),
                  pl.BlockSpec((tf, d), lambda i, f: (f, 0)),
                  pl.BlockSpec((1, d), lambda i, f: (0, 0))],
        out_specs=pl.BlockSpec((tm, d), lambda i, f: (i, 0)),
        out_shape=jax.ShapeDtypeStruct((t, d), f32),
        compiler_params=_params("parallel", "arbitrary"),
        name="ffn",
    )(hn, x1, w_up, w_down, g)


def _layer(x3, w, mstate, fox_past):
    nstr, frames, d = x3.shape
    t = nstr * frames
    x = x3.reshape(t, d)
    xn, gates_t = _norm_gates(x, w["norm_mix"], w["w_in_t"], *w["gate_rows"])
    w_proj_t = w["w_proj_t"]
    n_rest = w_proj_t.shape[0] - 3 * F_WIDTH
    k16, k32 = _matmul_heads_t(xn, w_proj_t, 0, "proj_k", state_rows=True)
    v16, v32 = _matmul_heads_t(xn, w_proj_t, F_WIDTH, "proj_v", state_rows=True)
    z = _matmul_t(xn, w_proj_t, 2 * F_WIDTH, n_rest, "proj_rest")
    (q16,) = _matmul_heads_t(xn, w_proj_t, 2 * F_WIDTH + n_rest, "proj_q", state_rows=False)

    c0, n0, m0 = mstate
    h_a, c_new, n_slab, m_slab = _mlstm(z, gates_t, w["mlstm_bias"], w["norm_mlstm_h"], c0, n0, m0,
                                        n_streams=nstr)
    n_new = n_slab[..., 0]
    m_new = m_slab[:, :, 0]

    if fox_past is None:
        logf_t, cum_t = _scan_rows(gates_t, w["b_fox_f_col"], row_block=1, rows=F_HEADS, n_streams=nstr,
                                   apply_log_sigmoid=True)
        h_b = _fox_prompt(q16, k16, v16, cum_t, n_streams=nstr)
        logf = logf_t.T.reshape(nstr, frames, F_HEADS)
    else:
        ck, cv, clf = fox_past
        p = ck.shape[1]
        clf_t = jnp.transpose(clf, (0, 2, 1)).reshape(nstr * F_HEADS, p)
        _, cum_c = _scan_rows(clf_t, jnp.zeros((nstr * F_HEADS, 1), f32), row_block=0, rows=nstr * F_HEADS,
                              n_streams=1, apply_log_sigmoid=False)
        h_b, logf4 = _fox_decode(q16, k16, v16, gates_t, w["b_fox_f"], ck.reshape(nstr, p * F_HEADS, F_HEAD_DIM),
                                 cv.reshape(nstr, p * F_HEADS, F_HEAD_DIM), cum_c.reshape(nstr * F_HEADS, 1, p),
                                 n_streams=nstr)
        logf = jnp.transpose(logf4[:, :, 0, :], (0, 2, 1))

    x1, hn = _merge(x, h_a, h_b, z, w["w_branch_a"], w["w_branch_b"], w["w_out"], w["norm_ffn"],
                    gate_col0=w["gate_col0"])
    y = _ffn(hn, x1, w["w_up"], w["w_down"], w["norm_final"])

    k_rows = k32.reshape(nstr, frames, F_HEADS, F_HEAD_DIM)
    v_rows = v32.reshape(nstr, frames, F_HEADS, F_HEAD_DIM)
    return y.reshape(nstr, frames, d), (k_rows, v_rows, logf, c_new, n_new, m_new)


def _regroup_kernel(table_ref, w_ref, o_ref):
    sid = table_ref[pl.program_id(0), 1]
    scale = jnp.where(sid == 1, M_DQK ** -0.5, jnp.where(sid == 2, F_HEAD_DIM ** -0.5 * LOG2E, 1.0))
    o_ref[...] = (w_ref[...] * scale).astype(bf16)


def _regroup_w_in_t(w_t, offs):
    d = w_t.shape[1]
    mq, mk, mv, mo, mi, mf, fq, fk, fv, ff, ga, gb = range(12)
    order = ((fk, 0), (fv, 0), (mq, 0), (mk, 1), (mv, 0), (mo, 0), (ga, 0), (gb, 0), (fq, 2))
    tr = _tile(d, F_WIDTH)
    table = []
    for seg, scale_id in order:
        start, stop = offs[seg], offs[seg + 1]
        assert start % 8 == 0 and (stop - start) % tr == 0
        table += [(r // 8, scale_id) for r in range(start, stop, tr)]
    table = jnp.asarray(table, jnp.int32)
    return pl.pallas_call(
        _regroup_kernel,
        grid_spec=pltpu.PrefetchScalarGridSpec(
            num_scalar_prefetch=1,
            grid=(table.shape[0],),
            in_specs=[pl.BlockSpec((pl.Element(tr), pl.Element(d)),
                                   lambda t, tbl: (pl.multiple_of(tbl[t, 0] * 8, 8), 0))],
            out_specs=pl.BlockSpec((tr, d), lambda t, tbl: (t, 0))),
        out_shape=jax.ShapeDtypeStruct((table.shape[0] * tr, d), bf16),
        compiler_params=_params("parallel"),
        name="regroup_w_in",
    )(table, w_t)


def _prepare_weights(norm_mix, w_in, b_mlstm_i, b_mlstm_f, b_fox_f, norm_mlstm_h, w_branch_a, w_branch_b,
                     w_out, norm_ffn, w_up, w_down, norm_final):
    d = w_in.shape[0]
    sizes = (M_HEADS * M_DQK, M_HEADS * M_DQK, M_WIDTH, M_WIDTH, M_HEADS, M_HEADS,
             F_WIDTH, F_WIDTH, F_WIDTH, F_HEADS, d, d)
    offs = [0]
    for s in sizes:
        offs.append(offs[-1] + s)
    w_t = w_in.T
    return {
        "norm_mix": norm_mix.reshape(1, d),
        "w_in_t": w_t,
        "gate_rows": (offs[4], offs[9]),
        "w_proj_t": _regroup_w_in_t(w_t, offs),
        "gate_col0": 4 * M_WIDTH,
        "mlstm_bias": jnp.concatenate([b_mlstm_i, b_mlstm_f]).astype(f32).reshape(2 * M_HEADS, 1),
        "b_fox_f": b_fox_f.astype(f32),
        "b_fox_f_col": b_fox_f.astype(f32).reshape(F_HEADS, 1),
        "norm_mlstm_h": norm_mlstm_h.reshape(1, M_WIDTH),
        "w_branch_a": w_branch_a.astype(bf16),
        "w_branch_b": w_branch_b.astype(bf16),
        "w_out": w_out.astype(bf16),
        "norm_ffn": norm_ffn.reshape(1, d),
        "w_up": w_up.astype(bf16),
        "w_down": w_down.astype(bf16),
        "norm_final": norm_final.reshape(1, d),
    }


def kernel(x_prompt, x_sample, cache_fox_k, cache_fox_v, cache_fox_logf, state_mlstm_c, state_mlstm_n, state_mlstm_m, norm_mix, w_in, b_mlstm_i, b_mlstm_f, b_fox_f, norm_mlstm_h, w_branch_a, w_branch_b, w_out, norm_ffn, w_up, w_down, norm_final):
    depth = w_in.shape[0]
    assert depth == 1, "the final norm is fused into the layer's FFN kernel"
    w = _prepare_weights(norm_mix[0], w_in[0], b_mlstm_i[0], b_mlstm_f[0], b_fox_f[0], norm_mlstm_h[0],
                         w_branch_a[0], w_branch_b[0], w_out[0], norm_ffn[0], w_up[0], w_down[0], norm_final)
    bp = x_prompt.shape[0]
    fresh = (jnp.zeros((bp, M_HEADS, M_DQK, M_DV), f32), jnp.zeros((bp, M_HEADS, M_DQK), f32),
             jnp.zeros((bp, M_HEADS), f32))
    y_p, st_p = _layer(x_prompt, w, fresh, None)
    y_s, st_s = _layer(x_sample, w, (state_mlstm_c[0], state_mlstm_n[0], state_mlstm_m[0]),
                       (cache_fox_k[0], cache_fox_v[0], cache_fox_logf[0]))
    return (y_p, y_s) + tuple(a[None] for a in st_p) + tuple(a[None] for a in st_s)
```

```python
import functools

import jax
import jax.numpy as jnp
from jax import lax
from jax.experimental import pallas as pl
from jax.experimental.pallas import tpu as pltpu

M_HEADS = 4
M_DQK = 256
M_DV = 256
M_WIDTH = M_HEADS * M_DV
F_HEADS = 8
F_HEAD_DIM = 128
F_WIDTH = F_HEADS * F_HEAD_DIM
EPS = 1e-6
N_GATE_ROWS = 2 * M_HEADS + F_HEADS
LANES = 128
assert F_HEAD_DIM == LANES
MLSTM_BLOCK = 256
ATTN_BLOCK = 512
ATTN_TRIP = 6
DECODE_HEADS = 2
LOG2E = 1.4426950408889634
VMEM_LIMIT_BYTES = 56 * 1024 * 1024

f32 = jnp.float32
bf16 = jnp.bfloat16

_NT = (((1,), (1,)), ((), ()))
_TN = (((0,), (0,)), ((), ()))


def _tile(n, pref):
    t = min(n, pref)
    while n % t:
        t //= 2
    return t


def _params(*sem):
    return pltpu.CompilerParams(dimension_semantics=sem, vmem_limit_bytes=VMEM_LIMIT_BYTES)


def _log_sigmoid(z):
    return jnp.minimum(z, 0.0) - jnp.log1p(jnp.exp(-jnp.abs(z)))


def _cumsum_lanes(x):
    rows, n = x.shape
    pad = -rows % 16
    if rows == 1:
        x16 = jnp.broadcast_to(x, (16, n))
    elif pad == 0:
        x16 = x
    else:
        x16 = jnp.concatenate([x, jnp.zeros((pad, n), f32)], axis=0)
    r = lax.broadcasted_iota(jnp.int32, (n, n), 0)
    c = lax.broadcasted_iota(jnp.int32, (n, n), 1)
    u = jnp.where(r <= c, 1.0, 0.0).astype(bf16)
    hi = x16.astype(bf16)
    rem = x16 - hi.astype(f32)
    mid = rem.astype(bf16)
    lo = (rem - mid.astype(f32)).astype(bf16)
    out = (jnp.dot(hi, u, preferred_element_type=f32) + jnp.dot(mid, u, preferred_element_type=f32)
           + jnp.dot(lo, u, preferred_element_type=f32))
    return out[:rows]


def _norm_kernel(x_ref, g_ref, wm_ref, wf_ref, xn_ref, gt_ref):
    x = x_ref[...]
    y = x * lax.rsqrt(jnp.mean(x * x, axis=-1, keepdims=True) + EPS) * g_ref[...]
    xn = y.astype(bf16)
    xn_ref[...] = xn
    wg = jnp.concatenate([wm_ref[...], wf_ref[...]], axis=0).astype(bf16)
    gt_ref[...] = lax.dot_general(wg, xn, _NT, preferred_element_type=f32)


def _norm_gates(x, g, w_t, mlstm_gate_row, fox_gate_row):
    t, d = x.shape
    tm = _tile(t, 512)
    assert 2 * M_HEADS == F_HEADS and mlstm_gate_row % F_HEADS == 0 and fox_gate_row % F_HEADS == 0
    return pl.pallas_call(
        _norm_kernel,
        grid=(t // tm,),
        in_specs=[pl.BlockSpec((tm, d), lambda i: (i, 0)),
                  pl.BlockSpec((1, d), lambda i: (0, 0)),
                  pl.BlockSpec((F_HEADS, d), lambda i: (mlstm_gate_row // F_HEADS, 0)),
                  pl.BlockSpec((F_HEADS, d), lambda i: (fox_gate_row // F_HEADS, 0))],
        out_specs=[pl.BlockSpec((tm, d), lambda i: (i, 0)),
                   pl.BlockSpec((N_GATE_ROWS, tm), lambda i: (0, i))],
        out_shape=[jax.ShapeDtypeStruct((t, d), bf16), jax.ShapeDtypeStruct((N_GATE_ROWS, t), f32)],
        compiler_params=_params("parallel"),
        name="norm_gates",
    )(x, g, w_t, w_t)


def _mm_kernel(a_ref, w_ref, o_ref):
    o_ref[...] = lax.dot_general(a_ref[...], w_ref[...], _NT, preferred_element_type=f32).astype(o_ref.dtype)


def _matmul_t(a, w_t, row0, n, name):
    t, k = a.shape
    tm = _tile(t, 2048)
    tn = _tile(n, 1024)
    assert row0 % tn == 0
    return pl.pallas_call(
        _mm_kernel,
        grid=(t // tm, n // tn),
        in_specs=[pl.BlockSpec((tm, k), lambda i, j: (i, 0)),
                  pl.BlockSpec((tn, k), lambda i, j: (row0 // tn + j, 0))],
        out_specs=pl.BlockSpec((tm, tn), lambda i, j: (i, j)),
        out_shape=jax.ShapeDtypeStruct((t, n), bf16),
        compiler_params=_params("parallel", "arbitrary"),
        name=name,
    )(a, w_t)


def _mm_heads_kernel(a_ref, w_ref, o16_ref, *rows_ref):
    r = lax.dot_general(a_ref[...], w_ref[...], _NT, preferred_element_type=f32)
    tm = a_ref.shape[0]
    for h in range(F_HEADS):
        head = r[:, h * F_HEAD_DIM:(h + 1) * F_HEAD_DIM]
        o16_ref[h] = head.astype(bf16)
        for ref in rows_ref:
            ref[pl.ds(h, tm, stride=F_HEADS), :] = head


def _matmul_heads_t(a, w_t, row0, name, *, state_rows):
    t, k = a.shape
    assert row0 % F_WIDTH == 0
    tm = _tile(t, 1024)
    out_specs = [pl.BlockSpec((F_HEADS, tm, F_HEAD_DIM), lambda i: (0, i, 0))]
    out_shape = [jax.ShapeDtypeStruct((F_HEADS, t, F_HEAD_DIM), bf16)]
    if state_rows:
        out_specs.append(pl.BlockSpec((tm * F_HEADS, F_HEAD_DIM), lambda i: (i, 0)))
        out_shape.append(jax.ShapeDtypeStruct((t * F_HEADS, F_HEAD_DIM), f32))
    return pl.pallas_call(
        _mm_heads_kernel,
        grid=(t // tm,),
        in_specs=[pl.BlockSpec((tm, k), lambda i: (i, 0)),
                  pl.BlockSpec((F_WIDTH, k), lambda i: (row0 // F_WIDTH, 0))],
        out_specs=out_specs,
        out_shape=out_shape,
        compiler_params=_params("parallel"),
        name=name,
    )(a, w_t)


def _scan_kernel(x_ref, bias_ref, logf_ref, cum_ref, carry_ref, *, apply_log_sigmoid):
    @pl.when(pl.program_id(1) == 0)
    def _():
        carry_ref[...] = jnp.zeros_like(carry_ref)

    x = x_ref[...]
    if apply_log_sigmoid:
        x = _log_sigmoid(x + bias_ref[...])
    logf_ref[...] = x
    cum = _cumsum_lanes(x) + carry_ref[:, :1]
    cum_ref[...] = cum * LOG2E
    carry_ref[...] = jnp.broadcast_to(cum[:, -1:], carry_ref.shape)


def _scan_rows(x, bias, *, row_block, rows, n_streams, apply_log_sigmoid):
    total = x.shape[1]
    s = total // n_streams
    tb = _tile(s, 512)
    nb = s // tb
    spec = pl.BlockSpec((rows, tb), lambda b, j: (0, b * nb + j))
    return pl.pallas_call(
        functools.partial(_scan_kernel, apply_log_sigmoid=apply_log_sigmoid),
        grid=(n_streams, nb),
        in_specs=[pl.BlockSpec((rows, tb), lambda b, j: (row_block, b * nb + j)),
                  pl.BlockSpec((rows, 1), lambda b, j: (0, 0))],
        out_specs=[spec, spec],
        out_shape=[jax.ShapeDtypeStruct((rows, total), f32)] * 2,
        scratch_shapes=[pltpu.VMEM((rows, LANES), f32)],
        compiler_params=_params("arbitrary", "arbitrary"),
        name="logf_scan",
    )(x, bias)


def _mlstm_kernel(bias_ref, q_ref, k_ref, v_ref, o_ref, gates_ref, nh_ref, c0_ref, n0_ref, m0_ref,
                  h_ref, c_out_ref, n_out_ref, m_out_ref, caug_ref, m_ref):
    blk = pl.program_id(1)
    n_blk = pl.num_programs(1)
    L = q_ref.shape[0]

    @pl.when(blk == 0)
    def _():
        lane = lax.broadcasted_iota(jnp.int32, (M_HEADS, M_DQK, LANES), 2)
        caug_ref[:, :, :M_DV] = c0_ref[...]
        caug_ref[:, :, M_DV:] = jnp.where(lane == 0, n0_ref[...], 0.0)
        m_ref[...] = m0_ref[...]

    gates = gates_ref[...]
    ig_all = gates[:M_HEADS] + bias_ref[:M_HEADS]
    lf_all = _log_sigmoid(gates[M_HEADS:2 * M_HEADS] + bias_ref[M_HEADS:])
    a_all = ig_all - _cumsum_lanes(lf_all)

    t_idx = lax.broadcasted_iota(jnp.int32, (L, L), 0)
    s_idx = lax.broadcasted_iota(jnp.int32, (L, L), 1)
    causal = s_idx <= t_idx
    diag = s_idx == t_idx
    ones_col = jnp.where(lax.broadcasted_iota(jnp.int32, (L, LANES), 1) == 0, 1.0, 0.0).astype(bf16)

    for h in range(M_HEADS):
        cols = slice(h * M_DV, (h + 1) * M_DV)
        m0 = m_ref[h:h + 1, :1]
        ig = ig_all[h:h + 1]
        lf = lf_all[h:h + 1]
        a_row = a_all[h:h + 1]
        a_mat = jnp.where(causal, a_row, -jnp.inf)
        g_col = jnp.maximum(m0, jnp.max(a_mat, axis=1, keepdims=True))
        b_col = jnp.sum(jnp.where(causal, lf, 0.0), axis=1, keepdims=True)
        w_intra = jnp.exp(a_mat - g_col)
        w_inter = jnp.exp(m0 - g_col)

        q = q_ref[:, cols]
        k = k_ref[:, cols]
        v_aug = jnp.concatenate([v_ref[:, cols], ones_col], axis=1)
        c_aug = caug_ref[h]

        s = lax.dot_general(q, k, _NT, preferred_element_type=f32)
        sw = (s * w_intra).astype(bf16)
        num_aug = (jnp.dot(sw, v_aug, preferred_element_type=f32)
                   + w_inter * jnp.dot(q, c_aug.astype(bf16), preferred_element_type=f32))
        num = num_aug[:, :M_DV]
        den = num_aug[:, M_DV:M_DV + 1]
        den = jnp.maximum(jnp.abs(den), jnp.exp(-(b_col + g_col)))
        hh = num / den
        hh = hh * lax.rsqrt(jnp.mean(hh * hh, axis=-1, keepdims=True) + EPS)
        hh = hh * nh_ref[:, cols] * jax.nn.sigmoid(o_ref[:, cols].astype(f32))
        h_ref[:, cols] = hh.astype(h_ref.dtype)

        g_end = jnp.maximum(m0, jnp.max(a_row, axis=1, keepdims=True))
        b_end = jnp.sum(lf, axis=1, keepdims=True)
        ig_col = jnp.sum(jnp.where(diag, ig, 0.0), axis=1, keepdims=True)
        w_tok = jnp.exp(ig_col - b_col - g_end)
        w_state = jnp.exp(m0 - g_end)
        kw = (k.astype(f32) * w_tok).astype(bf16)
        caug_ref[h] = w_state * c_aug + lax.dot_general(kw, v_aug, _TN, preferred_element_type=f32)
        m_ref[h:h + 1, :] = jnp.broadcast_to(b_end + g_end, (1, LANES))

    @pl.when(blk == n_blk - 1)
    def _():
        c_out_ref[...] = caug_ref[:, :, :M_DV]
        n_out_ref[...] = caug_ref[:, :, M_DV:]
        m_out_ref[...] = m_ref[...]


def _mlstm(z, gates_t, gate_bias, norm_h, c0, n0, m0, *, n_streams):
    t = z.shape[0]
    s = t // n_streams
    L = _tile(s, MLSTM_BLOCK)
    nb = s // L
    gates3 = gates_t.reshape(N_GATE_ROWS, n_streams * nb, L).transpose(1, 0, 2)
    n0c = n0.reshape(n_streams, M_HEADS, M_DQK, 1)
    m0b = jnp.broadcast_to(m0.reshape(n_streams, M_HEADS, 1), (n_streams, M_HEADS, LANES))

    def zcols(group):
        return pl.BlockSpec((L, M_WIDTH), lambda b, c: (b * nb + c, group))

    def state(*tail):
        return pl.BlockSpec((None, M_HEADS) + tail, lambda b, c: (b, 0) + (0,) * len(tail))

    return pl.pallas_call(
        _mlstm_kernel,
        grid=(n_streams, nb),
        in_specs=[pl.BlockSpec((2 * M_HEADS, 1), lambda b, c: (0, 0)),
                  zcols(0), zcols(1), zcols(2), zcols(3),
                  pl.BlockSpec((None, N_GATE_ROWS, L), lambda b, c: (b * nb + c, 0, 0)),
                  pl.BlockSpec((1, M_WIDTH), lambda b, c: (0, 0)),
                  state(M_DQK, M_DV), state(M_DQK, 1), state(LANES)],
        out_specs=[pl.BlockSpec((L, M_WIDTH), lambda b, c: (b * nb + c, 0)),
                   state(M_DQK, M_DV), state(M_DQK, LANES), state(LANES)],
        out_shape=[jax.ShapeDtypeStruct((t, M_WIDTH), bf16),
                   jax.ShapeDtypeStruct((n_streams, M_HEADS, M_DQK, M_DV), f32),
                   jax.ShapeDtypeStruct((n_streams, M_HEADS, M_DQK, LANES), f32),
                   jax.ShapeDtypeStruct((n_streams, M_HEADS, LANES), f32)],
        scratch_shapes=[pltpu.VMEM((M_HEADS, M_DQK, M_DV + LANES), f32), pltpu.VMEM((M_HEADS, LANES), f32)],
        compiler_params=_params("parallel", "arbitrary"),
        name="mlstm",
    )(gate_bias, z, z, z, z, gates3, norm_h, c0, n0c, m0b)


def _fox_prompt_kernel(q_ref, k_ref, v_ref, cum_ref, o_ref,
                       sa_ref, sb_ref, pa_ref, pb_ref, xa_ref, xb_ref, m_ref, l_ref, acc_ref):
    i = pl.program_id(2)
    tq = o_ref.shape[0]
    tk = sa_ref.shape[1]
    assert tq == tk

    def q_rows(r):
        return q_ref[pl.ds(pl.multiple_of(r * tq, tq), tq), :]

    def k_rows(ref, j):
        return ref[pl.ds(pl.multiple_of(j * tk, tk), tk), :]

    def causal(s, j):
        ahead = (lax.broadcasted_iota(jnp.int32, (tq, tk), 1) - lax.broadcasted_iota(jnp.int32, (tq, tk), 0))
        return jnp.where(ahead <= i * tq - j * tk, s, -jnp.inf)

    def scores(j, s_ref, x_ref, masked, q_row=i):
        s = lax.dot_general(q_rows(q_row), k_rows(k_ref, j), _NT, preferred_element_type=f32) - cum_ref[j]
        if masked:
            s = causal(s, j)
        s_ref[...] = s
        x_ref[...] = jnp.broadcast_to(jnp.max(s, axis=1, keepdims=True), x_ref.shape)

    def values(j, p_ref):
        return jnp.dot(p_ref[...], k_rows(v_ref, j), preferred_element_type=f32)

    def softmax(s_ref, x_ref, p_ref, pv_prev, mask_block=None):
        s = s_ref[...]
        if mask_block is None:
            x = x_ref[...]
        else:
            s = causal(s, mask_block)
            x = jnp.max(s, axis=1, keepdims=True)
        m_old = m_ref[...]
        m_new = jnp.maximum(m_old, x)
        alpha = jnp.exp2(m_old - m_new)
        p = jnp.exp2((s - jnp.concatenate([m_new] * (tk // LANES), axis=1)).astype(bf16))
        p_ref[...] = p
        l_ref[...] = alpha * l_ref[...] + sum(p[:, c:c + LANES] for c in range(0, tk, LANES)).astype(f32)
        acc_ref[...] = alpha * (acc_ref[...] + pv_prev)
        m_ref[...] = m_new

    m_ref[...] = jnp.full(m_ref.shape, -jnp.inf, f32)
    l_ref[...] = jnp.zeros(l_ref.shape, f32)
    acc_ref[...] = jnp.zeros(acc_ref.shape, f32)
    pb_ref[...] = jnp.zeros(pb_ref.shape, bf16)

    @pl.when(i == 0)
    def _():
        scores(0, sa_ref, xa_ref, False)

    def pair(j0):
        scores(j0 + 1, sb_ref, xb_ref, False)
        softmax(sa_ref, xa_ref, pa_ref, values(jnp.maximum(j0 - 1, 0), pb_ref))
        scores(j0 + 2, sa_ref, xa_ref, False)
        softmax(sb_ref, xb_ref, pb_ref, values(j0, pa_ref))

    def long_trip(jj, carry):
        for u in range(0, ATTN_TRIP, 2):
            pair(ATTN_TRIP * jj + u)
        return carry

    def short_trip(jj, carry):
        pair(ATTN_TRIP * (i // ATTN_TRIP) + 2 * jj)
        return carry

    lax.fori_loop(0, i // ATTN_TRIP, long_trip, 0)
    lax.fori_loop(0, (i % ATTN_TRIP) // 2, short_trip, 0)

    def scores_of_next_row():
        scores(0, sa_ref, xa_ref, False, q_row=jnp.minimum(i + 1, pl.num_programs(2) - 1))

    @pl.when(i % 2 == 0)
    def _():
        softmax(sa_ref, xa_ref, pa_ref, values(jnp.maximum(i - 1, 0), pb_ref), mask_block=i)
        scores_of_next_row()
        acc_ref[...] += values(i, pa_ref)

    @pl.when(i % 2 == 1)
    def _():
        scores(i, sb_ref, xb_ref, True)
        softmax(sa_ref, xa_ref, pa_ref, values(jnp.maximum(i - 2, 0), pb_ref))
        scores_of_next_row()
        softmax(sb_ref, xb_ref, pb_ref, values(i - 1, pa_ref))
        acc_ref[...] += values(i, pb_ref)

    o_ref[...] = (acc_ref[...] / jnp.sum(l_ref[...], axis=1, keepdims=True)).astype(o_ref.dtype)


def _fox_prompt(q, k, v, cum_t, *, n_streams):
    t = q.shape[1]
    s = t // n_streams
    tq = tk = _tile(s, ATTN_BLOCK)
    nq = nk = s // tk
    cum4 = cum_t.reshape(F_HEADS, n_streams * nk, 1, tk)
    head_rows = pl.BlockSpec((None, s, F_HEAD_DIM), lambda b, h, i: (h, b, 0))
    return pl.pallas_call(
        _fox_prompt_kernel,
        grid=(n_streams, F_HEADS, nq),
        in_specs=[head_rows, head_rows, head_rows,
                  pl.BlockSpec((None, nk, 1, tk), lambda b, h, i: (h, b, 0, 0))],
        out_specs=pl.BlockSpec((tq, F_HEAD_DIM), lambda b, h, i: (b * nq + i, h)),
        out_shape=jax.ShapeDtypeStruct((t, F_WIDTH), bf16),
        scratch_shapes=[pltpu.VMEM((tq, tk), f32)] * 2 + [pltpu.VMEM((tq, tk), bf16)] * 2
        + [pltpu.VMEM((tq, LANES), f32)] * 5,
        compiler_params=_params("parallel", "parallel", "arbitrary"),
        name="fox_prompt",
    )(q, k, v, cum4)


def _fox_decode_kernel(bias_ref, q_ref, kn_ref, vn_ref, kc_ref, vc_ref, cumc_ref, fg_ref, o_ref, logf_ref):
    n_heads, L = q_ref.shape[:2]
    p_len = kc_ref.shape[0] // F_HEADS
    row = lax.broadcasted_iota(jnp.int32, (L, L), 0)
    col = lax.broadcasted_iota(jnp.int32, (L, L), 1)
    for g in range(n_heads):
        head = pl.program_id(1) * n_heads + g
        q = q_ref[g]
        logf = _log_sigmoid(fg_ref[g] + bias_ref[head])
        logf_ref[g] = logf
        cum_new = _cumsum_lanes(logf) * LOG2E
        cum_c = cumc_ref[g]
        cum_c = cum_c - cum_c[:, -1:]

        kc = kc_ref[pl.ds(head, p_len, stride=F_HEADS), :].astype(bf16)
        vc = vc_ref[pl.ds(head, p_len, stride=F_HEADS), :].astype(bf16)
        s_c = lax.dot_general(q, kc, _NT, preferred_element_type=f32) - cum_c
        s_n = lax.dot_general(q, kn_ref[g], _NT, preferred_element_type=f32) - cum_new
        s_n = jnp.where(col <= row, s_n, -jnp.inf)
        m = jnp.maximum(jnp.max(s_c, axis=1, keepdims=True), jnp.max(s_n, axis=1, keepdims=True))
        p_c = jnp.exp2(s_c - m)
        p_n = jnp.exp2(s_n - m)
        l = jnp.sum(p_c, axis=1, keepdims=True) + jnp.sum(p_n, axis=1, keepdims=True)
        acc = (jnp.dot(p_c.astype(bf16), vc, preferred_element_type=f32)
               + jnp.dot(p_n.astype(bf16), vn_ref[g], preferred_element_type=f32))
        o_ref[:, g * F_HEAD_DIM:(g + 1) * F_HEAD_DIM] = (acc / l).astype(o_ref.dtype)


def _fox_decode(q, k, v, gates_t, bias, cache_k, cache_v, cum_cache, *, n_streams):
    t = q.shape[1]
    L = t // n_streams
    p = cache_k.shape[1] // F_HEADS
    g = DECODE_HEADS
    steps = F_HEADS // g
    gates4 = gates_t.reshape(N_GATE_ROWS, n_streams, 1, L)
    head_rows = pl.BlockSpec((g, L, F_HEAD_DIM), lambda b, h: (h, b, 0))
    return pl.pallas_call(
        _fox_decode_kernel,
        grid=(n_streams, steps),
        in_specs=[pl.BlockSpec(memory_space=pltpu.SMEM),
                  head_rows, head_rows, head_rows,
                  pl.BlockSpec((None, p * F_HEADS, F_HEAD_DIM), lambda b, h: (b, 0, 0)),
                  pl.BlockSpec((None, p * F_HEADS, F_HEAD_DIM), lambda b, h: (b, 0, 0)),
                  pl.BlockSpec((g, 1, p), lambda b, h: (b * steps + h, 0, 0)),
                  pl.BlockSpec((g, None, 1, L), lambda b, h: (2 * M_HEADS // g + h, b, 0, 0))],
        out_specs=[pl.BlockSpec((L, g * F_HEAD_DIM), lambda b, h: (b, h)),
                   pl.BlockSpec((None, g, 1, L), lambda b, h: (b, h, 0, 0))],
        out_shape=[jax.ShapeDtypeStruct((t, F_WIDTH), bf16),
                   jax.ShapeDtypeStruct((n_streams, F_HEADS, 1, L), f32)],
        compiler_params=_params("parallel", "parallel"),
        name="fox_decode",
    )(bias, q, k, v, cache_k, cache_v, cum_cache, gates4)


def _merge_kernel(x_ref, ha_ref, hb_ref, ga_ref, gb_ref, wa_ref, wb_ref, wo_ref, g_ref, x1_ref, hn_ref):
    pa = jnp.dot(ha_ref[...], wa_ref[...], preferred_element_type=f32)
    pb = jnp.dot(hb_ref[...], wb_ref[...], preferred_element_type=f32)
    merged = (jax.nn.sigmoid(ga_ref[...].astype(f32)) * pa + jax.nn.sigmoid(gb_ref[...].astype(f32)) * pb)
    x1 = x_ref[...] + jnp.dot(merged.astype(bf16), wo_ref[...], preferred_element_type=f32)
    x1_ref[...] = x1
    hn = x1 * lax.rsqrt(jnp.mean(x1 * x1, axis=-1, keepdims=True) + EPS) * g_ref[...]
    hn_ref[...] = hn.astype(hn_ref.dtype)


def _merge(x, h_a, h_b, z, w_a, w_b, w_o, g, *, gate_col0):
    t, d = x.shape
    tm = _tile(t, 256)
    ga_blk = gate_col0 // d
    row = lambda i: (i, 0)
    fixed = lambda i: (0, 0)
    resident = functools.partial(pl.BlockSpec, index_map=fixed, pipeline_mode=pl.Buffered(1))
    return pl.pallas_call(
        _merge_kernel,
        grid=(t // tm,),
        in_specs=[pl.BlockSpec((tm, d), row),
                  pl.BlockSpec((tm, M_WIDTH), row),
                  pl.BlockSpec((tm, F_WIDTH), row),
                  pl.BlockSpec((tm, d), lambda i: (i, ga_blk)),
                  pl.BlockSpec((tm, d), lambda i: (i, ga_blk + 1)),
                  resident(w_a.shape), resident(w_b.shape), resident(w_o.shape),
                  pl.BlockSpec((1, d), fixed)],
        out_specs=[pl.BlockSpec((tm, d), row), pl.BlockSpec((tm, d), row)],
        out_shape=[jax.ShapeDtypeStruct((t, d), f32), jax.ShapeDtypeStruct((t, d), bf16)],
        compiler_params=_params("parallel"),
        name="merge",
    )(x, h_a, h_b, z, z, w_a, w_b, w_o, g)


def _ffn_kernel(hn_ref, x1_ref, wu_ref, wd_ref, g_ref, y_ref):
    f = pl.program_id(1)

    @pl.when(f == 0)
    def _():
        y_ref[...] = x1_ref[...]

    u = jnp.maximum(jnp.dot(hn_ref[...], wu_ref[...], preferred_element_type=f32), 0.0)
    y_ref[...] += jnp.dot((u * u).astype(bf16), wd_ref[...], preferred_element_type=f32)

    @pl.when(f == pl.num_programs(1) - 1)
    def _():
        x2 = y_ref[...]
        y_ref[...] = x2 * lax.rsqrt(jnp.mean(x2 * x2, axis=-1, keepdims=True) + EPS) * g_ref[...]


def _ffn(hn, x1, w_up, w_down, g):
    t, d = x1.shape
    dff = w_up.shape[1]
    tm = _tile(t, 512)
    tf = _tile(dff, 1024)
    return pl.pallas_call(
        _ffn_kernel,
        grid=(t // tm, dff // tf),
        in_specs=[pl.BlockSpec((tm, d), lambda i, f: (i, 0)),
                  pl.BlockSpec((tm, d), lambda i, f: (i, 0)),
                  pl.BlockSpec((d, tf), lambda i, f: (0, f)),
                  pl.BlockSpec((tf, d), lambda i, f: (f, 0)),
                  pl.BlockSpec((1, d), lambda i, f: (0, 0))],
        out_specs=pl.BlockSpec((tm, d), lambda i, f: (i, 0)),
        out_shape=jax.ShapeDtypeStruct((t, d), f32),
        compiler_params=_params("parallel", "arbitrary"),
        name="ffn",
    )(hn, x1, w_up, w_down, g)


def _layer(x3, w, mstate, fox_past):
    nstr, frames, d = x3.shape
    t = nstr * frames
    x = x3.reshape(t, d)
    xn, gates_t = _norm_gates(x, w["norm_mix"], w["w_in_t"], *w["gate_rows"])
    w_proj_t = w["w_proj_t"]
    n_rest = w_proj_t.shape[0] - 3 * F_WIDTH
    k16, k32 = _matmul_heads_t(xn, w_proj_t, 0, "proj_k", state_rows=True)
    v16, v32 = _matmul_heads_t(xn, w_proj_t, F_WIDTH, "proj_v", state_rows=True)
    z = _matmul_t(xn, w_proj_t, 2 * F_WIDTH, n_rest, "proj_rest")
    (q16,) = _matmul_heads_t(xn, w_proj_t, 2 * F_WIDTH + n_rest, "proj_q", state_rows=False)

    c0, n0, m0 = mstate
    h_a, c_new, n_slab, m_slab = _mlstm(z, gates_t, w["mlstm_bias"], w["norm_mlstm_h"], c0, n0, m0,
                                        n_streams=nstr)
    n_new = n_slab[..., 0]
    m_new = m_slab[:, :, 0]

    if fox_past is None:
        logf_t, cum_t = _scan_rows(gates_t, w["b_fox_f_col"], row_block=1, rows=F_HEADS, n_streams=nstr,
                                   apply_log_sigmoid=True)
        h_b = _fox_prompt(q16, k16, v16, cum_t, n_streams=nstr)
        logf = logf_t.T.reshape(nstr, frames, F_HEADS)
    else:
        ck, cv, clf = fox_past
        p = ck.shape[1]
        clf_t = jnp.transpose(clf, (0, 2, 1)).reshape(nstr * F_HEADS, p)
        _, cum_c = _scan_rows(clf_t, jnp.zeros((nstr * F_HEADS, 1), f32), row_block=0, rows=nstr * F_HEADS,
                              n_streams=1, apply_log_sigmoid=False)
        h_b, logf4 = _fox_decode(q16, k16, v16, gates_t, w["b_fox_f"], ck.reshape(nstr, p * F_HEADS, F_HEAD_DIM),
                                 cv.reshape(nstr, p * F_HEADS, F_HEAD_DIM), cum_c.reshape(nstr * F_HEADS, 1, p),
                                 n_streams=nstr)
        logf = jnp.transpose(logf4[:, :, 0, :], (0, 2, 1))

    x1, hn = _merge(x, h_a, h_b, z, w["w_branch_a"], w["w_branch_b"], w["w_out"], w["norm_ffn"],
                    gate_col0=w["gate_col0"])
    y = _ffn(hn, x1, w["w_up"], w["w_down"], w["norm_final"])

    k_rows = k32.reshape(nstr, frames, F_HEADS, F_HEAD_DIM)
    v_rows = v32.reshape(nstr, frames, F_HEADS, F_HEAD_DIM)
    return y.reshape(nstr, frames, d), (k_rows, v_rows, logf, c_new, n_new, m_new)


def _regroup_kernel(table_ref, w_ref, o_ref):
    sid = table_ref[pl.program_id(0), 1]
    scale = jnp.where(sid == 1, M_DQK ** -0.5, jnp.where(sid == 2, F_HEAD_DIM ** -0.5 * LOG2E, 1.0))
    o_ref[...] = (w_ref[...] * scale).astype(bf16)


def _regroup_w_in_t(w_t, offs):
    d = w_t.shape[1]
    mq, mk, mv, mo, mi, mf, fq, fk, fv, ff, ga, gb = range(12)
    order = ((fk, 0), (fv, 0), (mq, 0), (mk, 1), (mv, 0), (mo, 0), (ga, 0), (gb, 0), (fq, 2))
    tr = _tile(d, F_WIDTH)
    table = []
    for seg, scale_id in order:
        start, stop = offs[seg], offs[seg + 1]
        assert start % 8 == 0 and (stop - start) % tr == 0
        table += [(r // 8, scale_id) for r in range(start, stop, tr)]
    table = jnp.asarray(table, jnp.int32)
    return pl.pallas_call(
        _regroup_kernel,
        grid_spec=pltpu.PrefetchScalarGridSpec(
            num_scalar_prefetch=1,
            grid=(table.shape[0],),
            in_specs=[pl.BlockSpec((pl.Element(tr), pl.Element(d)),
                                   lambda t, tbl: (pl.multiple_of(tbl[t, 0] * 8, 8), 0))],
            out_specs=pl.BlockSpec((tr, d), lambda t, tbl: (t, 0))),
        out_shape=jax.ShapeDtypeStruct((table.shape[0] * tr, d), bf16),
        compiler_params=_params("parallel"),
        name="regroup_w_in",
    )(table, w_t)


def _prepare_weights(norm_mix, w_in, b_mlstm_i, b_mlstm_f, b_fox_f, norm_mlstm_h, w_branch_a, w_branch_b,
                     w_out, norm_ffn, w_up, w_down, norm_final):
    d = w_in.shape[0]
    sizes = (M_HEADS * M_DQK, M_HEADS * M_DQK, M_WIDTH, M_WIDTH, M_HEADS, M_HEADS,
             F_WIDTH, F_WIDTH, F_WIDTH, F_HEADS, d, d)
    offs = [0]
    for s in sizes:
        offs.append(offs[-1] + s)
    w_t = w_in.T
    return {
        "norm_mix": norm_mix.reshape(1, d),
        "w_in_t": w_t,
        "gate_rows": (offs[4], offs[9]),
        "w_proj_t": _regroup_w_in_t(w_t, offs),
        "gate_col0": 4 * M_WIDTH,
        "mlstm_bias": jnp.concatenate([b_mlstm_i, b_mlstm_f]).astype(f32).reshape(2 * M_HEADS, 1),
        "b_fox_f": b_fox_f.astype(f32),
        "b_fox_f_col": b_fox_f.astype(f32).reshape(F_HEADS, 1),
        "norm_mlstm_h": norm_mlstm_h.reshape(1, M_WIDTH),
        "w_branch_a": w_branch_a.astype(bf16),
        "w_branch_b": w_branch_b.astype(bf16),
        "w_out": w_out.astype(bf16),
        "norm_ffn": norm_ffn.reshape(1, d),
        "w_up": w_up.astype(bf16),
        "w_down": w_down.astype(bf16),
        "norm_final": norm_final.reshape(1, d),
    }


def kernel(x_prompt, x_sample, cache_fox_k, cache_fox_v, cache_fox_logf, state_mlstm_c, state_mlstm_n, state_mlstm_m, norm_mix, w_in, b_mlstm_i, b_mlstm_f, b_fox_f, norm_mlstm_h, w_branch_a, w_branch_b, w_out, norm_ffn, w_up, w_down, norm_final):
    depth = w_in.shape[0]
    assert depth == 1, "the final norm is fused into the layer's FFN kernel"
    w = _prepare_weights(norm_mix[0], w_in[0], b_mlstm_i[0], b_mlstm_f[0], b_fox_f[0], norm_mlstm_h[0],
                         w_branch_a[0], w_branch_b[0], w_out[0], norm_ffn[0], w_up[0], w_down[0], norm_final)
    bp = x_prompt.shape[0]
    fresh = (jnp.zeros((bp, M_HEADS, M_DQK, M_DV), f32), jnp.zeros((bp, M_HEADS, M_DQK), f32),
             jnp.zeros((bp, M_HEADS), f32))
    y_p, st_p = _layer(x_prompt, w, fresh, None)
    y_s, st_s = _layer(x_sample, w, (state_mlstm_c[0], state_mlstm_n[0], state_mlstm_m[0]),
                       (cache_fox_k[0], cache_fox_v[0], cache_fox_logf[0]))
    return (y_p, y_s) + tuple(a[None] for a in st_p) + tuple(a[None] for a in st_s)
```

```python
import functools

import jax
import jax.numpy as jnp
from jax import lax
from jax.experimental import pallas as pl
from jax.experimental.pallas import tpu as pltpu

M_HEADS = 4
M_DQK = 256
M_DV = 256
M_WIDTH = M_HEADS * M_DV
F_HEADS = 8
F_HEAD_DIM = 128
F_WIDTH = F_HEADS * F_HEAD_DIM
EPS = 1e-6
N_GATE_ROWS = 2 * M_HEADS + F_HEADS
LANES = 128
assert F_HEAD_DIM == LANES
MLSTM_BLOCK = 256
ATTN_BLOCK = 512
ATTN_TRIP = 6
LOG2E = 1.4426950408889634
VMEM_LIMIT_BYTES = 56 * 1024 * 1024

f32 = jnp.float32
bf16 = jnp.bfloat16

_NT = (((1,), (1,)), ((), ()))
_TN = (((0,), (0,)), ((), ()))


def _tile(n, pref):
    t = min(n, pref)
    while n % t:
        t //= 2
    return t


def _params(*sem):
    return pltpu.CompilerParams(dimension_semantics=sem, vmem_limit_bytes=VMEM_LIMIT_BYTES)


def _log_sigmoid(z):
    return jnp.minimum(z, 0.0) - jnp.log1p(jnp.exp(-jnp.abs(z)))


def _cumsum_lanes(x):
    rows, n = x.shape
    pad = -rows % 16
    if rows == 1:
        x16 = jnp.broadcast_to(x, (16, n))
    elif pad == 0:
        x16 = x
    else:
        x16 = jnp.concatenate([x, jnp.zeros((pad, n), f32)], axis=0)
    r = lax.broadcasted_iota(jnp.int32, (n, n), 0)
    c = lax.broadcasted_iota(jnp.int32, (n, n), 1)
    u = jnp.where(r <= c, 1.0, 0.0).astype(bf16)
    hi = x16.astype(bf16)
    rem = x16 - hi.astype(f32)
    mid = rem.astype(bf16)
    lo = (rem - mid.astype(f32)).astype(bf16)
    out = (jnp.dot(hi, u, preferred_element_type=f32) + jnp.dot(mid, u, preferred_element_type=f32)
           + jnp.dot(lo, u, preferred_element_type=f32))
    return out[:rows]


def _norm_kernel(x_ref, g_ref, wm_ref, wf_ref, xn_ref, gt_ref):
    x = x_ref[...]
    y = x * lax.rsqrt(jnp.mean(x * x, axis=-1, keepdims=True) + EPS) * g_ref[...]
    xn = y.astype(bf16)
    xn_ref[...] = xn
    wg = jnp.concatenate([wm_ref[...], wf_ref[...]], axis=0).astype(bf16)
    gt_ref[...] = lax.dot_general(wg, xn, _NT, preferred_element_type=f32)


def _norm_gates(x, g, w_t, mlstm_gate_row, fox_gate_row):
    t, d = x.shape
    tm = _tile(t, 512)
    assert 2 * M_HEADS == F_HEADS and mlstm_gate_row % F_HEADS == 0 and fox_gate_row % F_HEADS == 0
    return pl.pallas_call(
        _norm_kernel,
        grid=(t // tm,),
        in_specs=[pl.BlockSpec((tm, d), lambda i: (i, 0)),
                  pl.BlockSpec((1, d), lambda i: (0, 0)),
                  pl.BlockSpec((F_HEADS, d), lambda i: (mlstm_gate_row // F_HEADS, 0)),
                  pl.BlockSpec((F_HEADS, d), lambda i: (fox_gate_row // F_HEADS, 0))],
        out_specs=[pl.BlockSpec((tm, d), lambda i: (i, 0)),
                   pl.BlockSpec((N_GATE_ROWS, tm), lambda i: (0, i))],
        out_shape=[jax.ShapeDtypeStruct((t, d), bf16), jax.ShapeDtypeStruct((N_GATE_ROWS, t), f32)],
        compiler_params=_params("parallel"),
        name="norm_gates",
    )(x, g, w_t, w_t)


def _mm_kernel(a_ref, w_ref, o_ref):
    o_ref[...] = lax.dot_general(a_ref[...], w_ref[...], _NT, preferred_element_type=f32).astype(o_ref.dtype)


def _matmul_t(a, w_t, row0, n, name):
    t, k = a.shape
    tm = _tile(t, 2048)
    tn = _tile(n, 1024)
    assert row0 % tn == 0
    return pl.pallas_call(
        _mm_kernel,
        grid=(t // tm, n // tn),
        in_specs=[pl.BlockSpec((tm, k), lambda i, j: (i, 0)),
                  pl.BlockSpec((tn, k), lambda i, j: (row0 // tn + j, 0))],
        out_specs=pl.BlockSpec((tm, tn), lambda i, j: (i, j)),
        out_shape=jax.ShapeDtypeStruct((t, n), bf16),
        compiler_params=_params("parallel", "arbitrary"),
        name=name,
    )(a, w_t)


def _mm_heads_kernel(a_ref, w_ref, o16_ref, *rows_ref):
    r = lax.dot_general(a_ref[...], w_ref[...], _NT, preferred_element_type=f32)
    tm = a_ref.shape[0]
    for h in range(F_HEADS):
        head = r[:, h * F_HEAD_DIM:(h + 1) * F_HEAD_DIM]
        o16_ref[h] = head.astype(bf16)
        for ref in rows_ref:
            ref[pl.ds(h, tm, stride=F_HEADS), :] = head


def _matmul_heads_t(a, w_t, row0, name, *, state_rows):
    t, k = a.shape
    assert row0 % F_WIDTH == 0
    tm = _tile(t, 1024)
    out_specs = [pl.BlockSpec((F_HEADS, tm, F_HEAD_DIM), lambda i: (0, i, 0))]
    out_shape = [jax.ShapeDtypeStruct((F_HEADS, t, F_HEAD_DIM), bf16)]
    if state_rows:
        out_specs.append(pl.BlockSpec((tm * F_HEADS, F_HEAD_DIM), lambda i: (i, 0)))
        out_shape.append(jax.ShapeDtypeStruct((t * F_HEADS, F_HEAD_DIM), f32))
    return pl.pallas_call(
        _mm_heads_kernel,
        grid=(t // tm,),
        in_specs=[pl.BlockSpec((tm, k), lambda i: (i, 0)),
                  pl.BlockSpec((F_WIDTH, k), lambda i: (row0 // F_WIDTH, 0))],
        out_specs=out_specs,
        out_shape=out_shape,
        compiler_params=_params("parallel"),
        name=name,
    )(a, w_t)


def _scan_kernel(x_ref, bias_ref, logf_ref, cum_ref, carry_ref, *, apply_log_sigmoid):
    @pl.when(pl.program_id(1) == 0)
    def _():
        carry_ref[...] = jnp.zeros_like(carry_ref)

    x = x_ref[...]
    if apply_log_sigmoid:
        x = _log_sigmoid(x + bias_ref[...])
    logf_ref[...] = x
    cum = _cumsum_lanes(x) + carry_ref[:, :1]
    cum_ref[...] = cum * LOG2E
    carry_ref[...] = jnp.broadcast_to(cum[:, -1:], carry_ref.shape)


def _scan_rows(x, bias, *, row_block, rows, n_streams, apply_log_sigmoid):
    total = x.shape[1]
    s = total // n_streams
    tb = _tile(s, 512)
    nb = s // tb
    spec = pl.BlockSpec((rows, tb), lambda b, j: (0, b * nb + j))
    return pl.pallas_call(
        functools.partial(_scan_kernel, apply_log_sigmoid=apply_log_sigmoid),
        grid=(n_streams, nb),
        in_specs=[pl.BlockSpec((rows, tb), lambda b, j: (row_block, b * nb + j)),
                  pl.BlockSpec((rows, 1), lambda b, j: (0, 0))],
        out_specs=[spec, spec],
        out_shape=[jax.ShapeDtypeStruct((rows, total), f32)] * 2,
        scratch_shapes=[pltpu.VMEM((rows, LANES), f32)],
        compiler_params=_params("arbitrary", "arbitrary"),
        name="logf_scan",
    )(x, bias)


def _mlstm_kernel(bias_ref, q_ref, k_ref, v_ref, o_ref, gates_ref, nh_ref, c0_ref, n0_ref, m0_ref,
                  h_ref, c_out_ref, n_out_ref, m_out_ref, caug_ref, m_ref):
    blk = pl.program_id(1)
    n_blk = pl.num_programs(1)
    L = q_ref.shape[0]

    @pl.when(blk == 0)
    def _():
        lane = lax.broadcasted_iota(jnp.int32, (M_HEADS, M_DQK, LANES), 2)
        caug_ref[:, :, :M_DV] = c0_ref[...]
        caug_ref[:, :, M_DV:] = jnp.where(lane == 0, n0_ref[...], 0.0)
        m_ref[...] = m0_ref[...]

    gates = gates_ref[...]
    ig_all = gates[:M_HEADS] + bias_ref[:M_HEADS]
    lf_all = _log_sigmoid(gates[M_HEADS:2 * M_HEADS] + bias_ref[M_HEADS:])
    a_all = ig_all - _cumsum_lanes(lf_all)

    t_idx = lax.broadcasted_iota(jnp.int32, (L, L), 0)
    s_idx = lax.broadcasted_iota(jnp.int32, (L, L), 1)
    causal = s_idx <= t_idx
    diag = s_idx == t_idx
    ones_col = jnp.where(lax.broadcasted_iota(jnp.int32, (L, LANES), 1) == 0, 1.0, 0.0).astype(bf16)

    for h in range(M_HEADS):
        cols = slice(h * M_DV, (h + 1) * M_DV)
        m0 = m_ref[h:h + 1, :1]
        ig = ig_all[h:h + 1]
        lf = lf_all[h:h + 1]
        a_row = a_all[h:h + 1]
        a_mat = jnp.where(causal, a_row, -jnp.inf)
        g_col = jnp.maximum(m0, jnp.max(a_mat, axis=1, keepdims=True))
        b_col = jnp.sum(jnp.where(causal, lf, 0.0), axis=1, keepdims=True)
        w_intra = jnp.exp(a_mat - g_col)
        w_inter = jnp.exp(m0 - g_col)

        q = q_ref[:, cols]
        k = k_ref[:, cols]
        v_aug = jnp.concatenate([v_ref[:, cols], ones_col], axis=1)
        c_aug = caug_ref[h]

        s = lax.dot_general(q, k, _NT, preferred_element_type=f32)
        sw = (s * w_intra).astype(bf16)
        num_aug = (jnp.dot(sw, v_aug, preferred_element_type=f32)
                   + w_inter * jnp.dot(q, c_aug.astype(bf16), preferred_element_type=f32))
        num = num_aug[:, :M_DV]
        den = num_aug[:, M_DV:M_DV + 1]
        den = jnp.maximum(jnp.abs(den), jnp.exp(-(b_col + g_col)))
        hh = num / den
        hh = hh * lax.rsqrt(jnp.mean(hh * hh, axis=-1, keepdims=True) + EPS)
        hh = hh * nh_ref[:, cols] * jax.nn.sigmoid(o_ref[:, cols].astype(f32))
        h_ref[:, cols] = hh.astype(h_ref.dtype)

        g_end = jnp.maximum(m0, jnp.max(a_row, axis=1, keepdims=True))
        b_end = jnp.sum(lf, axis=1, keepdims=True)
        ig_col = jnp.sum(jnp.where(diag, ig, 0.0), axis=1, keepdims=True)
        w_tok = jnp.exp(ig_col - b_col - g_end)
        w_state = jnp.exp(m0 - g_end)
        kw = (k.astype(f32) * w_tok).astype(bf16)
        caug_ref[h] = w_state * c_aug + lax.dot_general(kw, v_aug, _TN, preferred_element_type=f32)
        m_ref[h:h + 1, :] = jnp.broadcast_to(b_end + g_end, (1, LANES))

    @pl.when(blk == n_blk - 1)
    def _():
        c_out_ref[...] = caug_ref[:, :, :M_DV]
        n_out_ref[...] = caug_ref[:, :, M_DV:]
        m_out_ref[...] = m_ref[...]


def _mlstm(z, gates_t, gate_bias, norm_h, c0, n0, m0, *, n_streams):
    t = z.shape[0]
    s = t // n_streams
    L = _tile(s, MLSTM_BLOCK)
    nb = s // L
    gates3 = gates_t.reshape(N_GATE_ROWS, n_streams * nb, L).transpose(1, 0, 2)
    n0c = n0.reshape(n_streams, M_HEADS, M_DQK, 1)
    m0b = jnp.broadcast_to(m0.reshape(n_streams, M_HEADS, 1), (n_streams, M_HEADS, LANES))

    def zcols(group):
        return pl.BlockSpec((L, M_WIDTH), lambda b, c: (b * nb + c, group))

    def state(*tail):
        return pl.BlockSpec((None, M_HEADS) + tail, lambda b, c: (b, 0) + (0,) * len(tail))

    return pl.pallas_call(
        _mlstm_kernel,
        grid=(n_streams, nb),
        in_specs=[pl.BlockSpec((2 * M_HEADS, 1), lambda b, c: (0, 0)),
                  zcols(0), zcols(1), zcols(2), zcols(3),
                  pl.BlockSpec((None, N_GATE_ROWS, L), lambda b, c: (b * nb + c, 0, 0)),
                  pl.BlockSpec((1, M_WIDTH), lambda b, c: (0, 0)),
                  state(M_DQK, M_DV), state(M_DQK, 1), state(LANES)],
        out_specs=[pl.BlockSpec((L, M_WIDTH), lambda b, c: (b * nb + c, 0)),
                   state(M_DQK, M_DV), state(M_DQK, LANES), state(LANES)],
        out_shape=[jax.ShapeDtypeStruct((t, M_WIDTH), bf16),
                   jax.ShapeDtypeStruct((n_streams, M_HEADS, M_DQK, M_DV), f32),
                   jax.ShapeDtypeStruct((n_streams, M_HEADS, M_DQK, LANES), f32),
                   jax.ShapeDtypeStruct((n_streams, M_HEADS, LANES), f32)],
        scratch_shapes=[pltpu.VMEM((M_HEADS, M_DQK, M_DV + LANES), f32), pltpu.VMEM((M_HEADS, LANES), f32)],
        compiler_params=_params("parallel", "arbitrary"),
        name="mlstm",
    )(gate_bias, z, z, z, z, gates3, norm_h, c0, n0c, m0b)


def _fox_prompt_kernel(q_ref, k_ref, v_ref, cum_ref, o_ref,
                       sa_ref, sb_ref, pa_ref, pb_ref, xa_ref, xb_ref, m_ref, l_ref, acc_ref):
    i = pl.program_id(2)
    tq = o_ref.shape[0]
    tk = sa_ref.shape[1]
    assert tq == tk

    def q_rows(r):
        return q_ref[pl.ds(pl.multiple_of(r * tq, tq), tq), :]

    def k_rows(ref, j):
        return ref[pl.ds(pl.multiple_of(j * tk, tk), tk), :]

    def causal(s, j):
        ahead = (lax.broadcasted_iota(jnp.int32, (tq, tk), 1) - lax.broadcasted_iota(jnp.int32, (tq, tk), 0))
        return jnp.where(ahead <= i * tq - j * tk, s, -jnp.inf)

    def scores(j, s_ref, x_ref, masked, q_row=i):
        s = lax.dot_general(q_rows(q_row), k_rows(k_ref, j), _NT, preferred_element_type=f32) - cum_ref[j]
        if masked:
            s = causal(s, j)
        s_ref[...] = s
        x_ref[...] = jnp.broadcast_to(jnp.max(s, axis=1, keepdims=True), x_ref.shape)

    def values(j, p_ref):
        return jnp.dot(p_ref[...], k_rows(v_ref, j), preferred_element_type=f32)

    def softmax(s_ref, x_ref, p_ref, pv_prev, mask_block=None):
        s = s_ref[...]
        if mask_block is None:
            x = x_ref[...]
        else:
            s = causal(s, mask_block)
            x = jnp.max(s, axis=1, keepdims=True)
        m_old = m_ref[...]
        m_new = jnp.maximum(m_old, x)
        alpha = jnp.exp2(m_old - m_new)
        p = jnp.exp2((s - jnp.concatenate([m_new] * (tk // LANES), axis=1)).astype(bf16))
        p_ref[...] = p
        l_ref[...] = alpha * l_ref[...] + sum(p[:, c:c + LANES] for c in range(0, tk, LANES)).astype(f32)
        acc_ref[...] = alpha * (acc_ref[...] + pv_prev)
        m_ref[...] = m_new

    m_ref[...] = jnp.full(m_ref.shape, -jnp.inf, f32)
    l_ref[...] = jnp.zeros(l_ref.shape, f32)
    acc_ref[...] = jnp.zeros(acc_ref.shape, f32)
    pb_ref[...] = jnp.zeros(pb_ref.shape, bf16)

    @pl.when(i == 0)
    def _():
        scores(0, sa_ref, xa_ref, False)

    def pair(j0):
        scores(j0 + 1, sb_ref, xb_ref, False)
        softmax(sa_ref, xa_ref, pa_ref, values(jnp.maximum(j0 - 1, 0), pb_ref))
        scores(j0 + 2, sa_ref, xa_ref, False)
        softmax(sb_ref, xb_ref, pb_ref, values(j0, pa_ref))

    def long_trip(jj, carry):
        for u in range(0, ATTN_TRIP, 2):
            pair(ATTN_TRIP * jj + u)
        return carry

    def short_trip(jj, carry):
        pair(ATTN_TRIP * (i // ATTN_TRIP) + 2 * jj)
        return carry

    lax.fori_loop(0, i // ATTN_TRIP, long_trip, 0)
    lax.fori_loop(0, (i % ATTN_TRIP) // 2, short_trip, 0)

    def scores_of_next_row():
        scores(0, sa_ref, xa_ref, False, q_row=jnp.minimum(i + 1, pl.num_programs(2) - 1))

    @pl.when(i % 2 == 0)
    def _():
        softmax(sa_ref, xa_ref, pa_ref, values(jnp.maximum(i - 1, 0), pb_ref), mask_block=i)
        scores_of_next_row()
        acc_ref[...] += values(i, pa_ref)

    @pl.when(i % 2 == 1)
    def _():
        scores(i, sb_ref, xb_ref, True)
        softmax(sa_ref, xa_ref, pa_ref, values(jnp.maximum(i - 2, 0), pb_ref))
        scores_of_next_row()
        softmax(sb_ref, xb_ref, pb_ref, values(i - 1, pa_ref))
        acc_ref[...] += values(i, pb_ref)

    o_ref[...] = (acc_ref[...] / jnp.sum(l_ref[...], axis=1, keepdims=True)).astype(o_ref.dtype)


def _fox_prompt(q, k, v, cum_t, *, n_streams):
    t = q.shape[1]
    s = t // n_streams
    tq = tk = _tile(s, ATTN_BLOCK)
    nq = nk = s // tk
    cum4 = cum_t.reshape(F_HEADS, n_streams * nk, 1, tk)
    head_rows = pl.BlockSpec((None, s, F_HEAD_DIM), lambda b, h, i: (h, b, 0))
    return pl.pallas_call(
        _fox_prompt_kernel,
        grid=(n_streams, F_HEADS, nq),
        in_specs=[head_rows, head_rows, head_rows,
                  pl.BlockSpec((None, nk, 1, tk), lambda b, h, i: (h, b, 0, 0))],
        out_specs=pl.BlockSpec((tq, F_HEAD_DIM), lambda b, h, i: (b * nq + i, h)),
        out_shape=jax.ShapeDtypeStruct((t, F_WIDTH), bf16),
        scratch_shapes=[pltpu.VMEM((tq, tk), f32)] * 2 + [pltpu.VMEM((tq, tk), bf16)] * 2
        + [pltpu.VMEM((tq, LANES), f32)] * 5,
        compiler_params=_params("parallel", "parallel", "arbitrary"),
        name="fox_prompt",
    )(q, k, v, cum4)


def _fox_decode_kernel(bias_ref, q_ref, kn_ref, vn_ref, kc_hbm, vc_hbm, cumc_ref, fg_ref, o_ref, logf_ref,
                       kbuf_ref, vbuf_ref, sem_ref):
    stream, head = pl.program_id(0), pl.program_id(1)
    n_heads = pl.num_programs(1)
    step = stream * n_heads + head
    slot = step % 2

    def fetch(s, h, into):
        return (pltpu.make_async_copy(kc_hbm.at[s, :, h, :], kbuf_ref.at[into], sem_ref.at[0, into]),
                pltpu.make_async_copy(vc_hbm.at[s, :, h, :], vbuf_ref.at[into], sem_ref.at[1, into]))

    @pl.when(step == 0)
    def _():
        for copy in fetch(stream, head, slot):
            copy.start()

    @pl.when(step + 1 < pl.num_programs(0) * n_heads)
    def _():
        for copy in fetch((step + 1) // n_heads, (step + 1) % n_heads, 1 - slot):
            copy.start()

    for copy in fetch(stream, head, slot):
        copy.wait()

    L = q_ref.shape[0]
    q = q_ref[...]
    logf = _log_sigmoid(fg_ref[...] + bias_ref[head])
    logf_ref[...] = logf
    cum_new = _cumsum_lanes(logf) * LOG2E
    cum_c = cumc_ref[...]
    cum_c = cum_c - cum_c[:, -1:]

    kc = kbuf_ref[slot].astype(bf16)
    vc = vbuf_ref[slot].astype(bf16)
    s_c = lax.dot_general(q, kc, _NT, preferred_element_type=f32) - cum_c
    s_n = lax.dot_general(q, kn_ref[...], _NT, preferred_element_type=f32) - cum_new
    row = lax.broadcasted_iota(jnp.int32, (L, L), 0)
    col = lax.broadcasted_iota(jnp.int32, (L, L), 1)
    s_n = jnp.where(col <= row, s_n, -jnp.inf)
    m = jnp.maximum(jnp.max(s_c, axis=1, keepdims=True), jnp.max(s_n, axis=1, keepdims=True))
    p_c = jnp.exp2(s_c - m)
    p_n = jnp.exp2(s_n - m)
    l = jnp.sum(p_c, axis=1, keepdims=True) + jnp.sum(p_n, axis=1, keepdims=True)
    acc = (jnp.dot(p_c.astype(bf16), vc, preferred_element_type=f32)
           + jnp.dot(p_n.astype(bf16), vn_ref[...], preferred_element_type=f32))
    o_ref[...] = (acc / l).astype(o_ref.dtype)


def _fox_decode(q, k, v, gates_t, bias, cache_k, cache_v, cum_cache, *, n_streams):
    t = q.shape[1]
    L = t // n_streams
    p = cache_k.shape[1]
    gates4 = gates_t.reshape(N_GATE_ROWS, n_streams, 1, L)
    head_rows = pl.BlockSpec((None, L, F_HEAD_DIM), lambda b, h: (h, b, 0))
    return pl.pallas_call(
        _fox_decode_kernel,
        grid=(n_streams, F_HEADS),
        in_specs=[pl.BlockSpec(memory_space=pltpu.SMEM),
                  head_rows, head_rows, head_rows,
                  pl.BlockSpec(memory_space=pl.ANY), pl.BlockSpec(memory_space=pl.ANY),
                  pl.BlockSpec((None, 1, p), lambda b, h: (b * F_HEADS + h, 0, 0)),
                  pl.BlockSpec((None, None, 1, L), lambda b, h: (2 * M_HEADS + h, b, 0, 0))],
        out_specs=[pl.BlockSpec((L, F_HEAD_DIM), lambda b, h: (b, h)),
                   pl.BlockSpec((None, None, 1, L), lambda b, h: (b, h, 0, 0))],
        out_shape=[jax.ShapeDtypeStruct((t, F_WIDTH), bf16),
                   jax.ShapeDtypeStruct((n_streams, F_HEADS, 1, L), f32)],
        scratch_shapes=[pltpu.VMEM((2, p, F_HEAD_DIM), f32), pltpu.VMEM((2, p, F_HEAD_DIM), f32),
                        pltpu.SemaphoreType.DMA((2, 2))],
        compiler_params=_params("arbitrary", "arbitrary"),
        name="fox_decode",
    )(bias, q, k, v, cache_k, cache_v, cum_cache, gates4)


def _merge_kernel(x_ref, ha_ref, hb_ref, ga_ref, gb_ref, wa_ref, wb_ref, wo_ref, g_ref, x1_ref, hn_ref):
    pa = jnp.dot(ha_ref[...], wa_ref[...], preferred_element_type=f32)
    pb = jnp.dot(hb_ref[...], wb_ref[...], preferred_element_type=f32)
    merged = (jax.nn.sigmoid(ga_ref[...].astype(f32)) * pa + jax.nn.sigmoid(gb_ref[...].astype(f32)) * pb)
    x1 = x_ref[...] + jnp.dot(merged.astype(bf16), wo_ref[...], preferred_element_type=f32)
    x1_ref[...] = x1
    hn = x1 * lax.rsqrt(jnp.mean(x1 * x1, axis=-1, keepdims=True) + EPS) * g_ref[...]
    hn_ref[...] = hn.astype(hn_ref.dtype)


def _merge(x, h_a, h_b, z, w_a, w_b, w_o, g, *, gate_col0):
    t, d = x.shape
    tm = _tile(t, 256)
    ga_blk = gate_col0 // d
    row = lambda i: (i, 0)
    fixed = lambda i: (0, 0)
    resident = functools.partial(pl.BlockSpec, index_map=fixed, pipeline_mode=pl.Buffered(1))
    return pl.pallas_call(
        _merge_kernel,
        grid=(t // tm,),
        in_specs=[pl.BlockSpec((tm, d), row),
                  pl.BlockSpec((tm, M_WIDTH), row),
                  pl.BlockSpec((tm, F_WIDTH), row),
                  pl.BlockSpec((tm, d), lambda i: (i, ga_blk)),
                  pl.BlockSpec((tm, d), lambda i: (i, ga_blk + 1)),
                  resident(w_a.shape), resident(w_b.shape), resident(w_o.shape),
                  pl.BlockSpec((1, d), fixed)],
        out_specs=[pl.BlockSpec((tm, d), row), pl.BlockSpec((tm, d), row)],
        out_shape=[jax.ShapeDtypeStruct((t, d), f32), jax.ShapeDtypeStruct((t, d), bf16)],
        compiler_params=_params("parallel"),
        name="merge",
    )(x, h_a, h_b, z, z, w_a, w_b, w_o, g)


def _ffn_kernel(hn_ref, x1_ref, wu_ref, wd_ref, g_ref, y_ref):
    f = pl.program_id(1)

    @pl.when(f == 0)
    def _():
        y_ref[...] = x1_ref[...]

    u = jnp.maximum(jnp.dot(hn_ref[...], wu_ref[...], preferred_element_type=f32), 0.0)
    y_ref[...] += jnp.dot((u * u).astype(bf16), wd_ref[...], preferred_element_type=f32)

    @pl.when(f == pl.num_programs(1) - 1)
    def _():
        x2 = y_ref[...]
        y_ref[...] = x2 * lax.rsqrt(jnp.mean(x2 * x2, axis=-1, keepdims=True) + EPS) * g_ref[...]


def _ffn(hn, x1, w_up, w_down, g):
    t, d = x1.shape
    dff = w_up.shape[1]
    tm = _tile(t, 512)
    tf = _tile(dff, 1024)
    return pl.pallas_call(
        _ffn_kernel,
        grid=(t // tm, dff // tf),
        in_specs=[pl.BlockSpec((tm, d), lambda i, f: (i, 0)),
                  pl.BlockSpec((tm, d), lambda i, f: (i, 0)),
                  pl.BlockSpec((d, tf), lambda i, f: (0, f)),
                  pl.BlockSpec((tf, d), lambda i, f: (f, 0)),
                  pl.BlockSpec((1, d), lambda i, f: (0, 0))],
        out_specs=pl.BlockSpec((tm, d), lambda i, f: (i, 0)),
        out_shape=jax.ShapeDtypeStruct((t, d), f32),
        compiler_params=_params("parallel", "arbitrary"),
        name="ffn",
    )(hn, x1, w_up, w_down, g)


def _layer(x3, w, mstate, fox_past):
    nstr, frames, d = x3.shape
    t = nstr * frames
    x = x3.reshape(t, d)
    xn, gates_t = _norm_gates(x, w["norm_mix"], w["w_in_t"], *w["gate_rows"])
    w_proj_t = w["w_proj_t"]
    n_rest = w_proj_t.shape[0] - 3 * F_WIDTH
    k16, k32 = _matmul_heads_t(xn, w_proj_t, 0, "proj_k", state_rows=True)
    v16, v32 = _matmul_heads_t(xn, w_proj_t, F_WIDTH, "proj_v", state_rows=True)
    z = _matmul_t(xn, w_proj_t, 2 * F_WIDTH, n_rest, "proj_rest")
    (q16,) = _matmul_heads_t(xn, w_proj_t, 2 * F_WIDTH + n_rest, "proj_q", state_rows=False)

    c0, n0, m0 = mstate
    h_a, c_new, n_slab, m_slab = _mlstm(z, gates_t, w["mlstm_bias"], w["norm_mlstm_h"], c0, n0, m0,
                                        n_streams=nstr)
    n_new = n_slab[..., 0]
    m_new = m_slab[:, :, 0]

    if fox_past is None:
        logf_t, cum_t = _scan_rows(gates_t, w["b_fox_f_col"], row_block=1, rows=F_HEADS, n_streams=nstr,
                                   apply_log_sigmoid=True)
        h_b = _fox_prompt(q16, k16, v16, cum_t, n_streams=nstr)
        logf = logf_t.T.reshape(nstr, frames, F_HEADS)
    else:
        ck, cv, clf = fox_past
        p = ck.shape[1]
        clf_t = jnp.transpose(clf, (0, 2, 1)).reshape(nstr * F_HEADS, p)
        _, cum_c = _scan_rows(clf_t, jnp.zeros((nstr * F_HEADS, 1), f32), row_block=0, rows=nstr * F_HEADS,
                              n_streams=1, apply_log_sigmoid=False)
        h_b, logf4 = _fox_decode(q16, k16, v16, gates_t, w["b_fox_f"], ck, cv, cum_c.reshape(nstr * F_HEADS, 1, p),
                                 n_streams=nstr)
        logf = jnp.transpose(logf4[:, :, 0, :], (0, 2, 1))

    x1, hn = _merge(x, h_a, h_b, z, w["w_branch_a"], w["w_branch_b"], w["w_out"], w["norm_ffn"],
                    gate_col0=w["gate_col0"])
    y = _ffn(hn, x1, w["w_up"], w["w_down"], w["norm_final"])

    k_rows = k32.reshape(nstr, frames, F_HEADS, F_HEAD_DIM)
    v_rows = v32.reshape(nstr, frames, F_HEADS, F_HEAD_DIM)
    return y.reshape(nstr, frames, d), (k_rows, v_rows, logf, c_new, n_new, m_new)


def _regroup_kernel(table_ref, w_ref, o_ref):
    sid = table_ref[pl.program_id(0), 1]
    scale = jnp.where(sid == 1, M_DQK ** -0.5, jnp.where(sid == 2, F_HEAD_DIM ** -0.5 * LOG2E, 1.0))
    o_ref[...] = (w_ref[...] * scale).astype(bf16)


def _regroup_w_in_t(w_t, offs):
    d = w_t.shape[1]
    mq, mk, mv, mo, mi, mf, fq, fk, fv, ff, ga, gb = range(12)
    order = ((fk, 0), (fv, 0), (mq, 0), (mk, 1), (mv, 0), (mo, 0), (ga, 0), (gb, 0), (fq, 2))
    tr = _tile(d, F_WIDTH)
    table = []
    for seg, scale_id in order:
        start, stop = offs[seg], offs[seg + 1]
        assert start % 8 == 0 and (stop - start) % tr == 0
        table += [(r // 8, scale_id) for r in range(start, stop, tr)]
    table = jnp.asarray(table, jnp.int32)
    return pl.pallas_call(
        _regroup_kernel,
        grid_spec=pltpu.PrefetchScalarGridSpec(
            num_scalar_prefetch=1,
            grid=(table.shape[0],),
            in_specs=[pl.BlockSpec((pl.Element(tr), pl.Element(d)),
                                   lambda t, tbl: (pl.multiple_of(tbl[t, 0] * 8, 8), 0))],
            out_specs=pl.BlockSpec((tr, d), lambda t, tbl: (t, 0))),
        out_shape=jax.ShapeDtypeStruct((table.shape[0] * tr, d), bf16),
        compiler_params=_params("parallel"),
        name="regroup_w_in",
    )(table, w_t)


def _prepare_weights(norm_mix, w_in, b_mlstm_i, b_mlstm_f, b_fox_f, norm_mlstm_h, w_branch_a, w_branch_b,
                     w_out, norm_ffn, w_up, w_down, norm_final):
    d = w_in.shape[0]
    sizes = (M_HEADS * M_DQK, M_HEADS * M_DQK, M_WIDTH, M_WIDTH, M_HEADS, M_HEADS,
             F_WIDTH, F_WIDTH, F_WIDTH, F_HEADS, d, d)
    offs = [0]
    for s in sizes:
        offs.append(offs[-1] + s)
    w_t = w_in.T
    return {
        "norm_mix": norm_mix.reshape(1, d),
        "w_in_t": w_t,
        "gate_rows": (offs[4], offs[9]),
        "w_proj_t": _regroup_w_in_t(w_t, offs),
        "gate_col0": 4 * M_WIDTH,
        "mlstm_bias": jnp.concatenate([b_mlstm_i, b_mlstm_f]).astype(f32).reshape(2 * M_HEADS, 1),
        "b_fox_f": b_fox_f.astype(f32),
        "b_fox_f_col": b_fox_f.astype(f32).reshape(F_HEADS, 1),
        "norm_mlstm_h": norm_mlstm_h.reshape(1, M_WIDTH),
        "w_branch_a": w_branch_a.astype(bf16),
        "w_branch_b": w_branch_b.astype(bf16),
        "w_out": w_out.astype(bf16),
        "norm_ffn": norm_ffn.reshape(1, d),
        "w_up": w_up.astype(bf16),
        "w_down": w_down.astype(bf16),
        "norm_final": norm_final.reshape(1, d),
    }


def kernel(x_prompt, x_sample, cache_fox_k, cache_fox_v, cache_fox_logf, state_mlstm_c, state_mlstm_n, state_mlstm_m, norm_mix, w_in, b_mlstm_i, b_mlstm_f, b_fox_f, norm_mlstm_h, w_branch_a, w_branch_b, w_out, norm_ffn, w_up, w_down, norm_final):
    depth = w_in.shape[0]
    assert depth == 1, "the final norm is fused into the layer's FFN kernel"
    w = _prepare_weights(norm_mix[0], w_in[0], b_mlstm_i[0], b_mlstm_f[0], b_fox_f[0], norm_mlstm_h[0],
                         w_branch_a[0], w_branch_b[0], w_out[0], norm_ffn[0], w_up[0], w_down[0], norm_final)
    bp = x_prompt.shape[0]
    fresh = (jnp.zeros((bp, M_HEADS, M_DQK, M_DV), f32), jnp.zeros((bp, M_HEADS, M_DQK), f32),
             jnp.zeros((bp, M_HEADS), f32))
    y_p, st_p = _layer(x_prompt, w, fresh, None)
    y_s, st_s = _layer(x_sample, w, (state_mlstm_c[0], state_mlstm_n[0], state_mlstm_m[0]),
                       (cache_fox_k[0], cache_fox_v[0], cache_fox_logf[0]))
    return (y_p, y_s) + tuple(a[None] for a in st_p) + tuple(a[None] for a in st_s)
```
